```python
import math
import functools
import jax
import jax.numpy as jnp
from jax import lax
import numpy as np

D_MODEL = 1024
BATCH = 16
SEQ = 2048
DEPTH = 2
DEC_BATCH = 32
DEC_SEQ = 8
PAST_LEN = 16384
PAGE_SIZE = 128

N_AB_LAYERS = (DEPTH + 1) // 2
N_POOL_LAYERS = DEPTH // 2
DIFF_HEADS = 4
DIFF_HEAD_DIM = 64
DIFF_V_DIM = 2 * DIFF_HEAD_DIM
DIFF_WIDTH = DIFF_HEADS * DIFF_V_DIM
QK_WIDTH = DIFF_HEADS * 2 * DIFF_HEAD_DIM
ROPE_THETA = 10000.0
Q_BLOCK = 128
SSM_INNER = D_MODEL // 2
SSM_HEAD_DIM = 64
SSM_HEADS = SSM_INNER // SSM_HEAD_DIM
SSM_GROUPS = 2
SSM_HEADS_PER_GROUP = SSM_HEADS // SSM_GROUPS
SSM_STATE = 128
CONV_WIDTH = 4
CONV_CH = SSM_INNER + 2 * SSM_GROUPS * SSM_STATE
SSD_CHUNK = 128
IN_PROJ_WIDTH = 2 * QK_WIDTH + DIFF_WIDTH + SSM_INNER + CONV_CH + SSM_HEADS
MIX_OUT_WIDTH = DIFF_WIDTH + SSM_INNER
POOL_WINDOWS = (2, 4, 8, 16)
POOL_GROUPS = len(POOL_WINDOWS)
POOL_GROUP_CH = D_MODEL // POOL_GROUPS
POOL_STATE_LEN = max(POOL_WINDOWS) - 1
FFN_HIDDEN = 2816
RMS_EPS = 1e-6
N_PAGES = PAST_LEN // PAGE_SIZE
N_PHYS_PAGES = DEC_BATCH * N_PAGES + max(1, (DEC_BATCH * N_PAGES) // 4)

kernel_name = 'hybrid_diffattn_ssd_pool_decoder_step'


def rms_norm(x, w):
    xf = x.astype(jnp.float32)
    y = xf * lax.rsqrt(jnp.mean(xf * xf, axis=-1, keepdims=True) + RMS_EPS)
    return (y * w.astype(jnp.float32)).astype(x.dtype)


def swiglu(u, w_gate, w_up, w_down):
    return (jax.nn.silu(u @ w_gate) * (u @ w_up)) @ w_down


def rope(t, pos):
    inv = 1.0 / (ROPE_THETA ** (jnp.arange(0, DIFF_HEAD_DIM, 2, dtype=jnp.float32) / DIFF_HEAD_DIM))
    ang = pos.astype(jnp.float32)[:, None] * inv[None, :]
    ang = jnp.concatenate([ang, ang], axis=-1)
    cos = jnp.cos(ang)[None, :, None, None, :]
    sin = jnp.sin(ang)[None, :, None, None, :]
    tf = t.astype(jnp.float32)
    t1, t2 = jnp.split(tf, 2, axis=-1)
    rot = jnp.concatenate([-t2, t1], axis=-1)
    return (tf * cos + rot * sin).astype(t.dtype)


def diff_weights(s, lam):
    p = jax.nn.softmax(s, axis=-1)
    return p[:, :, 0] - lam * p[:, :, 1]


def prompt_diff_attention(q, k, v, lam):
    b, L = q.shape[:2]
    nb = L // Q_BLOCK
    scale = DIFF_HEAD_DIM ** -0.5
    qb = jnp.moveaxis(q.reshape((b, nb, Q_BLOCK) + q.shape[2:]), 1, 0)
    kpos = jnp.arange(L, dtype=jnp.int32)

    def one_block(args):
        qblk, i = args
        s = jnp.einsum('bqhcd,bkhcd->bhcqk', qblk, k).astype(jnp.float32) * scale
        qpos = i * Q_BLOCK + jnp.arange(Q_BLOCK, dtype=jnp.int32)
        s = jnp.where(kpos[None, :] <= qpos[:, None], s, -jnp.inf)
        w = diff_weights(s, lam)
        return jnp.einsum('bhqk,bkhe->bqhe', w.astype(v.dtype), v)

    out = lax.map(one_block, (qb, jnp.arange(nb, dtype=jnp.int32)))
    return jnp.moveaxis(out, 0, 1).reshape(b, L, DIFF_HEADS, DIFF_V_DIM)


def sample_diff_attention(q, k, v, lam, cache_k, cache_v, page_table, layer):
    b, T = q.shape[:2]
    scale = DIFF_HEAD_DIM ** -0.5
    k_past = cache_k[layer, page_table]
    v_past = cache_v[layer, page_table]
    past = k_past.shape[1] * k_past.shape[2]
    k_past = k_past.reshape((b, past) + k_past.shape[3:])
    v_past = v_past.reshape((b, past) + v_past.shape[3:])
    s_past = jnp.einsum('bqhcd,bkhcd->bhcqk', q, k_past).astype(jnp.float32) * scale
    s_new = jnp.einsum('bqhcd,bkhcd->bhcqk', q, k).astype(jnp.float32) * scale
    causal = jnp.tril(jnp.ones((T, T), dtype=bool))
    s_new = jnp.where(causal, s_new, -jnp.inf)
    w = diff_weights(jnp.concatenate([s_past, s_new], axis=-1), lam).astype(v.dtype)
    return (jnp.einsum('bhqk,bkhe->bqhe', w[..., :past], v_past)
            + jnp.einsum('bhqk,bkhe->bqhe', w[..., past:], v))


def causal_dwconv(u, buf, w, bias):
    L = u.shape[1]
    full = jnp.concatenate([buf.astype(u.dtype), u], axis=1)
    y = bias
    for j in range(CONV_WIDTH):
        y = y + full[:, j:j + L] * w[j]
    return y, full[:, -(CONV_WIDTH - 1):]


def ssd_scan(x, dt, a, bmat, cmat, h0):
    b, L = x.shape[:2]
    Q = SSD_CHUNK if L % SSD_CHUNK == 0 else L
    nc = L // Q

    def chunk(t):
        return t.reshape((b, nc, Q) + t.shape[2:])

    x, dt, bmat, cmat = chunk(x), chunk(dt), chunk(bmat), chunk(cmat)
    acs = jnp.cumsum(dt * a, axis=2)
    seg = acs[:, :, :, None] - acs[:, :, None, :]
    causal = jnp.tril(jnp.ones((Q, Q), dtype=bool))[:, :, None, None]
    decay = jnp.exp(jnp.where(causal, seg, -jnp.inf))
    cb = jnp.einsum('bctgn,bcsgn->bctsg', cmat, bmat)
    w = cb[..., None] * decay * dt[:, :, None]
    y_diag = jnp.einsum('bctsgj,bcsgjp->bctgjp', w, x)
    decay_end = jnp.exp(acs[:, :, -1:] - acs)
    xw = x * (decay_end * dt)[..., None]
    chunk_states = jnp.einsum('bcsgn,bcsgjp->bcgjpn', bmat, xw)
    chunk_decay = jnp.exp(acs[:, :, -1])

    def step(h, inp):
        dec, st = inp
        return h * dec[..., None, None] + st, h

    h_final, h_starts = lax.scan(step, h0, (jnp.moveaxis(chunk_decay, 1, 0), jnp.moveaxis(chunk_states, 1, 0)))
    h_starts = jnp.moveaxis(h_starts, 0, 1)
    y_off = jnp.einsum('bctgn,bcgjpn->bctgjp', cmat, h_starts) * jnp.exp(acs)[..., None]
    y = (y_diag + y_off).reshape((b, L) + y_diag.shape[3:])
    return y, h_final


def mixer_ab(u, pos, attn, conv_buf, h0, p, i, lambda_init):
    b, L, _ = u.shape
    proj = u @ p['ab_w_in'][i]
    cuts = [QK_WIDTH, 2 * QK_WIDTH, 2 * QK_WIDTH + DIFF_WIDTH,
            2 * QK_WIDTH + DIFF_WIDTH + SSM_INNER, 2 * QK_WIDTH + DIFF_WIDTH + SSM_INNER + CONV_CH]
    q, k, v, z, xbc, dt_raw = jnp.split(proj, cuts, axis=-1)
    q = rope(q.reshape(b, L, DIFF_HEADS, 2, DIFF_HEAD_DIM), pos)
    k = rope(k.reshape(b, L, DIFF_HEADS, 2, DIFF_HEAD_DIM), pos)
    v = v.reshape(b, L, DIFF_HEADS, DIFF_V_DIM)
    f32 = jnp.float32
    lam = (jnp.exp(jnp.dot(p['ab_lambda_q1'][i].astype(f32), p['ab_lambda_k1'][i].astype(f32)))
           - jnp.exp(jnp.dot(p['ab_lambda_q2'][i].astype(f32), p['ab_lambda_k2'][i].astype(f32)))
           + lambda_init)
    attn_out = attn(q, k, v, lam)
    attn_out = (rms_norm(attn_out, p['ab_subln_w'][i]) * (1.0 - lambda_init)).reshape(b, L, DIFF_WIDTH)
    xbc, new_conv = causal_dwconv(xbc, conv_buf, p['ab_conv_w'][i], p['ab_conv_b'][i])
    xbc = jax.nn.silu(xbc)
    xs, bm, cm = jnp.split(xbc, [SSM_INNER, SSM_INNER + SSM_GROUPS * SSM_STATE], axis=-1)
    G, J = SSM_GROUPS, SSM_HEADS_PER_GROUP
    dt = jax.nn.softplus(dt_raw.astype(f32) + p['ab_dt_bias'][i].astype(f32)).reshape(b, L, G, J)
    a = -jnp.exp(p['ab_a_log'][i].astype(f32)).reshape(G, J)
    xs32 = xs.astype(f32).reshape(b, L, G, J, SSM_HEAD_DIM)
    y, h_final = ssd_scan(xs32, dt, a,
                          bm.astype(f32).reshape(b, L, G, SSM_STATE),
                          cm.astype(f32).reshape(b, L, G, SSM_STATE),
                          h0.astype(f32).reshape(b, G, J, SSM_HEAD_DIM, SSM_STATE))
    y = y + p['ab_d_skip'][i].astype(f32).reshape(G, J)[..., None] * xs32
    y = y.reshape(b, L, SSM_INNER) * jax.nn.silu(z.astype(f32))
    gsz = SSM_INNER // G
    y = rms_norm(y.reshape(b, L, G, gsz), p['ab_ssm_norm_w'][i].reshape(G, gsz)).reshape(b, L, SSM_INNER)
    out = jnp.concatenate([attn_out, y.astype(u.dtype)], axis=-1) @ p['ab_w_out'][i]
    h_final = h_final.reshape(b, SSM_HEADS, SSM_HEAD_DIM, SSM_STATE).astype(h0.dtype)
    return out, k, v, new_conv, h_final


def pool_mix(u, pos, buf, pool_w, pool_scale):
    b, L, D = u.shape
    S = POOL_STATE_LEN
    full = jnp.concatenate([buf.astype(u.dtype), u], axis=1)
    cs = jnp.concatenate([jnp.zeros((b, 1, D), jnp.float32), jnp.cumsum(full.astype(jnp.float32), axis=1)], axis=1)
    end = cs[:, S + 1:S + 1 + L]
    groups = []
    for g, w in enumerate(POOL_WINDOWS):
        sl = slice(g * POOL_GROUP_CH, (g + 1) * POOL_GROUP_CH)
        win = end[..., sl] - cs[:, S + 1 - w:S + 1 - w + L, sl]
        cnt = jnp.minimum(w, pos + 1).astype(jnp.float32)[None, :, None]
        groups.append(win / cnt)
    pooled = jnp.stack(groups, axis=2) - u.astype(jnp.float32).reshape(b, L, POOL_GROUPS, POOL_GROUP_CH)
    mixed = jnp.einsum('blgc,gcd->blgd', pooled.astype(u.dtype), pool_w).reshape(b, L, D)
    return mixed * pool_scale, full[:, -S:]


def trunk(x, pos, attn_fns, conv_bufs, ssm_states, pool_bufs, p):
    new_k, new_v, new_conv, new_ssm, new_pool = [], [], [], [], []
    for layer in range(DEPTH):
        nw = p['norm_w'][layer]
        x = x + 0.5 * swiglu(rms_norm(x, nw[0]), p['ffn_w_gate'][layer, 0], p['ffn_w_up'][layer, 0], p['ffn_w_down'][layer, 0])
        u = rms_norm(x, nw[1])
        i = layer // 2
        if layer % 2 == 0:
            lambda_init = 0.8 - 0.6 * math.exp(-0.3 * layer)
            mix, k, v, cb, hs = mixer_ab(u, pos, attn_fns[i], conv_bufs[i], ssm_states[i], p, i, lambda_init)
            new_k.append(k)
            new_v.append(v)
            new_conv.append(cb)
            new_ssm.append(hs)
        else:
            mix, pb = pool_mix(u, pos, pool_bufs[i], p['pool_w'][i], p['pool_scale'][i])
            new_pool.append(pb)
        x = x + mix
        x = x + 0.5 * swiglu(rms_norm(x, nw[2]), p['ffn_w_gate'][layer, 1], p['ffn_w_up'][layer, 1], p['ffn_w_down'][layer, 1])
    y = rms_norm(x, p['final_norm_w'])
    return y, jnp.stack(new_k), jnp.stack(new_v), jnp.stack(new_conv), jnp.stack(new_ssm), jnp.stack(new_pool)


def setup_inputs(seed: int = 0) -> dict:
    key = jax.random.key(seed)
    ks = jax.random.split(key, 32)
    f32 = jnp.float32

    def nrm(k, shape, scale):
        return jax.random.normal(k, shape, f32) * scale

    def gain(k, shape):
        return 1.0 + 0.02 * jax.random.normal(k, shape, f32)

    x_prompt = nrm(ks[0], (BATCH, SEQ, D_MODEL), 1.0)
    x_sample = nrm(ks[1], (DEC_BATCH, DEC_SEQ, D_MODEL), 1.0)
    cache_k = nrm(ks[2], (N_AB_LAYERS, N_PHYS_PAGES, PAGE_SIZE, DIFF_HEADS, 2, DIFF_HEAD_DIM), 1.0)
    cache_v = nrm(ks[3], (N_AB_LAYERS, N_PHYS_PAGES, PAGE_SIZE, DIFF_HEADS, DIFF_V_DIM), 1.0)
    state_conv = nrm(ks[4], (N_AB_LAYERS, DEC_BATCH, CONV_WIDTH - 1, CONV_CH), 1.0)
    state_ssm = nrm(ks[5], (N_AB_LAYERS, DEC_BATCH, SSM_HEADS, SSM_HEAD_DIM, SSM_STATE), 0.1)
    state_pool = nrm(ks[6], (N_POOL_LAYERS, DEC_BATCH, POOL_STATE_LEN, D_MODEL), 1.0)
    perm = jax.random.permutation(ks[7], N_PHYS_PAGES)
    page_table = perm[:DEC_BATCH * N_PAGES].reshape(DEC_BATCH, N_PAGES).astype(jnp.int32)
    norm_w = gain(ks[8], (DEPTH, 3, D_MODEL))
    ffn_w_gate = nrm(ks[9], (DEPTH, 2, D_MODEL, FFN_HIDDEN), D_MODEL ** -0.5)
    ffn_w_up = nrm(ks[10], (DEPTH, 2, D_MODEL, FFN_HIDDEN), D_MODEL ** -0.5)
    ffn_w_down = nrm(ks[11], (DEPTH, 2, FFN_HIDDEN, D_MODEL), FFN_HIDDEN ** -0.5)
    ab_w_in = nrm(ks[12], (N_AB_LAYERS, D_MODEL, IN_PROJ_WIDTH), D_MODEL ** -0.5)
    ab_w_out = nrm(ks[13], (N_AB_LAYERS, MIX_OUT_WIDTH, D_MODEL), MIX_OUT_WIDTH ** -0.5)
    ab_conv_w = nrm(ks[14], (N_AB_LAYERS, CONV_WIDTH, CONV_CH), CONV_WIDTH ** -0.5)
    ab_conv_b = nrm(ks[15], (N_AB_LAYERS, CONV_CH), 0.01)
    dt0 = jnp.exp(jax.random.uniform(ks[16], (N_AB_LAYERS, SSM_HEADS), f32, math.log(1e-3), math.log(1e-1)))
    ab_dt_bias = dt0 + jnp.log(-jnp.expm1(-dt0))
    ab_a_log = jnp.log(jax.random.uniform(ks[17], (N_AB_LAYERS, SSM_HEADS), f32, 1.0, 16.0))
    ab_d_skip = gain(ks[18], (N_AB_LAYERS, SSM_HEADS))
    ab_ssm_norm_w = gain(ks[19], (N_AB_LAYERS, SSM_INNER))
    ab_lambda_q1 = nrm(ks[20], (N_AB_LAYERS, DIFF_HEAD_DIM), 0.1)
    ab_lambda_k1 = nrm(ks[21], (N_AB_LAYERS, DIFF_HEAD_DIM), 0.1)
    ab_lambda_q2 = nrm(ks[22], (N_AB_LAYERS, DIFF_HEAD_DIM), 0.1)
    ab_lambda_k2 = nrm(ks[23], (N_AB_LAYERS, DIFF_HEAD_DIM), 0.1)
    ab_subln_w = gain(ks[24], (N_AB_LAYERS, DIFF_V_DIM))
    pool_w = nrm(ks[25], (N_POOL_LAYERS, POOL_GROUPS, POOL_GROUP_CH, POOL_GROUP_CH), POOL_GROUP_CH ** -0.5)
    pool_scale = 0.1 * gain(ks[26], (N_POOL_LAYERS, D_MODEL))
    final_norm_w = gain(ks[27], (D_MODEL,))
    return {'x_prompt': x_prompt, 'x_sample': x_sample, 'cache_k': cache_k, 'cache_v': cache_v,
            'state_conv': state_conv, 'state_ssm': state_ssm, 'state_pool': state_pool,
            'page_table': page_table, 'norm_w': norm_w, 'ffn_w_gate': ffn_w_gate, 'ffn_w_up': ffn_w_up,
            'ffn_w_down': ffn_w_down, 'ab_w_in': ab_w_in, 'ab_w_out': ab_w_out, 'ab_conv_w': ab_conv_w,
            'ab_conv_b': ab_conv_b, 'ab_dt_bias': ab_dt_bias, 'ab_a_log': ab_a_log, 'ab_d_skip': ab_d_skip,
            'ab_ssm_norm_w': ab_ssm_norm_w, 'ab_lambda_q1': ab_lambda_q1, 'ab_lambda_k1': ab_lambda_k1,
            'ab_lambda_q2': ab_lambda_q2, 'ab_lambda_k2': ab_lambda_k2, 'ab_subln_w': ab_subln_w,
            'pool_w': pool_w, 'pool_scale': pool_scale, 'final_norm_w': final_norm_w}


def reference(x_prompt, x_sample, cache_k, cache_v, state_conv, state_ssm, state_pool, page_table,
              norm_w, ffn_w_gate, ffn_w_up, ffn_w_down, ab_w_in, ab_w_out, ab_conv_w, ab_conv_b,
              ab_dt_bias, ab_a_log, ab_d_skip, ab_ssm_norm_w, ab_lambda_q1, ab_lambda_k1,
              ab_lambda_q2, ab_lambda_k2, ab_subln_w, pool_w, pool_scale, final_norm_w):
    p = dict(norm_w=norm_w, ffn_w_gate=ffn_w_gate, ffn_w_up=ffn_w_up, ffn_w_down=ffn_w_down,
             ab_w_in=ab_w_in, ab_w_out=ab_w_out, ab_conv_w=ab_conv_w, ab_conv_b=ab_conv_b,
             ab_dt_bias=ab_dt_bias, ab_a_log=ab_a_log, ab_d_skip=ab_d_skip, ab_ssm_norm_w=ab_ssm_norm_w,
             ab_lambda_q1=ab_lambda_q1, ab_lambda_k1=ab_lambda_k1, ab_lambda_q2=ab_lambda_q2,
             ab_lambda_k2=ab_lambda_k2, ab_subln_w=ab_subln_w, pool_w=pool_w, pool_scale=pool_scale,
             final_norm_w=final_norm_w)
    bp, lp, _ = x_prompt.shape
    pos_p = jnp.arange(lp, dtype=jnp.int32)
    zero_conv = jnp.zeros((N_AB_LAYERS, bp, CONV_WIDTH - 1, CONV_CH), x_prompt.dtype)
    zero_ssm = jnp.zeros((N_AB_LAYERS, bp, SSM_HEADS, SSM_HEAD_DIM, SSM_STATE), x_prompt.dtype)
    zero_pool = jnp.zeros((N_POOL_LAYERS, bp, POOL_STATE_LEN, D_MODEL), x_prompt.dtype)
    attn_p = [prompt_diff_attention for _ in range(N_AB_LAYERS)]
    y_prompt, k_prompt, v_prompt, conv_prompt, ssm_prompt, pool_prompt = trunk(
        x_prompt, pos_p, attn_p, zero_conv, zero_ssm, zero_pool, p)
    past_len = page_table.shape[1] * cache_k.shape[2]
    pos_s = past_len + jnp.arange(x_sample.shape[1], dtype=jnp.int32)
    attn_s = [functools.partial(sample_diff_attention, cache_k=cache_k, cache_v=cache_v,
                                page_table=page_table, layer=i) for i in range(N_AB_LAYERS)]
    y_sample, k_sample, v_sample, conv_sample, ssm_sample, pool_sample = trunk(
        x_sample, pos_s, attn_s, state_conv, state_ssm, state_pool, p)
    return (y_prompt, y_sample, k_prompt, v_prompt, conv_prompt, ssm_prompt, pool_prompt,
            k_sample, v_sample, conv_sample, ssm_sample, pool_sample)
```

```python
import functools
import math

import numpy as np
import jax
import jax.numpy as jnp
from jax import lax
from jax.experimental import pallas as pl
from jax.experimental.pallas import tpu as pltpu

F32 = jnp.float32
BF16 = jnp.bfloat16

RMS_EPS = 1e-6
ROPE_THETA = 10000.0

DIFF_HEADS = 4
DIFF_HEAD_DIM = 64
HEAD_LANES = 2 * DIFF_HEAD_DIM
QK_WIDTH = DIFF_HEADS * HEAD_LANES
SSM_INNER = 512
SSM_HEAD_DIM = 64
SSM_HEADS = 8
SSM_GROUPS = 2
SSM_STATE = 128
GROUP_LANES = SSM_INNER // SSM_GROUPS
CONV_WIDTH = 4
CONV_CH = SSM_INNER + 2 * SSM_GROUPS * SSM_STATE
SSD_CHUNK = 128
POOL_WINDOWS = (2, 4, 8, 16)
POOL_GROUP_CH = 256
POOL_HALO = 16
POOL_STATE_LEN = 15
MAIN_PROJ = 2 * QK_WIDTH + QK_WIDTH + SSM_INNER + CONV_CH

LANES = 128
SUBLANES = 8
MXU_DIM = 256
VMEM_LIMIT_BIG = 58 * 1024 * 1024
VMEM_LIMIT_SMALL = 40 * 1024 * 1024

FFN_CHUNK = MXU_DIM
TOKEN_TILE = 512
ATTN_TILE = 256
PAGES_PER_STEP = 8


def _cparams(sem, vmem):
    return pltpu.CompilerParams(dimension_semantics=sem, vmem_limit_bytes=vmem)


def _tile(m, pref):
    t = min(m, pref)
    while m % t:
        t //= 2
    return t


def _rms(x, w):
    ms = jnp.mean(x * x, axis=-1, keepdims=True)
    return x * lax.rsqrt(ms + RMS_EPS) * w


def _silu(x):
    return x * jax.nn.sigmoid(x)


def _dot(a, b):
    return jnp.dot(a, b, preferred_element_type=F32)


def _dot_nt(a, b):
    return lax.dot_general(a, b, (((1,), (1,)), ((), ())), preferred_element_type=F32)


def _dot_exact(a, b):
    return jnp.dot(a, b, precision=lax.Precision.HIGHEST, preferred_element_type=F32)


def _row_spec(tile, width):
    return pl.BlockSpec((tile, width), lambda i: (i, 0))


def _const_spec(shape):
    return pl.BlockSpec(shape, lambda *_: (0,) * len(shape))


def _rope_table_kernel(inv_ref, cos_ref, sin_ref, *, pos0, period):
    shape = cos_ref.shape
    row = lax.broadcasted_iota(jnp.int32, shape, 0)
    lane = lax.broadcasted_iota(jnp.int32, shape, 1)
    pos = pos0 + lax.rem(row, period)
    ang = pos.astype(F32) * inv_ref[...]
    cos_ref[...] = jnp.cos(ang)
    s = jnp.sin(ang)
    sin_ref[...] = jnp.where(lax.rem(lane, DIFF_HEAD_DIM) < DIFF_HEAD_DIM // 2, -s, s)


def _rope_tables(rows, pos0, period):
    half = DIFF_HEAD_DIM // 2
    inv = 1.0 / (ROPE_THETA ** (np.arange(0, DIFF_HEAD_DIM, 2, dtype=np.float32) / DIFF_HEAD_DIM))
    inv = np.tile(inv.astype(np.float32), LANES // half)[None, :]
    return pl.pallas_call(
        functools.partial(_rope_table_kernel, pos0=pos0, period=period),
        out_shape=(jax.ShapeDtypeStruct((rows, LANES), F32),) * 2,
        name="rope_table",
    )(jnp.asarray(inv))


def _ffn_kernel(*refs, n_chunks, has_mix, has_final):
    it = iter(refs)
    x_ref = next(it)
    if has_mix:
        a_ref, y_ref, wo_ref = next(it), next(it), next(it)
    nw_ref, wg_ref, wu_ref, wd_ref = next(it), next(it), next(it), next(it)
    fw_ref = next(it) if has_final else None
    o_ref = next(it)

    x = x_ref[...]
    if has_mix:
        half = a_ref.shape[1]
        x = x + _dot(a_ref[...].astype(BF16), wo_ref[:half, :]) + _dot(y_ref[...].astype(BF16), wo_ref[half:, :])
    u = _rms(x, nw_ref[...]).astype(BF16)
    acc = jnp.zeros(x.shape, F32)
    for c in range(n_chunks):
        sl = slice(c * FFN_CHUNK, (c + 1) * FFN_CHUNK)
        g = _dot(u, wg_ref[:, sl])
        up = _dot(u, wu_ref[:, sl])
        h = (_silu(g) * up).astype(BF16)
        acc = acc + _dot(h, wd_ref[sl, :])
    y = x + 0.5 * acc
    if has_final:
        y = _rms(y, fw_ref[...])
    o_ref[...] = y


def _ffn(x, nw, wg, wu, wd, mix=None, final_w=None):
    m, d = x.shape
    hidden = wg.shape[1]
    tm = _tile(m, TOKEN_TILE)
    args, specs = [x], [_row_spec(tm, d)]
    if mix is not None:
        a, y, wo = mix
        args += [a, y, wo]
        specs += [_row_spec(tm, a.shape[1]), _row_spec(tm, y.shape[1]), _const_spec(wo.shape)]
    args += [nw, wg, wu, wd]
    specs += [_const_spec(nw.shape), _const_spec(wg.shape), _const_spec(wu.shape), _const_spec(wd.shape)]
    if final_w is not None:
        args.append(final_w)
        specs.append(_const_spec(final_w.shape))
    return pl.pallas_call(
        functools.partial(_ffn_kernel, n_chunks=hidden // FFN_CHUNK, has_mix=mix is not None,
                          has_final=final_w is not None),
        grid=(m // tm,),
        in_specs=specs,
        out_specs=_row_spec(tm, d),
        out_shape=jax.ShapeDtypeStruct((m, d), F32),
        compiler_params=_cparams(("parallel",), VMEM_LIMIT_BIG),
        name="ffn",
    )(*args)


def _inproj_kernel(x_ref, nw_ref, w_ref, wdt_ref, cos_ref, sin_ref,
                   q_ref, k_ref, kb_ref, v_ref, vb_ref, z_ref, xbc_ref, dt_ref, *, cache_layout):
    u = _rms(x_ref[...], nw_ref[...]).astype(BF16)
    tm = u.shape[0]
    reps = QK_WIDTH // LANES
    cos = jnp.concatenate([cos_ref[...]] * reps, axis=1)
    sin = jnp.concatenate([sin_ref[...]] * reps, axis=1)
    lane = lax.broadcasted_iota(jnp.int32, (tm, QK_WIDTH), 1)
    low_half = lax.rem(lane, DIFF_HEAD_DIM) < DIFF_HEAD_DIM // 2
    half = DIFF_HEAD_DIM // 2

    def rope(t):
        up = pltpu.roll(t, QK_WIDTH - half, 1)
        down = pltpu.roll(t, half, 1)
        return t * cos + jnp.where(low_half, up, down) * sin

    q = rope(_dot(u, w_ref[:, 0:QK_WIDTH]))
    q_ref[...] = (q * (DIFF_HEAD_DIM ** -0.5)).astype(q_ref.dtype)
    k = rope(_dot(u, w_ref[:, QK_WIDTH:2 * QK_WIDTH]))
    kb_ref[...] = k.astype(BF16)
    v = _dot(u, w_ref[:, 2 * QK_WIDTH:3 * QK_WIDTH])
    vb_ref[...] = v.astype(BF16)
    if cache_layout:
        k_ref[...] = k.T
        for h in range(DIFF_HEADS):
            v_ref[pl.ds(h, tm, stride=DIFF_HEADS), :] = v[:, h * HEAD_LANES:(h + 1) * HEAD_LANES]
    else:
        k_ref[...] = k
        v_ref[...] = v
    z0 = 3 * QK_WIDTH
    z_ref[...] = _dot(u, w_ref[:, z0:z0 + SSM_INNER])
    xbc_ref[...] = _dot(u, w_ref[:, z0 + SSM_INNER:z0 + SSM_INNER + CONV_CH])
    dt_ref[...] = _dot(u, wdt_ref[...])


def _inproj(x, nw, w_main, w_dt, cos, sin, q_dtype, cache_layout):
    m, d = x.shape
    period = cos.shape[0]
    tm = _tile(period, TOKEN_TILE)
    n_per = period // tm
    tab_spec = pl.BlockSpec((tm, LANES), lambda i: (i % n_per, 0))
    rows = lambda w, dt: (_row_spec(tm, w), jax.ShapeDtypeStruct((m, w), dt))
    k_out, v_out = rows(QK_WIDTH, F32), rows(QK_WIDTH, F32)
    if cache_layout:
        k_out = (pl.BlockSpec((None, QK_WIDTH, tm), lambda i: (i // n_per, 0, i % n_per)),
                 jax.ShapeDtypeStruct((m // period, QK_WIDTH, period), F32))
        v_out = (_row_spec(tm * DIFF_HEADS, HEAD_LANES), jax.ShapeDtypeStruct((m * DIFF_HEADS, HEAD_LANES), F32))
    outs = [rows(QK_WIDTH, q_dtype), k_out, rows(QK_WIDTH, BF16), v_out, rows(QK_WIDTH, BF16),
            rows(SSM_INNER, F32), rows(CONV_CH, F32), rows(LANES, F32)]
    return pl.pallas_call(
        functools.partial(_inproj_kernel, cache_layout=cache_layout),
        grid=(m // tm,),
        in_specs=[_row_spec(tm, d), _const_spec(nw.shape), _const_spec(w_main.shape), _const_spec(w_dt.shape),
                  tab_spec, tab_spec],
        out_specs=[spec for spec, _ in outs],
        out_shape=[shape for _, shape in outs],
        compiler_params=_cparams(("parallel",), VMEM_LIMIT_SMALL),
        name="inproj",
    )(x, nw, w_main, w_dt, cos, sin)


def _lambda(lam_ref, lambda_init):
    v = lam_ref[...]
    d1 = jnp.sum(v[0:1] * v[1:2], axis=1, keepdims=True)
    d2 = jnp.sum(v[2:3] * v[3:4], axis=1, keepdims=True)
    return jnp.exp(d1) - jnp.exp(d2) + lambda_init


def _subln(d, w, lambda_init):
    return _rms(d, w) * (1.0 - lambda_init)


def _attn_prompt_kernel(lam_ref, sub_ref, q_ref, k_ref, v_ref, o_ref, *, tq, lambda_init):
    qi = pl.program_id(2)
    q = q_ref[...]
    lane = lax.broadcasted_iota(jnp.int32, q.shape, 1)
    zero = jnp.zeros_like(q)
    qs = jnp.concatenate([jnp.where(lane < DIFF_HEAD_DIM, q, zero),
                          jnp.where(lane >= DIFF_HEAD_DIM, q, zero)], axis=0)
    row = lax.broadcasted_iota(jnp.int32, (2 * tq, tq), 0)
    col = lax.broadcasted_iota(jnp.int32, (2 * tq, tq), 1)
    causal = col <= lax.rem(row, tq)

    def step(j, carry, masked):
        m, l, acc = carry
        start = pl.multiple_of(j * tq, tq)
        kb = k_ref[pl.ds(start, tq), :]
        vb = v_ref[pl.ds(start, tq), :]
        s = _dot_nt(qs, kb)
        if masked:
            s = jnp.where(causal, s, -jnp.inf)
        m_new = jnp.maximum(m, jnp.max(s, axis=1, keepdims=True))
        alpha = jnp.exp(m - m_new)
        p = jnp.exp(s - m_new)
        l = alpha * l + jnp.sum(p, axis=1, keepdims=True)
        acc = alpha * acc + _dot(p.astype(BF16), vb)
        return m_new, l, acc

    init = (jnp.full((2 * tq, 1), -jnp.inf, F32), jnp.zeros((2 * tq, 1), F32),
            jnp.zeros((2 * tq, HEAD_LANES), F32))
    carry = lax.fori_loop(0, qi, functools.partial(step, masked=False), init)
    _, l, acc = step(qi, carry, True)
    o = acc / l
    d = o[:tq] - _lambda(lam_ref, lambda_init) * o[tq:]
    o_ref[...] = _subln(d, sub_ref[...], lambda_init).astype(o_ref.dtype)


def _attn_prompt(qb, kb, vb, lam_vecs, sub_w, batch, seq, lambda_init):
    tq = _tile(seq, ATTN_TILE)
    nq = seq // tq
    kv_spec = pl.BlockSpec((seq, HEAD_LANES), lambda b, h, i: (b, h))
    q_spec = pl.BlockSpec((tq, HEAD_LANES), lambda b, h, i: (b * nq + i, h))
    return pl.pallas_call(
        functools.partial(_attn_prompt_kernel, tq=tq, lambda_init=lambda_init),
        grid=(batch, DIFF_HEADS, nq),
        in_specs=[_const_spec(lam_vecs.shape), _const_spec(sub_w.shape), q_spec, kv_spec, kv_spec],
        out_specs=q_spec,
        out_shape=jax.ShapeDtypeStruct(qb.shape, BF16),
        compiler_params=_cparams(("parallel", "parallel", "arbitrary"), VMEM_LIMIT_SMALL),
        name="attn_prompt",
    )(lam_vecs, sub_w, qb, kb, vb)


def _attn_sample_kernel(pt_ref, lam_ref, sub_ref, q_ref, kn_ref, vn_ref, *rest, n_pages, n_steps, lambda_init):
    del pt_ref
    k_refs, v_refs = rest[:n_pages], rest[n_pages:2 * n_pages]
    o_ref = rest[2 * n_pages]
    wq_s, m_s, l_s, acc_s = rest[2 * n_pages + 1:]
    step = pl.program_id(1)
    t_new = q_ref.shape[0]
    page = k_refs[0].shape[1]
    head_rows = 2 * t_new

    @pl.when(step == 0)
    def _():
        qt = jnp.concatenate([q_ref[...]] * (2 * DIFF_HEADS), axis=0)
        row = lax.broadcasted_iota(jnp.int32, qt.shape, 0)
        col = lax.broadcasted_iota(jnp.int32, qt.shape, 1)
        same = lax.div(row, t_new) == lax.div(col, DIFF_HEAD_DIM)
        wq_s[...] = jnp.where(same, qt, 0.0).astype(BF16)
        m_s[...] = jnp.full(m_s.shape, -jnp.inf, F32)
        l_s[...] = jnp.zeros(l_s.shape, F32)
        acc_s[...] = jnp.zeros(acc_s.shape, F32)

    def update(scores, values):
        m = m_s[...]
        m_new = m
        for s in scores:
            m_new = jnp.maximum(m_new, jnp.max(s, axis=1, keepdims=True))
        alpha = jnp.exp(m - m_new)
        l = alpha * l_s[...]
        acc = alpha * acc_s[...]
        for s, v_heads in zip(scores, values):
            p = jnp.exp(s - m_new)
            l = l + jnp.sum(p, axis=1, keepdims=True)
            pb = p.astype(BF16)
            acc = acc + jnp.concatenate(
                [_dot(pb[h * head_rows:(h + 1) * head_rows], v_heads[h]) for h in range(DIFF_HEADS)], axis=0)
        m_s[...] = m_new
        l_s[...] = l
        acc_s[...] = acc

    wq = wq_s[...]
    update([_dot(wq, k[...].astype(BF16)) for k in k_refs],
           [[v[pl.ds(h, page, stride=DIFF_HEADS), :].astype(BF16) for h in range(DIFF_HEADS)] for v in v_refs])

    @pl.when(step == n_steps - 1)
    def _():
        pad = jnp.zeros((page - t_new, QK_WIDTH), F32)
        kn = jnp.concatenate([kn_ref[...], pad], axis=0).astype(BF16)
        vn = jnp.concatenate([vn_ref[...], pad], axis=0).astype(BF16)
        s = _dot_nt(wq, kn)
        row = lax.broadcasted_iota(jnp.int32, s.shape, 0)
        col = lax.broadcasted_iota(jnp.int32, s.shape, 1)
        s = jnp.where(col <= lax.rem(row, t_new), s, -jnp.inf)
        update([s], [[vn[:, h * HEAD_LANES:(h + 1) * HEAD_LANES] for h in range(DIFF_HEADS)]])
        o = acc_s[...] / l_s[...]
        lam = _lambda(lam_ref, lambda_init)
        for h in range(DIFF_HEADS):
            r0 = h * head_rows
            d = o[r0:r0 + t_new] - lam * o[r0 + t_new:r0 + head_rows]
            o_ref[:, h * HEAD_LANES:(h + 1) * HEAD_LANES] = _subln(d, sub_ref[...], lambda_init)


def _attn_sample(q, k_new, v_new, cache_k, cache_v, page_table, layer, lam_vecs, sub_w, lambda_init):
    batch, n_log = page_table.shape
    m = q.shape[0]
    t_new = m // batch
    n_layers, n_phys, page = cache_k.shape[:3]
    ck = jnp.transpose(cache_k, (0, 1, 3, 4, 5, 2)).reshape(n_layers, n_phys, QK_WIDTH, page)
    cv = cache_v.reshape(n_layers, n_phys, page * DIFF_HEADS, HEAD_LANES)
    g = _tile(n_log, PAGES_PER_STEP)
    n_steps = n_log // g
    n_rows = 2 * DIFF_HEADS * t_new

    def page_spec(i, shape):
        return pl.BlockSpec((None, None) + shape, lambda b, s, pt: (layer, pt[b, s * g + i], 0, 0))

    tok_spec = pl.BlockSpec((t_new, QK_WIDTH), lambda b, s, pt: (b, 0))
    const2 = lambda shape: pl.BlockSpec(shape, lambda b, s, pt: (0, 0))
    grid_spec = pltpu.PrefetchScalarGridSpec(
        num_scalar_prefetch=1,
        grid=(batch, n_steps),
        in_specs=[const2(lam_vecs.shape), const2(sub_w.shape), tok_spec, tok_spec, tok_spec]
                 + [page_spec(i, (QK_WIDTH, page)) for i in range(g)]
                 + [page_spec(i, (page * DIFF_HEADS, HEAD_LANES)) for i in range(g)],
        out_specs=tok_spec,
        scratch_shapes=[pltpu.VMEM((n_rows, QK_WIDTH), BF16), pltpu.VMEM((n_rows, 1), F32),
                        pltpu.VMEM((n_rows, 1), F32), pltpu.VMEM((n_rows, HEAD_LANES), F32)],
    )
    return pl.pallas_call(
        functools.partial(_attn_sample_kernel, n_pages=g, n_steps=n_steps, lambda_init=lambda_init),
        grid_spec=grid_spec,
        out_shape=jax.ShapeDtypeStruct((m, QK_WIDTH), F32),
        compiler_params=_cparams(("parallel", "arbitrary"), VMEM_LIMIT_SMALL),
        name="attn_sample",
    )(page_table, lam_vecs, sub_w, q, k_new, v_new, *([ck] * g), *([cv] * g))


def _ssd_kernel(xbc_ref, z_ref, dt_ref, cw_ref, cb_ref, dtb_ref, alog_ref, dsk_ref, nw_ref, exp_ref,
                cinit_ref, h0_ref, y_ref, cout_ref, hout_ref, ext_s, st_s, *, t_in, n_chunks):
    c = pl.program_id(1)
    T = SSD_CHUNK
    tail = CONV_WIDTH - 1

    @pl.when(c == 0)
    def _():
        ext_s[0:SUBLANES, :] = jnp.zeros((SUBLANES, CONV_CH), F32)
        ext_s[SUBLANES - tail:SUBLANES, :] = cinit_ref[...]
        st_s[...] = h0_ref[...].reshape(SSM_INNER, SSM_STATE).T

    u = xbc_ref[...]
    ext_s[SUBLANES:SUBLANES + t_in, :] = u
    cw = cw_ref[...]
    conv = cb_ref[...] + cw[tail:tail + 1] * u
    for j in range(tail):
        conv = conv + cw[j:j + 1] * ext_s[SUBLANES - tail + j:SUBLANES - tail + j + t_in, :]
    ext_s[0:SUBLANES, :] = ext_s[t_in:t_in + SUBLANES, :]
    xc = _silu(conv)
    dt = jax.nn.softplus(dt_ref[...] + dtb_ref[...])
    z = z_ref[...]
    if t_in < T:
        xc = jnp.concatenate([xc, jnp.zeros((T - t_in, CONV_CH), F32)], axis=0)
        dt = jnp.concatenate([dt, jnp.zeros((T - t_in, LANES), F32)], axis=0)
        z = jnp.concatenate([z, jnp.zeros((T - t_in, SSM_INNER), F32)], axis=0)

    a = -jnp.exp(alog_ref[...])
    row = lax.broadcasted_iota(jnp.int32, (T, T), 0)
    col = lax.broadcasted_iota(jnp.int32, (T, T), 1)
    causal = row >= col
    acs = _dot_exact(causal.astype(F32), dt * a)
    acs_t = acs.T
    expand = exp_ref[...]
    dt_x = _dot_exact(dt, expand)
    acs_x = _dot_exact(acs, expand)
    last = acs_x[T - 1:T, :]
    decay_out = jnp.exp(acs_x)
    decay_end = jnp.exp(last - acs_x)
    chunk_decay = jnp.exp(last)

    xs = xc[:, :SSM_INNER]
    xdt = xs * dt_x
    xdt_b = xdt.astype(BF16)
    xw_b = (xdt * decay_end).astype(BF16)
    state = st_s[...]
    state_b = state.astype(BF16)
    lane = lax.broadcasted_iota(jnp.int32, (T, LANES), 1)
    heads_per_group = SSM_HEADS // SSM_GROUPS
    ys = []
    for g in range(SSM_GROUPS):
        b0 = SSM_INNER + g * SSM_STATE
        c0 = SSM_INNER + SSM_GROUPS * SSM_STATE + g * SSM_STATE
        bg_t = xc[:, b0:b0 + SSM_STATE].T.astype(BF16)
        cg = xc[:, c0:c0 + SSM_STATE].astype(BF16)
        gl = slice(g * GROUP_LANES, (g + 1) * GROUP_LANES)
        cb = _dot(cg, bg_t)
        y_off = _dot(cg, state_b[:, gl]) * decay_out[:, gl]
        st_s[:, gl] = state[:, gl] * chunk_decay[:, gl] + _dot(bg_t, xw_b[:, gl])
        for pair in range(heads_per_group // 2):
            h0 = g * heads_per_group + 2 * pair
            xp = xdt_b[:, h0 * SSM_HEAD_DIM:(h0 + 2) * SSM_HEAD_DIM]
            outs = []
            for hh in (h0, h0 + 1):
                seg = acs[:, hh:hh + 1] - acs_t[hh:hh + 1, :]
                w = cb * jnp.exp(jnp.where(causal, seg, -jnp.inf))
                outs.append(_dot(w.astype(BF16), xp))
            ys.append(jnp.where(lane < SSM_HEAD_DIM, outs[0], outs[1])
                      + y_off[:, 2 * pair * SSM_HEAD_DIM:(2 * pair + 2) * SSM_HEAD_DIM])
    y = jnp.concatenate(ys, axis=1) + dsk_ref[...] * xs
    y = y * _silu(z)
    nw = nw_ref[...]
    parts = []
    for g in range(SSM_GROUPS):
        gl = slice(g * GROUP_LANES, (g + 1) * GROUP_LANES)
        parts.append(_rms(y[:, gl], nw[:, gl]))
    y_ref[...] = jnp.concatenate(parts, axis=1)[:t_in].astype(y_ref.dtype)

    @pl.when(c == n_chunks - 1)
    def _():
        cout_ref[...] = ext_s[SUBLANES - tail:SUBLANES, :]
        hout_ref[...] = st_s[...].T.reshape(SSM_HEADS, SSM_HEAD_DIM, SSM_STATE)


def _ssd(xbc, z, dt_raw, p, conv_init, h0, batch, seq, y_dtype):
    t_in = _tile(seq, SSD_CHUNK)
    n_chunks = seq // t_in
    assert t_in == SSD_CHUNK or n_chunks == 1
    tail = CONV_WIDTH - 1
    rows = lambda w: pl.BlockSpec((t_in, w), lambda b, c: (b * n_chunks + c, 0))
    per_b3 = lambda s: pl.BlockSpec((None,) + s, lambda b, c: (b,) + (0,) * len(s))
    consts = [p['conv_w'], p['conv_b'], p['dt_bias'], p['a_log'], p['d_skip'], p['ssm_norm_w'], p['expand']]
    return pl.pallas_call(
        functools.partial(_ssd_kernel, t_in=t_in, n_chunks=n_chunks),
        grid=(batch, n_chunks),
        in_specs=[rows(CONV_CH), rows(SSM_INNER), rows(LANES)]
                 + [pl.BlockSpec(a.shape, lambda b, c: (0, 0)) for a in consts]
                 + [per_b3((tail, CONV_CH)), per_b3((SSM_HEADS, SSM_HEAD_DIM, SSM_STATE))],
        out_specs=[rows(SSM_INNER), per_b3((tail, CONV_CH)), per_b3((SSM_HEADS, SSM_HEAD_DIM, SSM_STATE))],
        out_shape=[jax.ShapeDtypeStruct((batch * seq, SSM_INNER), y_dtype),
                   jax.ShapeDtypeStruct((batch, tail, CONV_CH), F32),
                   jax.ShapeDtypeStruct((batch, SSM_HEADS, SSM_HEAD_DIM, SSM_STATE), F32)],
        scratch_shapes=[pltpu.VMEM((SUBLANES + t_in, CONV_CH), F32), pltpu.VMEM((SSM_STATE, SSM_INNER), F32)],
        compiler_params=_cparams(("parallel", "arbitrary"), VMEM_LIMIT_SMALL),
        name="ssd",
    )(xbc, z, dt_raw, *consts, conv_init, h0)


def _pool_kernel(x_ref, nw_ref, init_ref, pw_ref, ps_ref, o_ref, pout_ref, ext_s, *, tp, n_tiles, pos0):
    t = pl.program_id(1)
    halo = POOL_HALO

    @pl.when(t == 0)
    def _():
        ext_s[0:halo, :] = jnp.zeros((halo, ext_s.shape[1]), F32)
        ext_s[halo - POOL_STATE_LEN:halo, :] = init_ref[...]

    x = x_ref[...]
    u = _rms(x, nw_ref[...])
    ext_s[halo:halo + tp, :] = u
    e = ext_s[...]
    gc = POOL_GROUP_CH
    sums = [e]
    for lvl in range(len(POOL_WINDOWS)):
        prev = sums[-1][:, gc:] if lvl else sums[-1]
        sums.append(prev + pltpu.roll(prev, 1 << lvl, 0))
    pos = pos0 + t * tp + lax.broadcasted_iota(jnp.int32, (tp, 1), 0)
    mixed = []
    for g, w in enumerate(POOL_WINDOWS):
        cnt = jnp.minimum(w, pos + 1).astype(F32)
        win = sums[g + 1][halo:, :gc]
        pooled = win / cnt - u[:, g * gc:(g + 1) * gc]
        mixed.append(_dot(pooled.astype(BF16), pw_ref[g]))
    o_ref[...] = x + jnp.concatenate(mixed, axis=1) * ps_ref[...]

    @pl.when(t == n_tiles - 1)
    def _():
        pout_ref[...] = ext_s[tp + halo - POOL_STATE_LEN:tp + halo, :]

    ext_s[0:halo, :] = ext_s[tp:tp + halo, :]


def _pool(x, nw, pool_init, pool_w, pool_scale, batch, seq, pos0):
    d = x.shape[1]
    tp = _tile(seq, TOKEN_TILE)
    n_tiles = seq // tp
    rows = pl.BlockSpec((tp, d), lambda b, t: (b * n_tiles + t, 0))
    state = pl.BlockSpec((None, POOL_STATE_LEN, d), lambda b, t: (b, 0, 0))
    return pl.pallas_call(
        functools.partial(_pool_kernel, tp=tp, n_tiles=n_tiles, pos0=pos0),
        grid=(batch, n_tiles),
        in_specs=[rows, pl.BlockSpec(nw.shape, lambda b, t: (0, 0)), state,
                  pl.BlockSpec(pool_w.shape, lambda b, t: (0, 0, 0)), pl.BlockSpec(pool_scale.shape, lambda b, t: (0, 0))],
        out_specs=[rows, state],
        out_shape=[jax.ShapeDtypeStruct(x.shape, F32), jax.ShapeDtypeStruct((batch, POOL_STATE_LEN, d), F32)],
        scratch_shapes=[pltpu.VMEM((POOL_HALO + tp, d), F32)],
        compiler_params=_cparams(("parallel", "arbitrary"), VMEM_LIMIT_SMALL),
        name="pool",
    )(x, nw, pool_init, pool_w, pool_scale)


def _prep_params(norm_w, ffn_w_gate, ffn_w_up, ffn_w_down, ab_w_in, ab_w_out, ab_conv_w, ab_conv_b, ab_dt_bias,
                 ab_a_log, ab_d_skip, ab_ssm_norm_w, ab_lambda_q1, ab_lambda_k1, ab_lambda_q2, ab_lambda_k2,
                 ab_subln_w, pool_w, pool_scale, final_norm_w):
    n_ab = ab_w_in.shape[0]
    d = ab_w_in.shape[1]

    def lane_pad(v):
        return jnp.pad(v, ((0, 0), (0, LANES - v.shape[1])))[:, None, :]

    expand = np.zeros((LANES, SSM_INNER), np.float32)
    for h in range(SSM_HEADS):
        expand[h, h * SSM_HEAD_DIM:(h + 1) * SSM_HEAD_DIM] = 1.0
    return dict(
        norm_w=norm_w[:, :, None, :],
        wg=ffn_w_gate.astype(BF16), wu=ffn_w_up.astype(BF16), wd=ffn_w_down.astype(BF16),
        w_main=ab_w_in[:, :, :MAIN_PROJ].astype(BF16),
        w_dt=jnp.pad(ab_w_in[:, :, MAIN_PROJ:], ((0, 0), (0, 0), (0, LANES - SSM_HEADS))).astype(BF16),
        w_out=ab_w_out.astype(BF16),
        conv_w=ab_conv_w, conv_b=ab_conv_b[:, None, :],
        dt_bias=lane_pad(ab_dt_bias), a_log=lane_pad(ab_a_log),
        d_skip=jnp.repeat(ab_d_skip, SSM_HEAD_DIM, axis=1)[:, None, :],
        ssm_norm_w=ab_ssm_norm_w[:, None, :],
        lam=jnp.stack([ab_lambda_q1, ab_lambda_k1, ab_lambda_q2, ab_lambda_k2], axis=1),
        subln_w=ab_subln_w[:, None, :],
        pool_w=pool_w.astype(BF16), pool_scale=pool_scale[:, None, :],
        final_norm_w=final_norm_w[None, :],
        expand=jnp.asarray(expand),
    )


def _trunk(x, batch, seq, pos0, attn_fn, conv_init, ssm_init, pool_init, p, compact):
    depth = p['wg'].shape[0]
    m = x.shape[0]
    act_dtype = BF16 if compact else F32
    new_k, new_v, new_conv, new_ssm, new_pool = [], [], [], [], []
    period = seq if compact else m
    cos, sin = _rope_tables(period, pos0, seq)
    mix = None
    for layer in range(depth):
        nw = p['norm_w'][layer]
        i = layer // 2
        ffn_w = lambda j: (p['wg'][layer, j], p['wu'][layer, j], p['wd'][layer, j])
        x = _ffn(x, nw[0], *ffn_w(0), mix=mix)
        mix = None
        if layer % 2 == 0:
            lambda_init = 0.8 - 0.6 * math.exp(-0.3 * layer)
            q, k, kb, v, vb, z, xbc, dt_raw = _inproj(x, nw[1], p['w_main'][i], p['w_dt'][i], cos, sin, act_dtype,
                                                      cache_layout=compact)
            attn = attn_fn(i, q, k, kb, v, vb, p['lam'][i], p['subln_w'][i], lambda_init)
            ssm_p = dict(conv_w=p['conv_w'][i], conv_b=p['conv_b'][i], dt_bias=p['dt_bias'][i], a_log=p['a_log'][i],
                         d_skip=p['d_skip'][i], ssm_norm_w=p['ssm_norm_w'][i], expand=p['expand'])
            y, cb, hs = _ssd(xbc, z, dt_raw, ssm_p, conv_init[i], ssm_init[i], batch, seq, act_dtype)
            if compact:
                k = jnp.transpose(k.reshape(batch, DIFF_HEADS, 2, DIFF_HEAD_DIM, seq), (0, 4, 1, 2, 3))
            new_k.append(k.reshape(batch, seq, DIFF_HEADS, 2, DIFF_HEAD_DIM))
            new_v.append(v.reshape(batch, seq, DIFF_HEADS, HEAD_LANES))
            new_conv.append(cb)
            new_ssm.append(hs)
            mix = (attn, y, p['w_out'][i])
        else:
            x, pb = _pool(x, nw[1], pool_init[i], p['pool_w'][i], p['pool_scale'][i], batch, seq, pos0)
            new_pool.append(pb)
        final_w = p['final_norm_w'] if layer == depth - 1 else None
        x = _ffn(x, nw[2], *ffn_w(1), mix=mix, final_w=final_w)
        mix = None
    return x, jnp.stack(new_k), jnp.stack(new_v), jnp.stack(new_conv), jnp.stack(new_ssm), jnp.stack(new_pool)


def kernel(x_prompt, x_sample, cache_k, cache_v, state_conv, state_ssm, state_pool, page_table, norm_w, ffn_w_gate, ffn_w_up, ffn_w_down, ab_w_in, ab_w_out, ab_conv_w, ab_conv_b, ab_dt_bias, ab_a_log, ab_d_skip, ab_ssm_norm_w, ab_lambda_q1, ab_lambda_k1, ab_lambda_q2, ab_lambda_k2, ab_subln_w, pool_w, pool_scale, final_norm_w):
    p = _prep_params(norm_w, ffn_w_gate, ffn_w_up, ffn_w_down, ab_w_in, ab_w_out, ab_conv_w, ab_conv_b, ab_dt_bias,
                     ab_a_log, ab_d_skip, ab_ssm_norm_w, ab_lambda_q1, ab_lambda_k1, ab_lambda_q2, ab_lambda_k2,
                     ab_subln_w, pool_w, pool_scale, final_norm_w)
    n_ab, n_pool = state_conv.shape[0], state_pool.shape[0]
    bp, lp, d = x_prompt.shape
    bs, ls, _ = x_sample.shape

    def attn_p(i, q, k, kb, v, vb, lam, sub_w, lambda_init):
        return _attn_prompt(q, kb, vb, lam, sub_w, bp, lp, lambda_init)

    zeros = lambda shape: jnp.zeros(shape, F32)
    yp, kp, vp, cp, sp, pp = _trunk(
        x_prompt.reshape(bp * lp, d), bp, lp, 0, attn_p,
        zeros((n_ab, bp, CONV_WIDTH - 1, CONV_CH)), zeros((n_ab, bp, SSM_HEADS, SSM_HEAD_DIM, SSM_STATE)),
        zeros((n_pool, bp, POOL_STATE_LEN, d)), p, compact=True)

    past_len = page_table.shape[1] * cache_k.shape[2]

    def attn_s(i, q, k, kb, v, vb, lam, sub_w, lambda_init):
        return _attn_sample(q, k, v, cache_k, cache_v, page_table, i, lam, sub_w, lambda_init)

    ys, ks, vs, cs, ss, ps = _trunk(
        x_sample.reshape(bs * ls, d), bs, ls, past_len, attn_s, state_conv, state_ssm, state_pool, p,
        compact=False)
    return (yp.reshape(bp, lp, d), ys.reshape(bs, ls, d), kp, vp, cp, sp, pp, ks, vs, cs, ss, ps)
```

```python
import functools
import math

import numpy as np
import jax
import jax.numpy as jnp
from jax import lax
from jax.experimental import pallas as pl
from jax.experimental.pallas import tpu as pltpu

F32 = jnp.float32
BF16 = jnp.bfloat16

RMS_EPS = 1e-6
ROPE_THETA = 10000.0

DIFF_HEADS = 4
DIFF_HEAD_DIM = 64
HEAD_LANES = 2 * DIFF_HEAD_DIM
QK_WIDTH = DIFF_HEADS * HEAD_LANES
SSM_INNER = 512
SSM_HEAD_DIM = 64
SSM_HEADS = 8
SSM_GROUPS = 2
SSM_STATE = 128
GROUP_LANES = SSM_INNER // SSM_GROUPS
CONV_WIDTH = 4
CONV_CH = SSM_INNER + 2 * SSM_GROUPS * SSM_STATE
SSD_CHUNK = 128
POOL_WINDOWS = (2, 4, 8, 16)
POOL_GROUP_CH = 256
POOL_HALO = 16
POOL_STATE_LEN = 15
MAIN_PROJ = 2 * QK_WIDTH + QK_WIDTH + SSM_INNER + CONV_CH

LANES = 128
SUBLANES = 8
MXU_DIM = 256
VMEM_LIMIT_BIG = 58 * 1024 * 1024
VMEM_LIMIT_SMALL = 40 * 1024 * 1024

FFN_CHUNK = MXU_DIM
TOKEN_TILE = 512
ATTN_Q_TILE = 512
ATTN_K_TILE = 256
ATTN_STRIP = MXU_DIM
PAGES_PER_STEP = 8


def _cparams(sem, vmem):
    return pltpu.CompilerParams(dimension_semantics=sem, vmem_limit_bytes=vmem)


def _tile(m, pref):
    t = min(m, pref)
    while m % t:
        t //= 2
    return t


def _rms(x, w):
    ms = jnp.mean(x * x, axis=-1, keepdims=True)
    return x * lax.rsqrt(ms + RMS_EPS) * w


def _silu(x):
    return x * jax.nn.sigmoid(x)


def _dot(a, b):
    return jnp.dot(a, b, preferred_element_type=F32)


def _dot_nt(a, b):
    return lax.dot_general(a, b, (((1,), (1,)), ((), ())), preferred_element_type=F32)


def _dot_exact(a, b):
    return jnp.dot(a, b, precision=lax.Precision.HIGHEST, preferred_element_type=F32)


def _row_spec(tile, width):
    return pl.BlockSpec((tile, width), lambda i: (i, 0))


def _const_spec(shape):
    return pl.BlockSpec(shape, lambda *_: (0,) * len(shape))


def _rope_table_kernel(inv_ref, cos_ref, sin_ref, *, pos0, period):
    shape = cos_ref.shape
    row = lax.broadcasted_iota(jnp.int32, shape, 0)
    lane = lax.broadcasted_iota(jnp.int32, shape, 1)
    pos = pos0 + lax.rem(row, period)
    ang = pos.astype(F32) * inv_ref[...]
    cos_ref[...] = jnp.cos(ang)
    s = jnp.sin(ang)
    sin_ref[...] = jnp.where(lax.rem(lane, DIFF_HEAD_DIM) < DIFF_HEAD_DIM // 2, -s, s)


def _rope_tables(rows, pos0, period):
    half = DIFF_HEAD_DIM // 2
    inv = 1.0 / (ROPE_THETA ** (jnp.arange(0, DIFF_HEAD_DIM, 2, dtype=F32) / DIFF_HEAD_DIM))
    inv = jnp.tile(inv, LANES // half)[None, :]
    return pl.pallas_call(
        functools.partial(_rope_table_kernel, pos0=pos0, period=period),
        out_shape=(jax.ShapeDtypeStruct((rows, LANES), F32),) * 2,
        name="rope_table",
    )(inv)


def _ffn_kernel(*refs, n_chunks, has_mix, has_final):
    it = iter(refs)
    x_ref = next(it)
    if has_mix:
        a_ref, y_ref, wo_ref = next(it), next(it), next(it)
    nw_ref, wg_ref, wu_ref, wd_ref = next(it), next(it), next(it), next(it)
    fw_ref = next(it) if has_final else None
    o_ref = next(it)

    x = x_ref[...]
    if has_mix:
        half = a_ref.shape[1]
        x = x + _dot(a_ref[...].astype(BF16), wo_ref[:half, :]) + _dot(y_ref[...].astype(BF16), wo_ref[half:, :])
    u = _rms(x, nw_ref[...]).astype(BF16)
    acc = jnp.zeros(x.shape, F32)
    for c in range(n_chunks):
        sl = slice(c * FFN_CHUNK, (c + 1) * FFN_CHUNK)
        g = _dot(u, wg_ref[:, sl])
        up = _dot(u, wu_ref[:, sl])
        h = (_silu(g) * up).astype(BF16)
        acc = acc + _dot(h, wd_ref[sl, :])
    y = x + 0.5 * acc
    if has_final:
        y = _rms(y, fw_ref[...])
    o_ref[...] = y


def _stacked_spec(w, idx):
    n_lead = len(idx)
    return pl.BlockSpec((None,) * n_lead + w.shape[n_lead:], lambda *_: tuple(idx) + (0,) * (w.ndim - n_lead))


def _ffn(x, nw, wg, wu, wd, w_idx, mix=None, final_w=None):
    m, d = x.shape
    hidden = wg.shape[-1]
    tm = _tile(m, TOKEN_TILE)
    args, specs = [x], [_row_spec(tm, d)]
    if mix is not None:
        a, y, wo, wo_idx = mix
        args += [a, y, wo]
        specs += [_row_spec(tm, a.shape[1]), _row_spec(tm, y.shape[1]), _stacked_spec(wo, (wo_idx,))]
    args += [nw, wg, wu, wd]
    specs += [_const_spec(nw.shape), _stacked_spec(wg, w_idx), _stacked_spec(wu, w_idx), _stacked_spec(wd, w_idx)]
    if final_w is not None:
        args.append(final_w)
        specs.append(_const_spec(final_w.shape))
    return pl.pallas_call(
        functools.partial(_ffn_kernel, n_chunks=hidden // FFN_CHUNK, has_mix=mix is not None,
                          has_final=final_w is not None),
        grid=(m // tm,),
        in_specs=specs,
        out_specs=_row_spec(tm, d),
        out_shape=jax.ShapeDtypeStruct((m, d), F32),
        compiler_params=_cparams(("parallel",), VMEM_LIMIT_BIG),
        name="ffn",
    )(*args)


def _inproj_kernel(x_ref, nw_ref, w_ref, wdt_ref, cos_ref, sin_ref, *out_refs, prompt_layout):
    if prompt_layout:
        qt_ref, kt_ref, kb_ref, v4_ref, vt_ref, z_ref, xbc_ref, dt_ref = out_refs
    else:
        q_ref, k_ref, v_ref, z_ref, xbc_ref, dt_ref = out_refs
    u = _rms(x_ref[...], nw_ref[...]).astype(BF16)
    tm = u.shape[0]
    reps = QK_WIDTH // LANES
    cos = jnp.concatenate([cos_ref[...]] * reps, axis=1)
    sin = jnp.concatenate([sin_ref[...]] * reps, axis=1)
    lane = lax.broadcasted_iota(jnp.int32, (tm, QK_WIDTH), 1)
    low_half = lax.rem(lane, DIFF_HEAD_DIM) < DIFF_HEAD_DIM // 2
    half = DIFF_HEAD_DIM // 2

    def rope(t):
        up = pltpu.roll(t, QK_WIDTH - half, 1)
        down = pltpu.roll(t, half, 1)
        return t * cos + jnp.where(low_half, up, down) * sin

    q = rope(_dot(u, w_ref[:, 0:QK_WIDTH])) * (DIFF_HEAD_DIM ** -0.5)
    k = rope(_dot(u, w_ref[:, QK_WIDTH:2 * QK_WIDTH]))
    v = _dot(u, w_ref[:, 2 * QK_WIDTH:3 * QK_WIDTH])
    if prompt_layout:
        def head_blocks(t_ref, t):
            tb = t_ref.shape[-1]
            for h in range(DIFF_HEADS):
                for jb in range(tm // tb):
                    t_ref[h, jb] = t[h * HEAD_LANES:(h + 1) * HEAD_LANES, jb * tb:(jb + 1) * tb].astype(BF16)

        head_blocks(qt_ref, q.T)
        kt_ref[...] = k.T
        kb_ref[...] = k.astype(BF16)
        for h in range(DIFF_HEADS):
            v4_ref[pl.ds(h, tm, stride=DIFF_HEADS), :] = v[:, h * HEAD_LANES:(h + 1) * HEAD_LANES]
        head_blocks(vt_ref, v.T)
    else:
        q_ref[...] = q
        k_ref[...] = k
        v_ref[...] = v
    z0 = 3 * QK_WIDTH
    z_ref[...] = _dot(u, w_ref[:, z0:z0 + SSM_INNER])
    xbc_ref[...] = _dot(u, w_ref[:, z0 + SSM_INNER:z0 + SSM_INNER + CONV_CH])
    dt_ref[...] = _dot(u, wdt_ref[...])


def _inproj(x, nw, w_in, w_idx, w_dt, cos, sin, prompt_layout):
    m, d = x.shape
    period = cos.shape[0]
    tm = _tile(period, TOKEN_TILE)
    n_per = period // tm
    tab_spec = pl.BlockSpec((tm, LANES), lambda i: (i % n_per, 0))
    rows = lambda w, dt: (_row_spec(tm, w), jax.ShapeDtypeStruct((m, w), dt))
    tail = [rows(SSM_INNER, F32), rows(CONV_CH, F32), rows(LANES, F32)]
    if prompt_layout:
        n_seq = m // period
        def blocked(tile):
            ta = _tile(tm, tile)
            return (pl.BlockSpec((None, DIFF_HEADS, tm // ta, HEAD_LANES, ta), lambda i: (i // n_per, 0, i % n_per, 0, 0)),
                    jax.ShapeDtypeStruct((n_seq, DIFF_HEADS, period // ta, HEAD_LANES, ta), BF16))
        kt_out = (pl.BlockSpec((None, QK_WIDTH, tm), lambda i: (i // n_per, 0, i % n_per)),
                  jax.ShapeDtypeStruct((n_seq, QK_WIDTH, period), F32))
        v4_out = (_row_spec(tm * DIFF_HEADS, HEAD_LANES), jax.ShapeDtypeStruct((m * DIFF_HEADS, HEAD_LANES), F32))
        outs = [blocked(ATTN_Q_TILE), kt_out, rows(QK_WIDTH, BF16), v4_out, blocked(ATTN_K_TILE)] + tail
    else:
        outs = [rows(QK_WIDTH, F32)] * 3 + tail
    return pl.pallas_call(
        functools.partial(_inproj_kernel, prompt_layout=prompt_layout),
        grid=(m // tm,),
        in_specs=[_row_spec(tm, d), _const_spec(nw.shape), pl.BlockSpec((None,) + w_in.shape[1:], lambda i: (w_idx, 0, 0)),
                  _const_spec(w_dt.shape), tab_spec, tab_spec],
        out_specs=[spec for spec, _ in outs],
        out_shape=[shape for _, shape in outs],
        compiler_params=_cparams(("parallel",), VMEM_LIMIT_SMALL),
        name="inproj",
    )(x, nw, w_in, w_dt, cos, sin)


def _lambda(lam_ref, lambda_init):
    v = lam_ref[...]
    d1 = jnp.sum(v[0:1] * v[1:2], axis=1, keepdims=True)
    d2 = jnp.sum(v[2:3] * v[3:4], axis=1, keepdims=True)
    return jnp.exp(d1) - jnp.exp(d2) + lambda_init


def _subln(d, w, lambda_init):
    return _rms(d, w) * (1.0 - lambda_init)


def _attn_prompt_kernel(lam_ref, sub_ref, qt_ref, k_ref, vt_ref, o_ref, qs_s, s_s, m_s, l_s, acc_s,
                        *, tq, tk, lambda_init):
    qi = pl.program_id(2)
    qt = qt_ref[...]
    sub = lax.broadcasted_iota(jnp.int32, qt.shape, 0)
    zero = jnp.zeros_like(qt)
    qs_s[:, :tq] = jnp.where(sub < DIFF_HEAD_DIM, qt, zero)
    qs_s[:, tq:] = jnp.where(sub >= DIFF_HEAD_DIM, qt, zero)
    m_s[...] = jnp.full(m_s.shape, -jnp.inf, F32)
    l_s[...] = jnp.zeros(l_s.shape, F32)
    acc_s[...] = jnp.zeros(acc_s.shape, F32)
    strips = [slice(c * ATTN_STRIP, (c + 1) * ATTN_STRIP) for c in range(2 * tq // ATTN_STRIP)]
    krow = lax.broadcasted_iota(jnp.int32, (tk, ATTN_STRIP), 0)
    qcol = lax.broadcasted_iota(jnp.int32, (tk, ATTN_STRIP), 1)

    def scores_into(slot, j):
        start = pl.multiple_of(j * tk, tk)
        kb = k_ref[pl.ds(start, tk), :]
        for cols in strips:
            s_s[slot, :, cols] = _dot(kb, qs_s[:, cols])

    def reduce_block(j, diag, prefetch):
        slot = lax.rem(j, 2)
        vt = vt_ref[j]
        for c, cols in enumerate(strips):
            if diag is not None:
                q_lo = (c * ATTN_STRIP) % tq
                k_lo = diag * tk
                if k_lo > q_lo + ATTN_STRIP - 1:
                    continue
            s = s_s[slot, :, cols]
            if diag is not None and k_lo + tk - 1 > q_lo:
                s = jnp.where(krow + k_lo <= qcol + q_lo, s, -jnp.inf)
            m = m_s[:, cols]
            m_new = jnp.maximum(m, jnp.max(s, axis=0, keepdims=True))
            alpha = jnp.exp(m - m_new)
            p = jnp.exp(s - m_new)
            m_s[:, cols] = m_new
            l_s[:, cols] = alpha * l_s[:, cols] + jnp.sum(p, axis=0, keepdims=True)
            acc_s[:, cols] = alpha * acc_s[:, cols] + _dot(vt, p.astype(BF16))
        if prefetch:
            scores_into(1 - slot, j + 1)

    n_diag = tq // tk
    first_diag = qi * n_diag
    scores_into(0, 0)

    def body(j, _):
        reduce_block(j, None, True)
        return 0

    lax.fori_loop(0, first_diag, body, 0)
    for d in range(n_diag):
        reduce_block(first_diag + d, d, d + 1 < n_diag)
    ot = acc_s[...] * (1.0 / l_s[...])
    dt = ot[:, :tq] - _lambda(lam_ref, lambda_init) * ot[:, tq:]
    o_ref[...] = _subln(dt.T, sub_ref[...], lambda_init).astype(o_ref.dtype)


def _attn_prompt(qt, kb, vt, lam_vecs, sub_w, lambda_init):
    batch, _, nq, _, tq = qt.shape
    nk, tk = vt.shape[2], vt.shape[4]
    seq = nq * tq
    q_spec = pl.BlockSpec((None, None, None, HEAD_LANES, tq), lambda b, h, i: (b, h, i, 0, 0))
    k_spec = pl.BlockSpec((seq, HEAD_LANES), lambda b, h, i: (b, h))
    v_spec = pl.BlockSpec((None, None, nk, HEAD_LANES, tk), lambda b, h, i: (b, h, 0, 0, 0))
    return pl.pallas_call(
        functools.partial(_attn_prompt_kernel, tq=tq, tk=tk, lambda_init=lambda_init),
        grid=(batch, DIFF_HEADS, nq),
        in_specs=[_const_spec(lam_vecs.shape), _const_spec(sub_w.shape), q_spec, k_spec, v_spec],
        out_specs=pl.BlockSpec((tq, HEAD_LANES), lambda b, h, i: (b * nq + i, h)),
        out_shape=jax.ShapeDtypeStruct(kb.shape, BF16),
        scratch_shapes=[pltpu.VMEM((HEAD_LANES, 2 * tq), BF16), pltpu.VMEM((2, tk, 2 * tq), F32),
                        pltpu.VMEM((1, 2 * tq), F32), pltpu.VMEM((1, 2 * tq), F32),
                        pltpu.VMEM((HEAD_LANES, 2 * tq), F32)],
        compiler_params=_cparams(("parallel", "parallel", "arbitrary"), VMEM_LIMIT_SMALL),
        name="attn_prompt",
    )(lam_vecs, sub_w, qt, kb, vt)


def _attn_sample_kernel(pt_ref, lam_ref, sub_ref, q_ref, kn_ref, vn_ref, *rest, n_pages, n_steps, lambda_init):
    del pt_ref
    k_refs, v_refs = rest[:n_pages], rest[n_pages:2 * n_pages]
    o_ref = rest[2 * n_pages]
    wq_s, m_s, l_s, acc_s = rest[2 * n_pages + 1:]
    step = pl.program_id(1)
    t_new = q_ref.shape[0]
    page = k_refs[0].shape[1]
    head_rows = 2 * t_new

    @pl.when(step == 0)
    def _():
        qt = jnp.concatenate([q_ref[...]] * (2 * DIFF_HEADS), axis=0)
        row = lax.broadcasted_iota(jnp.int32, qt.shape, 0)
        col = lax.broadcasted_iota(jnp.int32, qt.shape, 1)
        same = lax.div(row, t_new) == lax.div(col, DIFF_HEAD_DIM)
        wq_s[...] = jnp.where(same, qt, 0.0).astype(BF16)
        m_s[...] = jnp.full(m_s.shape, -jnp.inf, F32)
        l_s[...] = jnp.zeros(l_s.shape, F32)
        acc_s[...] = jnp.zeros(acc_s.shape, F32)

    def update(scores, values):
        m = m_s[...]
        m_new = m
        for s in scores:
            m_new = jnp.maximum(m_new, jnp.max(s, axis=1, keepdims=True))
        alpha = jnp.exp(m - m_new)
        l = alpha * l_s[...]
        acc = alpha * acc_s[...]
        for s, v_heads in zip(scores, values):
            p = jnp.exp(s - m_new)
            l = l + jnp.sum(p, axis=1, keepdims=True)
            pb = p.astype(BF16)
            acc = acc + jnp.concatenate(
                [_dot(pb[h * head_rows:(h + 1) * head_rows], v_heads[h]) for h in range(DIFF_HEADS)], axis=0)
        m_s[...] = m_new
        l_s[...] = l
        acc_s[...] = acc

    wq = wq_s[...]
    update([_dot(wq, k[...].astype(BF16)) for k in k_refs],
           [[v[pl.ds(h, page, stride=DIFF_HEADS), :].astype(BF16) for h in range(DIFF_HEADS)] for v in v_refs])

    @pl.when(step == n_steps - 1)
    def _():
        pad = jnp.zeros((page - t_new, QK_WIDTH), F32)
        kn = jnp.concatenate([kn_ref[...], pad], axis=0).astype(BF16)
        vn = jnp.concatenate([vn_ref[...], pad], axis=0).astype(BF16)
        s = _dot_nt(wq, kn)
        row = lax.broadcasted_iota(jnp.int32, s.shape, 0)
        col = lax.broadcasted_iota(jnp.int32, s.shape, 1)
        s = jnp.where(col <= lax.rem(row, t_new), s, -jnp.inf)
        update([s], [[vn[:, h * HEAD_LANES:(h + 1) * HEAD_LANES] for h in range(DIFF_HEADS)]])
        o = acc_s[...] / l_s[...]
        lam = _lambda(lam_ref, lambda_init)
        for h in range(DIFF_HEADS):
            r0 = h * head_rows
            d = o[r0:r0 + t_new] - lam * o[r0 + t_new:r0 + head_rows]
            o_ref[:, h * HEAD_LANES:(h + 1) * HEAD_LANES] = _subln(d, sub_ref[...], lambda_init)


def _attn_sample(q, k_new, v_new, cache_k, cache_v, page_table, layer, lam_vecs, sub_w, lambda_init):
    batch, n_log = page_table.shape
    m = q.shape[0]
    t_new = m // batch
    n_layers, n_phys, page = cache_k.shape[:3]
    ck = jnp.transpose(cache_k, (0, 1, 3, 4, 5, 2)).reshape(n_layers, n_phys, QK_WIDTH, page)
    cv = cache_v.reshape(n_layers, n_phys, page * DIFF_HEADS, HEAD_LANES)
    g = _tile(n_log, PAGES_PER_STEP)
    n_steps = n_log // g
    n_rows = 2 * DIFF_HEADS * t_new

    def page_spec(i, shape):
        return pl.BlockSpec((None, None) + shape, lambda b, s, pt: (layer, pt[b, s * g + i], 0, 0))

    tok_spec = pl.BlockSpec((t_new, QK_WIDTH), lambda b, s, pt: (b, 0))
    const2 = lambda shape: pl.BlockSpec(shape, lambda b, s, pt: (0, 0))
    grid_spec = pltpu.PrefetchScalarGridSpec(
        num_scalar_prefetch=1,
        grid=(batch, n_steps),
        in_specs=[const2(lam_vecs.shape), const2(sub_w.shape), tok_spec, tok_spec, tok_spec]
                 + [page_spec(i, (QK_WIDTH, page)) for i in range(g)]
                 + [page_spec(i, (page * DIFF_HEADS, HEAD_LANES)) for i in range(g)],
        out_specs=tok_spec,
        scratch_shapes=[pltpu.VMEM((n_rows, QK_WIDTH), BF16), pltpu.VMEM((n_rows, 1), F32),
                        pltpu.VMEM((n_rows, 1), F32), pltpu.VMEM((n_rows, HEAD_LANES), F32)],
    )
    return pl.pallas_call(
        functools.partial(_attn_sample_kernel, n_pages=g, n_steps=n_steps, lambda_init=lambda_init),
        grid_spec=grid_spec,
        out_shape=jax.ShapeDtypeStruct((m, QK_WIDTH), F32),
        compiler_params=_cparams(("parallel", "arbitrary"), VMEM_LIMIT_SMALL),
        name="attn_sample",
    )(page_table, lam_vecs, sub_w, q, k_new, v_new, *([ck] * g), *([cv] * g))


def _ssd_kernel(xbc_ref, z_ref, dt_ref, cw_ref, cb_ref, dtb_ref, alog_ref, dsk_ref, nw_ref, exp_ref,
                cinit_ref, h0_ref, y_ref, cout_ref, hout_ref, ext_s, st_s, *, t_in, n_chunks):
    c = pl.program_id(1)
    T = SSD_CHUNK
    tail = CONV_WIDTH - 1

    @pl.when(c == 0)
    def _():
        ext_s[0:SUBLANES, :] = jnp.zeros((SUBLANES, CONV_CH), F32)
        ext_s[SUBLANES - tail:SUBLANES, :] = cinit_ref[...]
        st_s[...] = h0_ref[...].reshape(SSM_INNER, SSM_STATE).T

    u = xbc_ref[...]
    ext_s[SUBLANES:SUBLANES + t_in, :] = u
    cw = cw_ref[...]
    conv = cb_ref[...] + cw[tail:tail + 1] * u
    for j in range(tail):
        conv = conv + cw[j:j + 1] * ext_s[SUBLANES - tail + j:SUBLANES - tail + j + t_in, :]
    ext_s[0:SUBLANES, :] = ext_s[t_in:t_in + SUBLANES, :]
    xc = _silu(conv)
    dt = jax.nn.softplus(dt_ref[...] + dtb_ref[...])
    z = z_ref[...]
    if t_in < T:
        xc = jnp.concatenate([xc, jnp.zeros((T - t_in, CONV_CH), F32)], axis=0)
        dt = jnp.concatenate([dt, jnp.zeros((T - t_in, LANES), F32)], axis=0)
        z = jnp.concatenate([z, jnp.zeros((T - t_in, SSM_INNER), F32)], axis=0)

    a = -jnp.exp(alog_ref[...])
    row = lax.broadcasted_iota(jnp.int32, (T, T), 0)
    col = lax.broadcasted_iota(jnp.int32, (T, T), 1)
    causal = row >= col
    acs = _dot_exact(causal.astype(F32), dt * a)
    acs_t = acs.T
    expand = exp_ref[...]
    dt_x = _dot_exact(dt, expand)
    acs_x = _dot_exact(acs, expand)
    last = acs_x[T - 1:T, :]
    decay_out = jnp.exp(acs_x)
    decay_end = jnp.exp(last - acs_x)
    chunk_decay = jnp.exp(last)

    xs = xc[:, :SSM_INNER]
    xdt = xs * dt_x
    xdt_b = xdt.astype(BF16)
    xw_b = (xdt * decay_end).astype(BF16)
    state = st_s[...]
    state_b = state.astype(BF16)
    lane = lax.broadcasted_iota(jnp.int32, (T, LANES), 1)
    heads_per_group = SSM_HEADS // SSM_GROUPS
    ys = []
    for g in range(SSM_GROUPS):
        b0 = SSM_INNER + g * SSM_STATE
        c0 = SSM_INNER + SSM_GROUPS * SSM_STATE + g * SSM_STATE
        bg_t = xc[:, b0:b0 + SSM_STATE].T.astype(BF16)
        cg = xc[:, c0:c0 + SSM_STATE].astype(BF16)
        gl = slice(g * GROUP_LANES, (g + 1) * GROUP_LANES)
        cb = _dot(cg, bg_t)
        y_off = _dot(cg, state_b[:, gl]) * decay_out[:, gl]
        st_s[:, gl] = state[:, gl] * chunk_decay[:, gl] + _dot(bg_t, xw_b[:, gl])
        for pair in range(heads_per_group // 2):
            h0 = g * heads_per_group + 2 * pair
            xp = xdt_b[:, h0 * SSM_HEAD_DIM:(h0 + 2) * SSM_HEAD_DIM]
            outs = []
            for hh in (h0, h0 + 1):
                seg = acs[:, hh:hh + 1] - acs_t[hh:hh + 1, :]
                w = cb * jnp.exp(jnp.where(causal, seg, -jnp.inf))
                outs.append(_dot(w.astype(BF16), xp))
            ys.append(jnp.where(lane < SSM_HEAD_DIM, outs[0], outs[1])
                      + y_off[:, 2 * pair * SSM_HEAD_DIM:(2 * pair + 2) * SSM_HEAD_DIM])
    y = jnp.concatenate(ys, axis=1) + dsk_ref[...] * xs
    y = y * _silu(z)
    nw = nw_ref[...]
    parts = []
    for g in range(SSM_GROUPS):
        gl = slice(g * GROUP_LANES, (g + 1) * GROUP_LANES)
        parts.append(_rms(y[:, gl], nw[:, gl]))
    y_ref[...] = jnp.concatenate(parts, axis=1)[:t_in].astype(y_ref.dtype)

    @pl.when(c == n_chunks - 1)
    def _():
        cout_ref[...] = ext_s[SUBLANES - tail:SUBLANES, :]
        hout_ref[...] = st_s[...].T.reshape(SSM_HEADS, SSM_HEAD_DIM, SSM_STATE)


def _ssd(xbc, z, dt_raw, p, conv_init, h0, batch, seq, y_dtype):
    t_in = _tile(seq, SSD_CHUNK)
    n_chunks = seq // t_in
    assert t_in == SSD_CHUNK or n_chunks == 1
    tail = CONV_WIDTH - 1
    rows = lambda w: pl.BlockSpec((t_in, w), lambda b, c: (b * n_chunks + c, 0))
    per_b3 = lambda s: pl.BlockSpec((None,) + s, lambda b, c: (b,) + (0,) * len(s))
    consts = [p['conv_w'], p['conv_b'], p['dt_bias'], p['a_log'], p['d_skip'], p['ssm_norm_w'], p['expand']]
    return pl.pallas_call(
        functools.partial(_ssd_kernel, t_in=t_in, n_chunks=n_chunks),
        grid=(batch, n_chunks),
        in_specs=[rows(CONV_CH), rows(SSM_INNER), rows(LANES)]
                 + [pl.BlockSpec(a.shape, lambda b, c: (0, 0)) for a in consts]
                 + [per_b3((tail, CONV_CH)), per_b3((SSM_HEADS, SSM_HEAD_DIM, SSM_STATE))],
        out_specs=[rows(SSM_INNER), per_b3((tail, CONV_CH)), per_b3((SSM_HEADS, SSM_HEAD_DIM, SSM_STATE))],
        out_shape=[jax.ShapeDtypeStruct((batch * seq, SSM_INNER), y_dtype),
                   jax.ShapeDtypeStruct((batch, tail, CONV_CH), F32),
                   jax.ShapeDtypeStruct((batch, SSM_HEADS, SSM_HEAD_DIM, SSM_STATE), F32)],
        scratch_shapes=[pltpu.VMEM((SUBLANES + t_in, CONV_CH), F32), pltpu.VMEM((SSM_STATE, SSM_INNER), F32)],
        compiler_params=_cparams(("parallel", "arbitrary"), VMEM_LIMIT_SMALL),
        name="ssd",
    )(xbc, z, dt_raw, *consts, conv_init, h0)


def _pool_kernel(x_ref, nw_ref, init_ref, pw_ref, ps_ref, o_ref, pout_ref, ext_s, *, tp, n_tiles, pos0):
    t = pl.program_id(1)
    halo = POOL_HALO

    @pl.when(t == 0)
    def _():
        ext_s[0:halo, :] = jnp.zeros((halo, ext_s.shape[1]), F32)
        ext_s[halo - POOL_STATE_LEN:halo, :] = init_ref[...]

    x = x_ref[...]
    u = _rms(x, nw_ref[...])
    ext_s[halo:halo + tp, :] = u
    e = ext_s[...]
    gc = POOL_GROUP_CH
    sums = [e]
    for lvl in range(len(POOL_WINDOWS)):
        prev = sums[-1][:, gc:] if lvl else sums[-1]
        sums.append(prev + pltpu.roll(prev, 1 << lvl, 0))
    pos = pos0 + t * tp + lax.broadcasted_iota(jnp.int32, (tp, 1), 0)
    mixed = []
    for g, w in enumerate(POOL_WINDOWS):
        cnt = jnp.minimum(w, pos + 1).astype(F32)
        win = sums[g + 1][halo:, :gc]
        pooled = win / cnt - u[:, g * gc:(g + 1) * gc]
        mixed.append(_dot(pooled.astype(BF16), pw_ref[g]))
    o_ref[...] = x + jnp.concatenate(mixed, axis=1) * ps_ref[...]

    @pl.when(t == n_tiles - 1)
    def _():
        pout_ref[...] = ext_s[tp + halo - POOL_STATE_LEN:tp + halo, :]

    ext_s[0:halo, :] = ext_s[tp:tp + halo, :]


def _pool(x, nw, pool_init, pool_w, w_idx, pool_scale, batch, seq, pos0):
    d = x.shape[1]
    tp = _tile(seq, TOKEN_TILE)
    n_tiles = seq // tp
    rows = pl.BlockSpec((tp, d), lambda b, t: (b * n_tiles + t, 0))
    state = pl.BlockSpec((None, POOL_STATE_LEN, d), lambda b, t: (b, 0, 0))
    return pl.pallas_call(
        functools.partial(_pool_kernel, tp=tp, n_tiles=n_tiles, pos0=pos0),
        grid=(batch, n_tiles),
        in_specs=[rows, pl.BlockSpec(nw.shape, lambda b, t: (0, 0)), state,
                  _stacked_spec(pool_w, (w_idx,)), pl.BlockSpec(pool_scale.shape, lambda b, t: (0, 0))],
        out_specs=[rows, state],
        out_shape=[jax.ShapeDtypeStruct(x.shape, F32), jax.ShapeDtypeStruct((batch, POOL_STATE_LEN, d), F32)],
        scratch_shapes=[pltpu.VMEM((POOL_HALO + tp, d), F32)],
        compiler_params=_cparams(("parallel", "arbitrary"), VMEM_LIMIT_SMALL),
        name="pool",
    )(x, nw, pool_init, pool_w, pool_scale)


def _prep_params(norm_w, ffn_w_gate, ffn_w_up, ffn_w_down, ab_w_in, ab_w_out, ab_conv_w, ab_conv_b, ab_dt_bias,
                 ab_a_log, ab_d_skip, ab_ssm_norm_w, ab_lambda_q1, ab_lambda_k1, ab_lambda_q2, ab_lambda_k2,
                 ab_subln_w, pool_w, pool_scale, final_norm_w):
    n_ab = ab_w_in.shape[0]
    d = ab_w_in.shape[1]

    def lane_pad(v):
        return jnp.pad(v, ((0, 0), (0, LANES - v.shape[1])))[:, None, :]

    expand = np.zeros((LANES, SSM_INNER), np.float32)
    for h in range(SSM_HEADS):
        expand[h, h * SSM_HEAD_DIM:(h + 1) * SSM_HEAD_DIM] = 1.0
    return dict(
        norm_w=norm_w[:, :, None, :],
        wg=ffn_w_gate.astype(BF16), wu=ffn_w_up.astype(BF16), wd=ffn_w_down.astype(BF16),
        w_in=ab_w_in.astype(BF16),
        w_dt=jnp.pad(ab_w_in[:, :, MAIN_PROJ:], ((0, 0), (0, 0), (0, LANES - SSM_HEADS))).astype(BF16),
        w_out=ab_w_out.astype(BF16),
        conv_w=ab_conv_w, conv_b=ab_conv_b[:, None, :],
        dt_bias=lane_pad(ab_dt_bias), a_log=lane_pad(ab_a_log),
        d_skip=jnp.repeat(ab_d_skip, SSM_HEAD_DIM, axis=1)[:, None, :],
        ssm_norm_w=ab_ssm_norm_w[:, None, :],
        lam=jnp.stack([ab_lambda_q1, ab_lambda_k1, ab_lambda_q2, ab_lambda_k2], axis=1),
        subln_w=ab_subln_w[:, None, :],
        pool_w=pool_w.astype(BF16), pool_scale=pool_scale[:, None, :],
        final_norm_w=final_norm_w[None, :],
        expand=jnp.asarray(expand),
    )


def _trunk(x, batch, seq, pos0, conv_init, ssm_init, pool_init, p, paged=None):
    depth = p['wg'].shape[0]
    m = x.shape[0]
    prompt = paged is None
    new_k, new_v, new_conv, new_ssm, new_pool = [], [], [], [], []
    cos, sin = _rope_tables(seq if prompt else m, pos0, seq)
    ffn_w = (p['wg'], p['wu'], p['wd'])
    mix = None
    for layer in range(depth):
        nw = p['norm_w'][layer]
        i = layer // 2
        x = _ffn(x, nw[0], *ffn_w, (layer, 0), mix=mix)
        mix = None
        if layer % 2 == 0:
            lambda_init = 0.8 - 0.6 * math.exp(-0.3 * layer)
            lam, sub_w = p['lam'][i], p['subln_w'][i]
            proj = _inproj(x, nw[1], p['w_in'], i, p['w_dt'][i], cos, sin, prompt_layout=prompt)
            if prompt:
                qt, kt, kb, v4, vt, z, xbc, dt_raw = proj
                attn = _attn_prompt(qt, kb, vt, lam, sub_w, lambda_init)
                k = jnp.transpose(kt.reshape(batch, DIFF_HEADS, 2, DIFF_HEAD_DIM, seq), (0, 4, 1, 2, 3))
                v = v4
            else:
                q, k, v, z, xbc, dt_raw = proj
                attn = _attn_sample(q, k, v, *paged, i, lam, sub_w, lambda_init)
            ssm_p = dict(conv_w=p['conv_w'][i], conv_b=p['conv_b'][i], dt_bias=p['dt_bias'][i], a_log=p['a_log'][i],
                         d_skip=p['d_skip'][i], ssm_norm_w=p['ssm_norm_w'][i], expand=p['expand'])
            y, cb, hs = _ssd(xbc, z, dt_raw, ssm_p, conv_init[i], ssm_init[i], batch, seq, BF16 if prompt else F32)
            new_k.append(k.reshape(batch, seq, DIFF_HEADS, 2, DIFF_HEAD_DIM))
            new_v.append(v.reshape(batch, seq, DIFF_HEADS, HEAD_LANES))
            new_conv.append(cb)
            new_ssm.append(hs)
            mix = (attn, y, p['w_out'], i)
        else:
            x, pb = _pool(x, nw[1], pool_init[i], p['pool_w'], i, p['pool_scale'][i], batch, seq, pos0)
            new_pool.append(pb)
        final_w = p['final_norm_w'] if layer == depth - 1 else None
        x = _ffn(x, nw[2], *ffn_w, (layer, 1), mix=mix, final_w=final_w)
        mix = None
    return x, jnp.stack(new_k), jnp.stack(new_v), jnp.stack(new_conv), jnp.stack(new_ssm), jnp.stack(new_pool)


def kernel(x_prompt, x_sample, cache_k, cache_v, state_conv, state_ssm, state_pool, page_table, norm_w, ffn_w_gate, ffn_w_up, ffn_w_down, ab_w_in, ab_w_out, ab_conv_w, ab_conv_b, ab_dt_bias, ab_a_log, ab_d_skip, ab_ssm_norm_w, ab_lambda_q1, ab_lambda_k1, ab_lambda_q2, ab_lambda_k2, ab_subln_w, pool_w, pool_scale, final_norm_w):
    p = _prep_params(norm_w, ffn_w_gate, ffn_w_up, ffn_w_down, ab_w_in, ab_w_out, ab_conv_w, ab_conv_b, ab_dt_bias,
                     ab_a_log, ab_d_skip, ab_ssm_norm_w, ab_lambda_q1, ab_lambda_k1, ab_lambda_q2, ab_lambda_k2,
                     ab_subln_w, pool_w, pool_scale, final_norm_w)
    n_ab, n_pool = state_conv.shape[0], state_pool.shape[0]
    bp, lp, d = x_prompt.shape
    bs, ls, _ = x_sample.shape

    zeros = lambda shape: jnp.zeros(shape, F32)
    yp, kp, vp, cp, sp, pp = _trunk(
        x_prompt.reshape(bp * lp, d), bp, lp, 0,
        zeros((n_ab, bp, CONV_WIDTH - 1, CONV_CH)), zeros((n_ab, bp, SSM_HEADS, SSM_HEAD_DIM, SSM_STATE)),
        zeros((n_pool, bp, POOL_STATE_LEN, d)), p)

    past_len = page_table.shape[1] * cache_k.shape[2]
    ys, ks, vs, cs, ss, ps = _trunk(
        x_sample.reshape(bs * ls, d), bs, ls, past_len, state_conv, state_ssm, state_pool, p,
        paged=(cache_k, cache_v, page_table))
    return (yp.reshape(bp, lp, d), ys.reshape(bs, ls, d), kp, vp, cp, sp, pp, ks, vs, cs, ss, ps)
```

```python
import functools
import math

import numpy as np
import jax
import jax.numpy as jnp
from jax import lax
from jax.experimental import pallas as pl
from jax.experimental.pallas import tpu as pltpu

F32 = jnp.float32
BF16 = jnp.bfloat16

RMS_EPS = 1e-6
ROPE_THETA = 10000.0
LOG2_E = 1.4426950408889634

DIFF_HEADS = 4
DIFF_HEAD_DIM = 64
HEAD_LANES = 2 * DIFF_HEAD_DIM
QK_WIDTH = DIFF_HEADS * HEAD_LANES
SSM_INNER = 512
SSM_HEAD_DIM = 64
SSM_HEADS = 8
SSM_GROUPS = 2
SSM_STATE = 128
GROUP_LANES = SSM_INNER // SSM_GROUPS
CONV_WIDTH = 4
CONV_CH = SSM_INNER + 2 * SSM_GROUPS * SSM_STATE
SSD_CHUNK = 128
SSD_CHUNKS_PER_STEP = 4
POOL_WINDOWS = (2, 4, 8, 16)
POOL_GROUP_CH = 256
POOL_HALO = 16
POOL_STATE_LEN = 15
MAIN_PROJ = 2 * QK_WIDTH + QK_WIDTH + SSM_INNER + CONV_CH

LANES = 128
SUBLANES = 8
MXU_DIM = 256
VMEM_LIMIT_BIG = 58 * 1024 * 1024
VMEM_LIMIT_SMALL = 40 * 1024 * 1024

FFN_CHUNK = MXU_DIM
TOKEN_TILE = 512
ATTN_Q_TILE = 512
ATTN_K_TILE = 256
ATTN_SUM_ROWS = 16
ATTN_LOOKAHEAD = 4
PAGES_PER_STEP = 16


def _cparams(sem, vmem):
    return pltpu.CompilerParams(dimension_semantics=sem, vmem_limit_bytes=vmem)


def _tile(m, pref):
    t = min(m, pref)
    while m % t:
        t //= 2
    return t


def _rms(x, w):
    ms = jnp.mean(x * x, axis=-1, keepdims=True)
    return x * lax.rsqrt(ms + RMS_EPS) * w


def _silu(x):
    return x * jax.nn.sigmoid(x)


def _dot(a, b):
    return jnp.dot(a, b, preferred_element_type=F32)


def _dot_nt(a, b):
    return lax.dot_general(a, b, (((1,), (1,)), ((), ())), preferred_element_type=F32)


def _split3(a):
    hi = a.astype(BF16)
    r = a - hi.astype(F32)
    mid = r.astype(BF16)
    return hi, mid, (r - mid.astype(F32)).astype(BF16)


def _dot_exact(a, b, exact_lhs=False):
    if exact_lhs:
        a = a.astype(BF16)
        return sum(_dot(a, t) for t in _split3(b))
    b = b.astype(BF16)
    return sum(_dot(t, b) for t in _split3(a))


def _row_spec(tile, width):
    return pl.BlockSpec((tile, width), lambda i: (i, 0))


def _const_spec(shape):
    return pl.BlockSpec(shape, lambda *_: (0,) * len(shape))


def _rope_table_kernel(inv_ref, cos_ref, sin_ref, *, pos0, period):
    shape = cos_ref.shape
    row = lax.broadcasted_iota(jnp.int32, shape, 0)
    lane = lax.broadcasted_iota(jnp.int32, shape, 1)
    pos = pos0 + lax.rem(row, period)
    ang = pos.astype(F32) * inv_ref[...]
    cos_ref[...] = jnp.cos(ang)
    s = jnp.sin(ang)
    sin_ref[...] = jnp.where(lax.rem(lane, DIFF_HEAD_DIM) < DIFF_HEAD_DIM // 2, -s, s)


def _rope_tables(rows, pos0, period):
    half = DIFF_HEAD_DIM // 2
    inv = 1.0 / (ROPE_THETA ** (jnp.arange(0, DIFF_HEAD_DIM, 2, dtype=F32) / DIFF_HEAD_DIM))
    inv = jnp.tile(inv, LANES // half)[None, :]
    return pl.pallas_call(
        functools.partial(_rope_table_kernel, pos0=pos0, period=period),
        out_shape=(jax.ShapeDtypeStruct((rows, LANES), F32),) * 2,
        name="rope_table",
    )(inv)


def _ffn_kernel(*refs, n_chunks, has_mix, has_final):
    it = iter(refs)
    x_ref = next(it)
    if has_mix:
        a_ref, y_ref, wo_ref = next(it), next(it), next(it)
    nw_ref, wg_ref, wu_ref, wd_ref = next(it), next(it), next(it), next(it)
    fw_ref = next(it) if has_final else None
    o_ref = next(it)

    x = x_ref[...]
    if has_mix:
        half = a_ref.shape[1]
        x = x + _dot(a_ref[...].astype(BF16), wo_ref[:half, :]) + _dot(y_ref[...].astype(BF16), wo_ref[half:, :])
    u = _rms(x, nw_ref[...]).astype(BF16)
    acc = jnp.zeros(x.shape, F32)
    for c in range(n_chunks):
        sl = slice(c * FFN_CHUNK, (c + 1) * FFN_CHUNK)
        g = _dot(u, wg_ref[:, sl])
        up = _dot(u, wu_ref[:, sl])
        h = (_silu(g) * up).astype(BF16)
        acc = acc + _dot(h, wd_ref[sl, :])
    y = x + 0.5 * acc
    if has_final:
        y = _rms(y, fw_ref[...])
    o_ref[...] = y


def _stacked_spec(w, idx):
    n_lead = len(idx)
    return pl.BlockSpec((None,) * n_lead + w.shape[n_lead:], lambda *_: tuple(idx) + (0,) * (w.ndim - n_lead))


def _ffn(x, nw, wg, wu, wd, w_idx, mix=None, final_w=None):
    m, d = x.shape
    hidden = wg.shape[-1]
    tm = _tile(m, TOKEN_TILE)
    args, specs = [x], [_row_spec(tm, d)]
    if mix is not None:
        a, y, wo, wo_idx = mix
        args += [a, y, wo]
        specs += [_row_spec(tm, a.shape[1]), _row_spec(tm, y.shape[1]), _stacked_spec(wo, (wo_idx,))]
    args += [nw, wg, wu, wd]
    specs += [_const_spec(nw.shape), _stacked_spec(wg, w_idx), _stacked_spec(wu, w_idx), _stacked_spec(wd, w_idx)]
    if final_w is not None:
        args.append(final_w)
        specs.append(_const_spec(final_w.shape))
    return pl.pallas_call(
        functools.partial(_ffn_kernel, n_chunks=hidden // FFN_CHUNK, has_mix=mix is not None,
                          has_final=final_w is not None),
        grid=(m // tm,),
        in_specs=specs,
        out_specs=_row_spec(tm, d),
        out_shape=jax.ShapeDtypeStruct((m, d), F32),
        compiler_params=_cparams(("parallel",), VMEM_LIMIT_BIG),
        name="ffn",
    )(*args)


def _inproj_kernel(x_ref, nw_ref, w_ref, wdt_ref, cos_ref, sin_ref, *out_refs, prompt_layout):
    if prompt_layout:
        qt_ref, kt_ref, kb_ref, v4_ref, vt_ref, z_ref, xbc_ref, dt_ref = out_refs
    else:
        q_ref, k_ref, v_ref, z_ref, xbc_ref, dt_ref = out_refs
    u = _rms(x_ref[...], nw_ref[...]).astype(BF16)
    tm = u.shape[0]
    reps = QK_WIDTH // LANES
    cos = jnp.concatenate([cos_ref[...]] * reps, axis=1)
    sin = jnp.concatenate([sin_ref[...]] * reps, axis=1)
    lane = lax.broadcasted_iota(jnp.int32, (tm, QK_WIDTH), 1)
    low_half = lax.rem(lane, DIFF_HEAD_DIM) < DIFF_HEAD_DIM // 2
    half = DIFF_HEAD_DIM // 2

    def rope(t):
        up = pltpu.roll(t, QK_WIDTH - half, 1)
        down = pltpu.roll(t, half, 1)
        return t * cos + jnp.where(low_half, up, down) * sin

    q_scale = DIFF_HEAD_DIM ** -0.5 * (LOG2_E if prompt_layout else 1.0)
    q = rope(_dot(u, w_ref[:, 0:QK_WIDTH])) * q_scale
    k = rope(_dot(u, w_ref[:, QK_WIDTH:2 * QK_WIDTH]))
    v = _dot(u, w_ref[:, 2 * QK_WIDTH:3 * QK_WIDTH])
    if prompt_layout:
        def head_blocks(t_ref, t):
            tb = t_ref.shape[-1]
            for h in range(DIFF_HEADS):
                for jb in range(tm // tb):
                    t_ref[h, jb] = t[h * HEAD_LANES:(h + 1) * HEAD_LANES, jb * tb:(jb + 1) * tb].astype(BF16)

        head_blocks(qt_ref, q.T)
        kt_ref[...] = k.T
        kb_ref[...] = k.astype(BF16)
        for h in range(DIFF_HEADS):
            v4_ref[pl.ds(h, tm, stride=DIFF_HEADS), :] = v[:, h * HEAD_LANES:(h + 1) * HEAD_LANES]
        head_blocks(vt_ref, v.T)
    else:
        q_ref[...] = q
        k_ref[...] = k
        v_ref[...] = v
    z0 = 3 * QK_WIDTH
    z_ref[...] = _dot(u, w_ref[:, z0:z0 + SSM_INNER])
    xbc_ref[...] = _dot(u, w_ref[:, z0 + SSM_INNER:z0 + SSM_INNER + CONV_CH])
    dt_ref[...] = _dot(u, wdt_ref[...])


def _inproj(x, nw, w_in, w_idx, w_dt, cos, sin, prompt_layout):
    m, d = x.shape
    period = cos.shape[0]
    tm = _tile(period, TOKEN_TILE)
    n_per = period // tm
    tab_spec = pl.BlockSpec((tm, LANES), lambda i: (i % n_per, 0))
    rows = lambda w, dt: (_row_spec(tm, w), jax.ShapeDtypeStruct((m, w), dt))
    tail = [rows(SSM_INNER, F32), rows(CONV_CH, F32), rows(LANES, F32)]
    if prompt_layout:
        n_seq = m // period

        def blocked(tile):
            ta = _tile(tm, tile)
            return (pl.BlockSpec((None, DIFF_HEADS, tm // ta, HEAD_LANES, ta), lambda i: (i // n_per, 0, i % n_per, 0, 0)),
                    jax.ShapeDtypeStruct((n_seq, DIFF_HEADS, period // ta, HEAD_LANES, ta), BF16))

        kt_out = (pl.BlockSpec((None, QK_WIDTH, tm), lambda i: (i // n_per, 0, i % n_per)),
                  jax.ShapeDtypeStruct((n_seq, QK_WIDTH, period), F32))
        v4_out = (_row_spec(tm * DIFF_HEADS, HEAD_LANES), jax.ShapeDtypeStruct((m * DIFF_HEADS, HEAD_LANES), F32))
        outs = [blocked(ATTN_Q_TILE), kt_out, rows(QK_WIDTH, BF16), v4_out, blocked(ATTN_K_TILE)] + tail
    else:
        outs = [rows(QK_WIDTH, F32)] * 3 + tail
    return pl.pallas_call(
        functools.partial(_inproj_kernel, prompt_layout=prompt_layout),
        grid=(m // tm,),
        in_specs=[_row_spec(tm, d), _const_spec(nw.shape), pl.BlockSpec((None,) + w_in.shape[1:], lambda i: (w_idx, 0, 0)),
                  _const_spec(w_dt.shape), tab_spec, tab_spec],
        out_specs=[spec for spec, _ in outs],
        out_shape=[shape for _, shape in outs],
        compiler_params=_cparams(("parallel",), VMEM_LIMIT_SMALL),
        name="inproj",
    )(x, nw, w_in, w_dt, cos, sin)


def _lambda(lam_ref, lambda_init):
    v = lam_ref[...]
    d1 = jnp.sum(v[0:1] * v[1:2], axis=1, keepdims=True)
    d2 = jnp.sum(v[2:3] * v[3:4], axis=1, keepdims=True)
    return jnp.exp(d1) - jnp.exp(d2) + lambda_init


def _subln(d, w, lambda_init):
    return _rms(d, w) * (1.0 - lambda_init)


def _attn_prompt_kernel(lam_ref, sub_ref, qt_ref, k_ref, vt_ref, o_ref, m_s, acc_s, *, lambda_init):
    n_qb, _, tqb = qt_ref.shape
    n_kb, _, tk = vt_ref.shape
    strip = tk
    n_strips = n_qb * tqb // strip
    sub = lax.broadcasted_iota(jnp.int32, (HEAD_LANES, strip), 0)
    first_comp = sub < DIFF_HEAD_DIM
    krow = lax.broadcasted_iota(jnp.int32, (tk, strip), 0)
    qcol = lax.broadcasted_iota(jnp.int32, (tk, strip), 1)
    causal = krow <= qcol
    zero = jnp.zeros((HEAD_LANES, strip), BF16)

    def cols(qs, comp):
        return slice((comp * n_strips + qs) * strip, (comp * n_strips + qs + 1) * strip)

    units = [(kb, qs, comp) for kb in range(n_kb) for qs in range(kb, n_strips) for comp in range(2)]

    def scores(unit):
        kb, qs, comp = unit
        qb, off = divmod(qs * strip, tqb)
        qt = qt_ref[qb, :, off:off + strip]
        q = jnp.where(first_comp, qt, zero) if comp == 0 else jnp.where(first_comp, zero, qt)
        return _dot(k_ref[kb * tk:(kb + 1) * tk, :], q)

    pending = [scores(u) for u in units[:ATTN_LOOKAHEAD]]
    ones = jnp.ones((ATTN_SUM_ROWS, tk), BF16)
    for i, (kb, qs, comp) in enumerate(units):
        if i + ATTN_LOOKAHEAD < len(units):
            pending.append(scores(units[i + ATTN_LOOKAHEAD]))
        s = pending.pop(0)
        c = cols(qs, comp)
        if comp == 0 and qs == kb:
            vt = jnp.concatenate([vt_ref[kb], ones], axis=0)
        if kb == qs:
            s = jnp.where(causal, s, -jnp.inf)
        if kb == 0:
            m = jnp.max(s, axis=0, keepdims=True)
            acc_s[:, c] = _dot(vt, jnp.exp2(s - m).astype(BF16))
        else:
            m_old = m_s[:, c]
            m = jnp.maximum(m_old, jnp.max(s, axis=0, keepdims=True))
            acc_s[:, c] = jnp.exp2(m_old - m) * acc_s[:, c] + _dot(vt, jnp.exp2(s - m).astype(BF16))
        m_s[:, c] = m
    lam = _lambda(lam_ref, lambda_init)
    for qs in range(n_strips):
        o0, o1 = (acc_s[:HEAD_LANES, cols(qs, comp)] * (1.0 / acc_s[HEAD_LANES:HEAD_LANES + 1, cols(qs, comp)])
                  for comp in range(2))
        d = (o0 - lam * o1).T
        o_ref[qs * strip:(qs + 1) * strip, :] = _subln(d, sub_ref[...], lambda_init).astype(o_ref.dtype)


def _attn_prompt(qt, kb, vt, lam_vecs, sub_w, lambda_init):
    batch, _, nq, _, tq = qt.shape
    nk, tk = vt.shape[2], vt.shape[4]
    seq = nq * tq
    q_spec = pl.BlockSpec((None, None, nq, HEAD_LANES, tq), lambda b, h: (b, h, 0, 0, 0))
    k_spec = pl.BlockSpec((seq, HEAD_LANES), lambda b, h: (b, h))
    v_spec = pl.BlockSpec((None, None, nk, HEAD_LANES, tk), lambda b, h: (b, h, 0, 0, 0))
    return pl.pallas_call(
        functools.partial(_attn_prompt_kernel, lambda_init=lambda_init),
        grid=(batch, DIFF_HEADS),
        in_specs=[pl.BlockSpec(lam_vecs.shape, lambda b, h: (0, 0)), pl.BlockSpec(sub_w.shape, lambda b, h: (0, 0)),
                  q_spec, k_spec, v_spec],
        out_specs=k_spec,
        out_shape=jax.ShapeDtypeStruct(kb.shape, BF16),
        scratch_shapes=[pltpu.VMEM((1, 2 * seq), F32), pltpu.VMEM((HEAD_LANES + ATTN_SUM_ROWS, 2 * seq), F32)],
        compiler_params=_cparams(("parallel", "parallel"), VMEM_LIMIT_SMALL),
        name="attn_prompt",
    )(lam_vecs, sub_w, qt, kb, vt)


def _attn_sample_kernel(pt_ref, lam_ref, sub_ref, q_ref, kn_ref, vn_ref, *rest, n_pages, n_steps, lambda_init):
    del pt_ref
    k_refs, v_refs = rest[:n_pages], rest[n_pages:2 * n_pages]
    o_ref = rest[2 * n_pages]
    wq_s, m_s, l_s, acc_s = rest[2 * n_pages + 1:]
    step = pl.program_id(1)
    t_new = q_ref.shape[0]
    page = k_refs[0].shape[1]
    head_rows = 2 * t_new

    @pl.when(step == 0)
    def _():
        qt = jnp.concatenate([q_ref[...]] * (2 * DIFF_HEADS), axis=0)
        row = lax.broadcasted_iota(jnp.int32, qt.shape, 0)
        col = lax.broadcasted_iota(jnp.int32, qt.shape, 1)
        same = lax.div(row, t_new) == lax.div(col, DIFF_HEAD_DIM)
        wq_s[...] = jnp.where(same, qt, 0.0).astype(BF16)
        m_s[...] = jnp.full(m_s.shape, -jnp.inf, F32)
        l_s[...] = jnp.zeros(l_s.shape, F32)
        acc_s[...] = jnp.zeros(acc_s.shape, F32)

    def update(scores, values):
        m = m_s[...]
        m_new = m
        for s in scores:
            m_new = jnp.maximum(m_new, jnp.max(s, axis=1, keepdims=True))
        alpha = jnp.exp(m - m_new)
        l = alpha * l_s[...]
        acc = alpha * acc_s[...]
        for s, v_heads in zip(scores, values):
            p = jnp.exp(s - m_new)
            l = l + jnp.sum(p, axis=1, keepdims=True)
            pb = p.astype(BF16)
            acc = acc + jnp.concatenate(
                [_dot(pb[h * head_rows:(h + 1) * head_rows], v_heads[h]) for h in range(DIFF_HEADS)], axis=0)
        m_s[...] = m_new
        l_s[...] = l
        acc_s[...] = acc

    wq = wq_s[...]
    update([_dot(wq, k[...].astype(BF16)) for k in k_refs],
           [[v[pl.ds(h, page, stride=DIFF_HEADS), :].astype(BF16) for h in range(DIFF_HEADS)] for v in v_refs])

    @pl.when(step == n_steps - 1)
    def _():
        pad = jnp.zeros((page - t_new, QK_WIDTH), F32)
        kn = jnp.concatenate([kn_ref[...], pad], axis=0).astype(BF16)
        vn = jnp.concatenate([vn_ref[...], pad], axis=0).astype(BF16)
        s = _dot_nt(wq, kn)
        row = lax.broadcasted_iota(jnp.int32, s.shape, 0)
        col = lax.broadcasted_iota(jnp.int32, s.shape, 1)
        s = jnp.where(col <= lax.rem(row, t_new), s, -jnp.inf)
        update([s], [[vn[:, h * HEAD_LANES:(h + 1) * HEAD_LANES] for h in range(DIFF_HEADS)]])
        o = acc_s[...] / l_s[...]
        lam = _lambda(lam_ref, lambda_init)
        for h in range(DIFF_HEADS):
            r0 = h * head_rows
            d = o[r0:r0 + t_new] - lam * o[r0 + t_new:r0 + head_rows]
            o_ref[:, h * HEAD_LANES:(h + 1) * HEAD_LANES] = _subln(d, sub_ref[...], lambda_init)


def _attn_sample(q, k_new, v_new, cache_k, cache_v, page_table, layer, lam_vecs, sub_w, lambda_init):
    batch, n_log = page_table.shape
    m = q.shape[0]
    t_new = m // batch
    n_layers, n_phys, page = cache_k.shape[:3]
    ck = jnp.transpose(cache_k, (0, 1, 3, 4, 5, 2)).reshape(n_layers, n_phys, QK_WIDTH, page)
    cv = cache_v.reshape(n_layers, n_phys, page * DIFF_HEADS, HEAD_LANES)
    g = _tile(n_log, PAGES_PER_STEP)
    n_steps = n_log // g
    n_rows = 2 * DIFF_HEADS * t_new

    def page_spec(i, shape):
        return pl.BlockSpec((None, None) + shape, lambda b, s, pt: (layer, pt[b, s * g + i], 0, 0))

    tok_spec = pl.BlockSpec((t_new, QK_WIDTH), lambda b, s, pt: (b, 0))
    const2 = lambda shape: pl.BlockSpec(shape, lambda b, s, pt: (0, 0))
    grid_spec = pltpu.PrefetchScalarGridSpec(
        num_scalar_prefetch=1,
        grid=(batch, n_steps),
        in_specs=[const2(lam_vecs.shape), const2(sub_w.shape), tok_spec, tok_spec, tok_spec]
                 + [page_spec(i, (QK_WIDTH, page)) for i in range(g)]
                 + [page_spec(i, (page * DIFF_HEADS, HEAD_LANES)) for i in range(g)],
        out_specs=tok_spec,
        scratch_shapes=[pltpu.VMEM((n_rows, QK_WIDTH), BF16), pltpu.VMEM((n_rows, 1), F32),
                        pltpu.VMEM((n_rows, 1), F32), pltpu.VMEM((n_rows, HEAD_LANES), F32)],
    )
    return pl.pallas_call(
        functools.partial(_attn_sample_kernel, n_pages=g, n_steps=n_steps, lambda_init=lambda_init),
        grid_spec=grid_spec,
        out_shape=jax.ShapeDtypeStruct((m, QK_WIDTH), F32),
        compiler_params=_cparams(("parallel", "arbitrary"), VMEM_LIMIT_SMALL),
        name="attn_sample",
    )(page_table, lam_vecs, sub_w, q, k_new, v_new, *([ck] * g), *([cv] * g))


def _ssd_kernel(xbc_ref, z_ref, dt_ref, cw_ref, cb_ref, dtb_ref, alog_ref, dsk_ref, nw_ref, exp_ref,
                cinit_ref, h0_ref, y_ref, cout_ref, hout_ref, ext_s, st_s, *, t_in, n_chunks):
    c = pl.program_id(1)
    T = SSD_CHUNK
    tail = CONV_WIDTH - 1

    @pl.when(c == 0)
    def _():
        ext_s[0:SUBLANES, :] = jnp.zeros((SUBLANES, CONV_CH), F32)
        ext_s[SUBLANES - tail:SUBLANES, :] = cinit_ref[...]
        st_s[...] = h0_ref[...].reshape(SSM_INNER, SSM_STATE).T

    u = xbc_ref[...]
    ext_s[SUBLANES:SUBLANES + t_in, :] = u
    cw = cw_ref[...]
    conv = cb_ref[...] + cw[tail:tail + 1] * u
    for j in range(tail):
        conv = conv + cw[j:j + 1] * ext_s[SUBLANES - tail + j:SUBLANES - tail + j + t_in, :]
    ext_s[0:SUBLANES, :] = ext_s[t_in:t_in + SUBLANES, :]
    xc_all = _silu(conv)
    dt_all = jax.nn.softplus(dt_ref[...] + dtb_ref[...])
    z_all = z_ref[...]
    if t_in < T:
        xc_all = jnp.concatenate([xc_all, jnp.zeros((T - t_in, CONV_CH), F32)], axis=0)
        dt_all = jnp.concatenate([dt_all, jnp.zeros((T - t_in, LANES), F32)], axis=0)
        z_all = jnp.concatenate([z_all, jnp.zeros((T - t_in, SSM_INNER), F32)], axis=0)

    a = -jnp.exp(alog_ref[...])
    row = lax.broadcasted_iota(jnp.int32, (T, T), 0)
    col = lax.broadcasted_iota(jnp.int32, (T, T), 1)
    causal = row >= col
    tri = causal.astype(F32)
    expand = exp_ref[...]
    lane = lax.broadcasted_iota(jnp.int32, (T, LANES), 1)
    heads_per_group = SSM_HEADS // SSM_GROUPS
    nw = nw_ref[...]
    for ci in range(max(1, t_in // T)):
        rows = slice(ci * T, (ci + 1) * T)
        xc, dt, z = xc_all[rows], dt_all[rows], z_all[rows]
        acs = _dot_exact(tri, dt * a, exact_lhs=True)
        acs_t = acs.T
        both_x = _dot_exact(jnp.concatenate([dt, acs], axis=0), expand)
        dt_x, acs_x = both_x[:T], both_x[T:]
        last = acs_x[T - 1:T, :]
        decay_out = jnp.exp(acs_x)
        decay_end = jnp.exp(last - acs_x)
        chunk_decay = jnp.exp(last)

        xs = xc[:, :SSM_INNER]
        xdt = xs * dt_x
        xdt_b = xdt.astype(BF16)
        xw_b = (xdt * decay_end).astype(BF16)
        state = st_s[...]
        state_b = state.astype(BF16)
        ys = []
        for g in range(SSM_GROUPS):
            b0 = SSM_INNER + g * SSM_STATE
            c0 = SSM_INNER + SSM_GROUPS * SSM_STATE + g * SSM_STATE
            bg_t = xc[:, b0:b0 + SSM_STATE].T.astype(BF16)
            cg = xc[:, c0:c0 + SSM_STATE].astype(BF16)
            gl = slice(g * GROUP_LANES, (g + 1) * GROUP_LANES)
            cb = _dot(cg, bg_t)
            y_off = _dot(cg, state_b[:, gl]) * decay_out[:, gl]
            st_s[:, gl] = state[:, gl] * chunk_decay[:, gl] + _dot(bg_t, xw_b[:, gl])
            for pair in range(heads_per_group // 2):
                h0 = g * heads_per_group + 2 * pair
                xp = xdt_b[:, h0 * SSM_HEAD_DIM:(h0 + 2) * SSM_HEAD_DIM]
                outs = []
                for hh in (h0, h0 + 1):
                    seg = acs[:, hh:hh + 1] - acs_t[hh:hh + 1, :]
                    w = cb * jnp.exp(jnp.where(causal, seg, -jnp.inf))
                    outs.append(_dot(w.astype(BF16), xp))
                ys.append(jnp.where(lane < SSM_HEAD_DIM, outs[0], outs[1])
                          + y_off[:, 2 * pair * SSM_HEAD_DIM:(2 * pair + 2) * SSM_HEAD_DIM])
        y = jnp.concatenate(ys, axis=1) + dsk_ref[...] * xs
        y = y * _silu(z)
        parts = []
        for g in range(SSM_GROUPS):
            gl = slice(g * GROUP_LANES, (g + 1) * GROUP_LANES)
            parts.append(_rms(y[:, gl], nw[:, gl]))
        y = jnp.concatenate(parts, axis=1).astype(y_ref.dtype)
        if t_in < T:
            y_ref[...] = y[:t_in]
        else:
            y_ref[rows, :] = y

    @pl.when(c == n_chunks - 1)
    def _():
        cout_ref[...] = ext_s[SUBLANES - tail:SUBLANES, :]
        hout_ref[...] = st_s[...].T.reshape(SSM_HEADS, SSM_HEAD_DIM, SSM_STATE)


def _ssd(xbc, z, dt_raw, p, conv_init, h0, batch, seq, y_dtype):
    t_in = _tile(seq, SSD_CHUNK * SSD_CHUNKS_PER_STEP)
    n_chunks = seq // t_in
    assert t_in % SSD_CHUNK == 0 or n_chunks == 1
    tail = CONV_WIDTH - 1
    rows = lambda w: pl.BlockSpec((t_in, w), lambda b, c: (b * n_chunks + c, 0))
    per_b3 = lambda s: pl.BlockSpec((None,) + s, lambda b, c: (b,) + (0,) * len(s))
    consts = [p['conv_w'], p['conv_b'], p['dt_bias'], p['a_log'], p['d_skip'], p['ssm_norm_w'], p['expand']]
    return pl.pallas_call(
        functools.partial(_ssd_kernel, t_in=t_in, n_chunks=n_chunks),
        grid=(batch, n_chunks),
        in_specs=[rows(CONV_CH), rows(SSM_INNER), rows(LANES)]
                 + [pl.BlockSpec(a.shape, lambda b, c: (0, 0)) for a in consts]
                 + [per_b3((tail, CONV_CH)), per_b3((SSM_HEADS, SSM_HEAD_DIM, SSM_STATE))],
        out_specs=[rows(SSM_INNER), per_b3((tail, CONV_CH)), per_b3((SSM_HEADS, SSM_HEAD_DIM, SSM_STATE))],
        out_shape=[jax.ShapeDtypeStruct((batch * seq, SSM_INNER), y_dtype),
                   jax.ShapeDtypeStruct((batch, tail, CONV_CH), F32),
                   jax.ShapeDtypeStruct((batch, SSM_HEADS, SSM_HEAD_DIM, SSM_STATE), F32)],
        scratch_shapes=[pltpu.VMEM((SUBLANES + t_in, CONV_CH), F32), pltpu.VMEM((SSM_STATE, SSM_INNER), F32)],
        compiler_params=_cparams(("parallel", "arbitrary"), VMEM_LIMIT_SMALL),
        name="ssd",
    )(xbc, z, dt_raw, *consts, conv_init, h0)


def _pool_kernel(x_ref, nw_ref, init_ref, pw_ref, ps_ref, o_ref, pout_ref, ext_s, *, tp, n_tiles, pos0):
    t = pl.program_id(1)
    halo = POOL_HALO

    @pl.when(t == 0)
    def _():
        ext_s[0:halo, :] = jnp.zeros((halo, ext_s.shape[1]), F32)
        ext_s[halo - POOL_STATE_LEN:halo, :] = init_ref[...]

    x = x_ref[...]
    u = _rms(x, nw_ref[...])
    ext_s[halo:halo + tp, :] = u
    e = ext_s[...]
    gc = POOL_GROUP_CH
    sums = [e]
    for lvl in range(len(POOL_WINDOWS)):
        prev = sums[-1][:, gc:] if lvl else sums[-1]
        sums.append(prev + pltpu.roll(prev, 1 << lvl, 0))
    pos = pos0 + t * tp + lax.broadcasted_iota(jnp.int32, (tp, 1), 0)
    mixed = []
    for g, w in enumerate(POOL_WINDOWS):
        cnt = jnp.minimum(w, pos + 1).astype(F32)
        win = sums[g + 1][halo:, :gc]
        pooled = win / cnt - u[:, g * gc:(g + 1) * gc]
        mixed.append(_dot(pooled.astype(BF16), pw_ref[g]))
    o_ref[...] = x + jnp.concatenate(mixed, axis=1) * ps_ref[...]

    @pl.when(t == n_tiles - 1)
    def _():
        pout_ref[...] = ext_s[tp + halo - POOL_STATE_LEN:tp + halo, :]

    ext_s[0:halo, :] = ext_s[tp:tp + halo, :]


def _pool(x, nw, pool_init, pool_w, w_idx, pool_scale, batch, seq, pos0):
    d = x.shape[1]
    tp = _tile(seq, TOKEN_TILE)
    n_tiles = seq // tp
    rows = pl.BlockSpec((tp, d), lambda b, t: (b * n_tiles + t, 0))
    state = pl.BlockSpec((None, POOL_STATE_LEN, d), lambda b, t: (b, 0, 0))
    return pl.pallas_call(
        functools.partial(_pool_kernel, tp=tp, n_tiles=n_tiles, pos0=pos0),
        grid=(batch, n_tiles),
        in_specs=[rows, pl.BlockSpec(nw.shape, lambda b, t: (0, 0)), state,
                  _stacked_spec(pool_w, (w_idx,)), pl.BlockSpec(pool_scale.shape, lambda b, t: (0, 0))],
        out_specs=[rows, state],
        out_shape=[jax.ShapeDtypeStruct(x.shape, F32), jax.ShapeDtypeStruct((batch, POOL_STATE_LEN, d), F32)],
        scratch_shapes=[pltpu.VMEM((POOL_HALO + tp, d), F32)],
        compiler_params=_cparams(("parallel", "arbitrary"), VMEM_LIMIT_SMALL),
        name="pool",
    )(x, nw, pool_init, pool_w, pool_scale)


def _prep_params(norm_w, ffn_w_gate, ffn_w_up, ffn_w_down, ab_w_in, ab_w_out, ab_conv_w, ab_conv_b, ab_dt_bias,
                 ab_a_log, ab_d_skip, ab_ssm_norm_w, ab_lambda_q1, ab_lambda_k1, ab_lambda_q2, ab_lambda_k2,
                 ab_subln_w, pool_w, pool_scale, final_norm_w):
    def lane_pad(v):
        return jnp.pad(v, ((0, 0), (0, LANES - v.shape[1])))[:, None, :]

    expand = np.zeros((LANES, SSM_INNER), np.float32)
    for h in range(SSM_HEADS):
        expand[h, h * SSM_HEAD_DIM:(h + 1) * SSM_HEAD_DIM] = 1.0
    return dict(
        norm_w=norm_w[:, :, None, :],
        wg=ffn_w_gate.astype(BF16), wu=ffn_w_up.astype(BF16), wd=ffn_w_down.astype(BF16),
        w_in=ab_w_in.astype(BF16),
        w_dt=jnp.pad(ab_w_in[:, :, MAIN_PROJ:], ((0, 0), (0, 0), (0, LANES - SSM_HEADS))).astype(BF16),
        w_out=ab_w_out.astype(BF16),
        conv_w=ab_conv_w, conv_b=ab_conv_b[:, None, :],
        dt_bias=lane_pad(ab_dt_bias), a_log=lane_pad(ab_a_log),
        d_skip=jnp.repeat(ab_d_skip, SSM_HEAD_DIM, axis=1)[:, None, :],
        ssm_norm_w=ab_ssm_norm_w[:, None, :],
        lam=jnp.stack([ab_lambda_q1, ab_lambda_k1, ab_lambda_q2, ab_lambda_k2], axis=1),
        subln_w=ab_subln_w[:, None, :],
        pool_w=pool_w.astype(BF16), pool_scale=pool_scale[:, None, :],
        final_norm_w=final_norm_w[None, :],
        expand=jnp.asarray(expand),
    )


def _trunk(x, batch, seq, pos0, conv_init, ssm_init, pool_init, p, paged=None):
    depth = p['wg'].shape[0]
    m = x.shape[0]
    prompt = paged is None
    new_k, new_v, new_conv, new_ssm, new_pool = [], [], [], [], []
    cos, sin = _rope_tables(seq if prompt else m, pos0, seq)
    ffn_w = (p['wg'], p['wu'], p['wd'])
    mix = None
    for layer in range(depth):
        nw = p['norm_w'][layer]
        i = layer // 2
        x = _ffn(x, nw[0], *ffn_w, (layer, 0), mix=mix)
        mix = None
        if layer % 2 == 0:
            lambda_init = 0.8 - 0.6 * math.exp(-0.3 * layer)
            lam, sub_w = p['lam'][i], p['subln_w'][i]
            proj = _inproj(x, nw[1], p['w_in'], i, p['w_dt'][i], cos, sin, prompt_layout=prompt)
            if prompt:
                qt, kt, kb, v4, vt, z, xbc, dt_raw = proj
                attn = _attn_prompt(qt, kb, vt, lam, sub_w, lambda_init)
                k = jnp.transpose(kt.reshape(batch, DIFF_HEADS, 2, DIFF_HEAD_DIM, seq), (0, 4, 1, 2, 3))
                v = v4
            else:
                q, k, v, z, xbc, dt_raw = proj
                attn = _attn_sample(q, k, v, *paged, i, lam, sub_w, lambda_init)
            ssm_p = dict(conv_w=p['conv_w'][i], conv_b=p['conv_b'][i], dt_bias=p['dt_bias'][i], a_log=p['a_log'][i],
                         d_skip=p['d_skip'][i], ssm_norm_w=p['ssm_norm_w'][i], expand=p['expand'])
            y, cb, hs = _ssd(xbc, z, dt_raw, ssm_p, conv_init[i], ssm_init[i], batch, seq, BF16 if prompt else F32)
            new_k.append(k.reshape(batch, seq, DIFF_HEADS, 2, DIFF_HEAD_DIM))
            new_v.append(v.reshape(batch, seq, DIFF_HEADS, HEAD_LANES))
            new_conv.append(cb)
            new_ssm.append(hs)
            mix = (attn, y, p['w_out'], i)
        else:
            x, pb = _pool(x, nw[1], pool_init[i], p['pool_w'], i, p['pool_scale'][i], batch, seq, pos0)
            new_pool.append(pb)
        final_w = p['final_norm_w'] if layer == depth - 1 else None
        x = _ffn(x, nw[2], *ffn_w, (layer, 1), mix=mix, final_w=final_w)
        mix = None
    return x, jnp.stack(new_k), jnp.stack(new_v), jnp.stack(new_conv), jnp.stack(new_ssm), jnp.stack(new_pool)


def kernel(x_prompt, x_sample, cache_k, cache_v, state_conv, state_ssm, state_pool, page_table, norm_w, ffn_w_gate, ffn_w_up, ffn_w_down, ab_w_in, ab_w_out, ab_conv_w, ab_conv_b, ab_dt_bias, ab_a_log, ab_d_skip, ab_ssm_norm_w, ab_lambda_q1, ab_lambda_k1, ab_lambda_q2, ab_lambda_k2, ab_subln_w, pool_w, pool_scale, final_norm_w):
    p = _prep_params(norm_w, ffn_w_gate, ffn_w_up, ffn_w_down, ab_w_in, ab_w_out, ab_conv_w, ab_conv_b, ab_dt_bias,
                     ab_a_log, ab_d_skip, ab_ssm_norm_w, ab_lambda_q1, ab_lambda_k1, ab_lambda_q2, ab_lambda_k2,
                     ab_subln_w, pool_w, pool_scale, final_norm_w)
    n_ab, n_pool = state_conv.shape[0], state_pool.shape[0]
    bp, lp, d = x_prompt.shape
    bs, ls, _ = x_sample.shape

    zeros = lambda shape: jnp.zeros(shape, F32)
    yp, kp, vp, cp, sp, pp = _trunk(
        x_prompt.reshape(bp * lp, d), bp, lp, 0,
        zeros((n_ab, bp, CONV_WIDTH - 1, CONV_CH)), zeros((n_ab, bp, SSM_HEADS, SSM_HEAD_DIM, SSM_STATE)),
        zeros((n_pool, bp, POOL_STATE_LEN, d)), p)

    past_len = page_table.shape[1] * cache_k.shape[2]
    ys, ks, vs, cs, ss, ps = _trunk(
        x_sample.reshape(bs * ls, d), bs, ls, past_len, state_conv, state_ssm, state_pool, p,
        paged=(cache_k, cache_v, page_table))
    return (yp.reshape(bp, lp, d), ys.reshape(bs, ls, d), kp, vp, cp, sp, pp, ks, vs, cs, ss, ps)
```

```python
import functools
import math

import numpy as np
import jax
import jax.numpy as jnp
from jax import lax
from jax.experimental import pallas as pl
from jax.experimental.pallas import tpu as pltpu

F32 = jnp.float32
BF16 = jnp.bfloat16

RMS_EPS = 1e-6
ROPE_THETA = 10000.0
LOG2_E = 1.4426950408889634

DIFF_HEADS = 4
DIFF_HEAD_DIM = 64
HEAD_LANES = 2 * DIFF_HEAD_DIM
QK_WIDTH = DIFF_HEADS * HEAD_LANES
SSM_INNER = 512
SSM_HEAD_DIM = 64
SSM_HEADS = 8
SSM_GROUPS = 2
SSM_STATE = 128
GROUP_LANES = SSM_INNER // SSM_GROUPS
CONV_WIDTH = 4
CONV_CH = SSM_INNER + 2 * SSM_GROUPS * SSM_STATE
SSD_CHUNK = 128
SSD_CHUNKS_PER_STEP = 4
POOL_WINDOWS = (2, 4, 8, 16)
POOL_GROUP_CH = 256
POOL_HALO = 16
POOL_STATE_LEN = 15
MAIN_PROJ = 2 * QK_WIDTH + QK_WIDTH + SSM_INNER + CONV_CH

LANES = 128
SUBLANES = 8
MXU_DIM = 256
VMEM_LIMIT_BIG = 58 * 1024 * 1024
VMEM_LIMIT_SMALL = 40 * 1024 * 1024

FFN_CHUNK = MXU_DIM
TOKEN_TILE = 512
FFN_TOKEN_TILE = 1024
ATTN_Q_TILE = 512
ATTN_K_TILE = 256
ATTN_SUM_ROWS = 16
ATTN_LOOKAHEAD = 4
PAGES_PER_STEP = 32


def _cparams(sem, vmem):
    return pltpu.CompilerParams(dimension_semantics=sem, vmem_limit_bytes=vmem)


def _tile(m, pref):
    t = min(m, pref)
    while m % t:
        t //= 2
    return t


def _rms(x, w):
    ms = jnp.mean(x * x, axis=-1, keepdims=True)
    return x * lax.rsqrt(ms + RMS_EPS) * w


def _silu(x):
    return x * jax.nn.sigmoid(x)


def _dot(a, b):
    return jnp.dot(a, b, preferred_element_type=F32)


def _dot_nt(a, b):
    return lax.dot_general(a, b, (((1,), (1,)), ((), ())), preferred_element_type=F32)


def _split3(a):
    hi = a.astype(BF16)
    r = a - hi.astype(F32)
    mid = r.astype(BF16)
    return hi, mid, (r - mid.astype(F32)).astype(BF16)


def _dot_exact(a, b, exact_lhs=False):
    if exact_lhs:
        a = a.astype(BF16)
        return sum(_dot(a, t) for t in _split3(b))
    b = b.astype(BF16)
    return sum(_dot(t, b) for t in _split3(a))


def _row_spec(tile, width):
    return pl.BlockSpec((tile, width), lambda i: (i, 0))


def _const_spec(shape):
    return pl.BlockSpec(shape, lambda *_: (0,) * len(shape))


def _rope_table_kernel(inv_ref, cos_ref, sin_ref, *, pos0, period):
    shape = cos_ref.shape
    row = lax.broadcasted_iota(jnp.int32, shape, 0)
    lane = lax.broadcasted_iota(jnp.int32, shape, 1)
    pos = pos0 + lax.rem(row, period)
    ang = pos.astype(F32) * inv_ref[...]
    cos_ref[...] = jnp.cos(ang)
    s = jnp.sin(ang)
    sin_ref[...] = jnp.where(lax.rem(lane, DIFF_HEAD_DIM) < DIFF_HEAD_DIM // 2, -s, s)


def _rope_tables(rows, pos0, period):
    half = DIFF_HEAD_DIM // 2
    inv = 1.0 / (ROPE_THETA ** (jnp.arange(0, DIFF_HEAD_DIM, 2, dtype=F32) / DIFF_HEAD_DIM))
    inv = jnp.tile(inv, LANES // half)[None, :]
    return pl.pallas_call(
        functools.partial(_rope_table_kernel, pos0=pos0, period=period),
        out_shape=(jax.ShapeDtypeStruct((rows, LANES), F32),) * 2,
        name="rope_table",
    )(inv)


def _ffn_kernel(*refs, n_chunks, has_mix, has_final):
    it = iter(refs)
    x_ref = next(it)
    if has_mix:
        a_ref, y_ref, wo_ref = next(it), next(it), next(it)
    nw_ref, wg_ref, wu_ref, wd_ref = next(it), next(it), next(it), next(it)
    fw_ref = next(it) if has_final else None
    o_ref = next(it)

    x = x_ref[...]
    if has_mix:
        half = a_ref.shape[1]
        x = x + _dot(a_ref[...].astype(BF16), wo_ref[:half, :]) + _dot(y_ref[...].astype(BF16), wo_ref[half:, :])
    u = _rms(x, nw_ref[...]).astype(BF16)
    acc = jnp.zeros(x.shape, F32)
    for c in range(n_chunks):
        sl = slice(c * FFN_CHUNK, (c + 1) * FFN_CHUNK)
        g = _dot(u, wg_ref[:, sl])
        up = _dot(u, wu_ref[:, sl])
        h = (_silu(g) * up).astype(BF16)
        acc = acc + _dot(h, wd_ref[sl, :])
    y = x + 0.5 * acc
    if has_final:
        y = _rms(y, fw_ref[...])
    o_ref[...] = y


def _stacked_spec(w, idx):
    n_lead = len(idx)
    return pl.BlockSpec((None,) * n_lead + w.shape[n_lead:], lambda *_: tuple(idx) + (0,) * (w.ndim - n_lead),
                        pipeline_mode=pl.Buffered(1))


def _ffn(x, nw, wg, wu, wd, w_idx, mix=None, final_w=None):
    m, d = x.shape
    hidden = wg.shape[-1]
    tm = _tile(m, FFN_TOKEN_TILE)
    args, specs = [x], [_row_spec(tm, d)]
    if mix is not None:
        a, y, wo, wo_idx = mix
        args += [a, y, wo]
        specs += [_row_spec(tm, a.shape[1]), _row_spec(tm, y.shape[1]), _stacked_spec(wo, (wo_idx,))]
    args += [nw, wg, wu, wd]
    specs += [_const_spec(nw.shape), _stacked_spec(wg, w_idx), _stacked_spec(wu, w_idx), _stacked_spec(wd, w_idx)]
    if final_w is not None:
        args.append(final_w)
        specs.append(_const_spec(final_w.shape))
    return pl.pallas_call(
        functools.partial(_ffn_kernel, n_chunks=hidden // FFN_CHUNK, has_mix=mix is not None,
                          has_final=final_w is not None),
        grid=(m // tm,),
        in_specs=specs,
        out_specs=_row_spec(tm, d),
        out_shape=jax.ShapeDtypeStruct((m, d), F32),
        compiler_params=_cparams(("parallel",), VMEM_LIMIT_BIG),
        name="ffn",
    )(*args)


def _inproj_kernel(x_ref, nw_ref, w_ref, wdt_ref, cos_ref, sin_ref, *out_refs, prompt_layout):
    if prompt_layout:
        qt_ref, kt_ref, kb_ref, v4_ref, vt_ref, z_ref, xbc_ref, dt_ref = out_refs
    else:
        q_ref, k_ref, v_ref, z_ref, xbc_ref, dt_ref = out_refs
    u = _rms(x_ref[...], nw_ref[...]).astype(BF16)
    tm = u.shape[0]
    reps = QK_WIDTH // LANES
    cos = jnp.concatenate([cos_ref[...]] * reps, axis=1)
    sin = jnp.concatenate([sin_ref[...]] * reps, axis=1)
    lane = lax.broadcasted_iota(jnp.int32, (tm, QK_WIDTH), 1)
    low_half = lax.rem(lane, DIFF_HEAD_DIM) < DIFF_HEAD_DIM // 2
    half = DIFF_HEAD_DIM // 2

    def rope(t):
        up = pltpu.roll(t, QK_WIDTH - half, 1)
        down = pltpu.roll(t, half, 1)
        return t * cos + jnp.where(low_half, up, down) * sin

    q_scale = DIFF_HEAD_DIM ** -0.5 * (LOG2_E if prompt_layout else 1.0)
    q = rope(_dot(u, w_ref[:, 0:QK_WIDTH])) * q_scale
    k = rope(_dot(u, w_ref[:, QK_WIDTH:2 * QK_WIDTH]))
    v = _dot(u, w_ref[:, 2 * QK_WIDTH:3 * QK_WIDTH])
    if prompt_layout:
        def head_blocks(t_ref, t):
            tb = t_ref.shape[-1]
            for h in range(DIFF_HEADS):
                for jb in range(tm // tb):
                    t_ref[h, jb] = t[h * HEAD_LANES:(h + 1) * HEAD_LANES, jb * tb:(jb + 1) * tb].astype(BF16)

        head_blocks(qt_ref, q.T)
        kt_ref[...] = k.T
        kb_ref[...] = k.astype(BF16)
        for h in range(DIFF_HEADS):
            v4_ref[pl.ds(h, tm, stride=DIFF_HEADS), :] = v[:, h * HEAD_LANES:(h + 1) * HEAD_LANES]
        head_blocks(vt_ref, v.T)
    else:
        q_ref[...] = q
        k_ref[...] = k
        v_ref[...] = v
    z0 = 3 * QK_WIDTH
    z_ref[...] = _dot(u, w_ref[:, z0:z0 + SSM_INNER])
    xbc_ref[...] = _dot(u, w_ref[:, z0 + SSM_INNER:z0 + SSM_INNER + CONV_CH])
    dt_ref[...] = _dot(u, wdt_ref[...])


def _inproj(x, nw, w_in, w_idx, w_dt, cos, sin, prompt_layout):
    m, d = x.shape
    period = cos.shape[0]
    tm = _tile(period, TOKEN_TILE)
    n_per = period // tm
    tab_spec = pl.BlockSpec((tm, LANES), lambda i: (i % n_per, 0))
    rows = lambda w, dt: (_row_spec(tm, w), jax.ShapeDtypeStruct((m, w), dt))
    tail = [rows(SSM_INNER, F32), rows(CONV_CH, F32), rows(LANES, F32)]
    if prompt_layout:
        n_seq = m // period

        def blocked(tile):
            ta = _tile(tm, tile)
            return (pl.BlockSpec((None, DIFF_HEADS, tm // ta, HEAD_LANES, ta), lambda i: (i // n_per, 0, i % n_per, 0, 0)),
                    jax.ShapeDtypeStruct((n_seq, DIFF_HEADS, period // ta, HEAD_LANES, ta), BF16))

        kt_out = (pl.BlockSpec((None, QK_WIDTH, tm), lambda i: (i // n_per, 0, i % n_per)),
                  jax.ShapeDtypeStruct((n_seq, QK_WIDTH, period), F32))
        v4_out = (_row_spec(tm * DIFF_HEADS, HEAD_LANES), jax.ShapeDtypeStruct((m * DIFF_HEADS, HEAD_LANES), F32))
        outs = [blocked(ATTN_Q_TILE), kt_out, rows(QK_WIDTH, BF16), v4_out, blocked(ATTN_K_TILE)] + tail
    else:
        outs = [rows(QK_WIDTH, F32)] * 3 + tail
    return pl.pallas_call(
        functools.partial(_inproj_kernel, prompt_layout=prompt_layout),
        grid=(m // tm,),
        in_specs=[_row_spec(tm, d), _const_spec(nw.shape), pl.BlockSpec((None,) + w_in.shape[1:], lambda i: (w_idx, 0, 0)),
                  _const_spec(w_dt.shape), tab_spec, tab_spec],
        out_specs=[spec for spec, _ in outs],
        out_shape=[shape for _, shape in outs],
        compiler_params=_cparams(("parallel",), VMEM_LIMIT_SMALL),
        name="inproj",
    )(x, nw, w_in, w_dt, cos, sin)


def _lambda(lam_ref, lambda_init):
    v = lam_ref[...]
    d1 = jnp.sum(v[0:1] * v[1:2], axis=1, keepdims=True)
    d2 = jnp.sum(v[2:3] * v[3:4], axis=1, keepdims=True)
    return jnp.exp(d1) - jnp.exp(d2) + lambda_init


def _subln(d, w, lambda_init):
    return _rms(d, w) * (1.0 - lambda_init)


def _attn_prompt_kernel(lam_ref, sub_ref, qt_ref, k_ref, vt_ref, o_ref, m_s, acc_s, *, lambda_init):
    n_qb, _, tqb = qt_ref.shape
    n_kb, _, tk = vt_ref.shape
    strip = tk
    n_strips = n_qb * tqb // strip
    sub = lax.broadcasted_iota(jnp.int32, (HEAD_LANES, strip), 0)
    first_comp = sub < DIFF_HEAD_DIM
    krow = lax.broadcasted_iota(jnp.int32, (tk, strip), 0)
    qcol = lax.broadcasted_iota(jnp.int32, (tk, strip), 1)
    causal = krow <= qcol
    zero = jnp.zeros((HEAD_LANES, strip), BF16)

    def cols(qs, comp):
        return slice((comp * n_strips + qs) * strip, (comp * n_strips + qs + 1) * strip)

    units = [(kb, qs, comp) for kb in range(n_kb) for qs in range(kb, n_strips) for comp in range(2)]

    def scores(unit):
        kb, qs, comp = unit
        qb, off = divmod(qs * strip, tqb)
        qt = qt_ref[qb, :, off:off + strip]
        q = jnp.where(first_comp, qt, zero) if comp == 0 else jnp.where(first_comp, zero, qt)
        return _dot(k_ref[kb * tk:(kb + 1) * tk, :], q)

    pending = [scores(u) for u in units[:ATTN_LOOKAHEAD]]
    ones = jnp.ones((ATTN_SUM_ROWS, tk), BF16)
    for i, (kb, qs, comp) in enumerate(units):
        if i + ATTN_LOOKAHEAD < len(units):
            pending.append(scores(units[i + ATTN_LOOKAHEAD]))
        s = pending.pop(0)
        c = cols(qs, comp)
        if comp == 0 and qs == kb:
            vt = jnp.concatenate([vt_ref[kb], ones], axis=0)
        if kb == qs:
            s = jnp.where(causal, s, -jnp.inf)
        if kb == 0:
            m = jnp.max(s, axis=0, keepdims=True)
            acc_s[:, c] = _dot(vt, jnp.exp2(s - m).astype(BF16))
        else:
            m_old = m_s[:, c]
            m = jnp.maximum(m_old, jnp.max(s, axis=0, keepdims=True))
            acc_s[:, c] = jnp.exp2(m_old - m) * acc_s[:, c] + _dot(vt, jnp.exp2(s - m).astype(BF16))
        m_s[:, c] = m
    lam = _lambda(lam_ref, lambda_init)
    for qs in range(n_strips):
        o0, o1 = (acc_s[:HEAD_LANES, cols(qs, comp)] * (1.0 / acc_s[HEAD_LANES:HEAD_LANES + 1, cols(qs, comp)])
                  for comp in range(2))
        d = (o0 - lam * o1).T
        o_ref[qs * strip:(qs + 1) * strip, :] = _subln(d, sub_ref[...], lambda_init).astype(o_ref.dtype)


def _attn_prompt(qt, kb, vt, lam_vecs, sub_w, lambda_init):
    batch, _, nq, _, tq = qt.shape
    nk, tk = vt.shape[2], vt.shape[4]
    seq = nq * tq
    q_spec = pl.BlockSpec((None, None, nq, HEAD_LANES, tq), lambda b, h: (b, h, 0, 0, 0))
    k_spec = pl.BlockSpec((seq, HEAD_LANES), lambda b, h: (b, h))
    v_spec = pl.BlockSpec((None, None, nk, HEAD_LANES, tk), lambda b, h: (b, h, 0, 0, 0))
    return pl.pallas_call(
        functools.partial(_attn_prompt_kernel, lambda_init=lambda_init),
        grid=(batch, DIFF_HEADS),
        in_specs=[pl.BlockSpec(lam_vecs.shape, lambda b, h: (0, 0)), pl.BlockSpec(sub_w.shape, lambda b, h: (0, 0)),
                  q_spec, k_spec, v_spec],
        out_specs=k_spec,
        out_shape=jax.ShapeDtypeStruct(kb.shape, BF16),
        scratch_shapes=[pltpu.VMEM((1, 2 * seq), F32), pltpu.VMEM((HEAD_LANES + ATTN_SUM_ROWS, 2 * seq), F32)],
        compiler_params=_cparams(("parallel", "parallel"), VMEM_LIMIT_SMALL),
        name="attn_prompt",
    )(lam_vecs, sub_w, qt, kb, vt)


def _attn_sample_kernel(pt_ref, lam_ref, sub_ref, q_ref, kn_ref, vn_ref, *rest, n_pages, n_steps, lambda_init):
    del pt_ref
    k_refs, v_refs = rest[:n_pages], rest[n_pages:2 * n_pages]
    o_ref = rest[2 * n_pages]
    wq_s, m_s, l_s, acc_s = rest[2 * n_pages + 1:]
    step = pl.program_id(1)
    t_new = q_ref.shape[0]
    page = k_refs[0].shape[1]
    head_rows = 2 * t_new

    @pl.when(step == 0)
    def _():
        qt = jnp.concatenate([q_ref[...]] * (2 * DIFF_HEADS), axis=0)
        row = lax.broadcasted_iota(jnp.int32, qt.shape, 0)
        col = lax.broadcasted_iota(jnp.int32, qt.shape, 1)
        same = lax.div(row, t_new) == lax.div(col, DIFF_HEAD_DIM)
        wq_s[...] = jnp.where(same, qt, 0.0).astype(BF16)
        m_s[...] = jnp.full(m_s.shape, -jnp.inf, F32)
        l_s[...] = jnp.zeros(l_s.shape, F32)
        acc_s[...] = jnp.zeros(acc_s.shape, F32)

    def update(scores, values):
        m = m_s[...]
        m_new = m
        for s in scores:
            m_new = jnp.maximum(m_new, jnp.max(s, axis=1, keepdims=True))
        alpha = jnp.exp(m - m_new)
        l = alpha * l_s[...]
        acc = alpha * acc_s[...]
        for s, v_heads in zip(scores, values):
            p = jnp.exp(s - m_new)
            l = l + jnp.sum(p, axis=1, keepdims=True)
            pb = p.astype(BF16)
            acc = acc + jnp.concatenate(
                [_dot(pb[h * head_rows:(h + 1) * head_rows], v_heads[h]) for h in range(DIFF_HEADS)], axis=0)
        m_s[...] = m_new
        l_s[...] = l
        acc_s[...] = acc

    wq = wq_s[...]
    update([_dot(wq, k[...].astype(BF16)) for k in k_refs],
           [[v[pl.ds(h, page, stride=DIFF_HEADS), :].astype(BF16) for h in range(DIFF_HEADS)] for v in v_refs])

    @pl.when(step == n_steps - 1)
    def _():
        pad = jnp.zeros((page - t_new, QK_WIDTH), F32)
        kn = jnp.concatenate([kn_ref[...], pad], axis=0).astype(BF16)
        vn = jnp.concatenate([vn_ref[...], pad], axis=0).astype(BF16)
        s = _dot_nt(wq, kn)
        row = lax.broadcasted_iota(jnp.int32, s.shape, 0)
        col = lax.broadcasted_iota(jnp.int32, s.shape, 1)
        s = jnp.where(col <= lax.rem(row, t_new), s, -jnp.inf)
        update([s], [[vn[:, h * HEAD_LANES:(h + 1) * HEAD_LANES] for h in range(DIFF_HEADS)]])
        o = acc_s[...] / l_s[...]
        lam = _lambda(lam_ref, lambda_init)
        for h in range(DIFF_HEADS):
            r0 = h * head_rows
            d = o[r0:r0 + t_new] - lam * o[r0 + t_new:r0 + head_rows]
            o_ref[:, h * HEAD_LANES:(h + 1) * HEAD_LANES] = _subln(d, sub_ref[...], lambda_init)


def _attn_sample(q, k_new, v_new, cache_k, cache_v, page_table, layer, lam_vecs, sub_w, lambda_init):
    batch, n_log = page_table.shape
    m = q.shape[0]
    t_new = m // batch
    n_layers, n_phys, page = cache_k.shape[:3]
    ck = jnp.transpose(cache_k, (0, 1, 3, 4, 5, 2)).reshape(n_layers, n_phys, QK_WIDTH, page)
    cv = cache_v.reshape(n_layers, n_phys, page * DIFF_HEADS, HEAD_LANES)
    g = _tile(n_log, PAGES_PER_STEP)
    n_steps = n_log // g
    n_rows = 2 * DIFF_HEADS * t_new

    def page_spec(i, shape):
        return pl.BlockSpec((None, None) + shape, lambda b, s, pt: (layer, pt[b, s * g + i], 0, 0))

    tok_spec = pl.BlockSpec((t_new, QK_WIDTH), lambda b, s, pt: (b, 0))
    const2 = lambda shape: pl.BlockSpec(shape, lambda b, s, pt: (0, 0))
    grid_spec = pltpu.PrefetchScalarGridSpec(
        num_scalar_prefetch=1,
        grid=(batch, n_steps),
        in_specs=[const2(lam_vecs.shape), const2(sub_w.shape), tok_spec, tok_spec, tok_spec]
                 + [page_spec(i, (QK_WIDTH, page)) for i in range(g)]
                 + [page_spec(i, (page * DIFF_HEADS, HEAD_LANES)) for i in range(g)],
        out_specs=tok_spec,
        scratch_shapes=[pltpu.VMEM((n_rows, QK_WIDTH), BF16), pltpu.VMEM((n_rows, 1), F32),
                        pltpu.VMEM((n_rows, 1), F32), pltpu.VMEM((n_rows, HEAD_LANES), F32)],
    )
    return pl.pallas_call(
        functools.partial(_attn_sample_kernel, n_pages=g, n_steps=n_steps, lambda_init=lambda_init),
        grid_spec=grid_spec,
        out_shape=jax.ShapeDtypeStruct((m, QK_WIDTH), F32),
        compiler_params=_cparams(("parallel", "arbitrary"), VMEM_LIMIT_BIG),
        name="attn_sample",
    )(page_table, lam_vecs, sub_w, q, k_new, v_new, *([ck] * g), *([cv] * g))


def _ssd_kernel(xbc_ref, z_ref, dt_ref, cw_ref, cb_ref, dtb_ref, alog_ref, dsk_ref, nw_ref, exp_ref,
                cinit_ref, h0_ref, y_ref, cout_ref, hout_ref, ext_s, st_s, *, t_in, n_chunks):
    c = pl.program_id(1)
    T = SSD_CHUNK
    tail = CONV_WIDTH - 1

    @pl.when(c == 0)
    def _():
        ext_s[0:SUBLANES, :] = jnp.zeros((SUBLANES, CONV_CH), F32)
        ext_s[SUBLANES - tail:SUBLANES, :] = cinit_ref[...]
        st_s[...] = h0_ref[...].reshape(SSM_INNER, SSM_STATE).T

    u = xbc_ref[...]
    ext_s[SUBLANES:SUBLANES + t_in, :] = u
    cw = cw_ref[...]
    conv = cb_ref[...] + cw[tail:tail + 1] * u
    for j in range(tail):
        conv = conv + cw[j:j + 1] * ext_s[SUBLANES - tail + j:SUBLANES - tail + j + t_in, :]
    ext_s[0:SUBLANES, :] = ext_s[t_in:t_in + SUBLANES, :]
    xc_all = _silu(conv)
    dt_all = jax.nn.softplus(dt_ref[...] + dtb_ref[...])
    z_all = z_ref[...]
    if t_in < T:
        xc_all = jnp.concatenate([xc_all, jnp.zeros((T - t_in, CONV_CH), F32)], axis=0)
        dt_all = jnp.concatenate([dt_all, jnp.zeros((T - t_in, LANES), F32)], axis=0)
        z_all = jnp.concatenate([z_all, jnp.zeros((T - t_in, SSM_INNER), F32)], axis=0)

    a = -jnp.exp(alog_ref[...])
    row = lax.broadcasted_iota(jnp.int32, (T, T), 0)
    col = lax.broadcasted_iota(jnp.int32, (T, T), 1)
    causal = row >= col
    tri = causal.astype(F32)
    expand = exp_ref[...]
    lane = lax.broadcasted_iota(jnp.int32, (T, LANES), 1)
    heads_per_group = SSM_HEADS // SSM_GROUPS
    nw = nw_ref[...]
    for ci in range(max(1, t_in // T)):
        rows = slice(ci * T, (ci + 1) * T)
        xc, dt, z = xc_all[rows], dt_all[rows], z_all[rows]
        acs = _dot_exact(tri, dt * a, exact_lhs=True)
        acs_t = acs.T
        both_x = _dot_exact(jnp.concatenate([dt, acs], axis=0), expand)
        dt_x, acs_x = both_x[:T], both_x[T:]
        last = acs_x[T - 1:T, :]
        decay_out = jnp.exp(acs_x)
        decay_end = jnp.exp(last - acs_x)
        chunk_decay = jnp.exp(last)

        xs = xc[:, :SSM_INNER]
        xdt = xs * dt_x
        xdt_b = xdt.astype(BF16)
        xw_b = (xdt * decay_end).astype(BF16)
        state = st_s[...]
        state_b = state.astype(BF16)
        ys = []
        for g in range(SSM_GROUPS):
            b0 = SSM_INNER + g * SSM_STATE
            c0 = SSM_INNER + SSM_GROUPS * SSM_STATE + g * SSM_STATE
            bg_t = xc[:, b0:b0 + SSM_STATE].T.astype(BF16)
            cg = xc[:, c0:c0 + SSM_STATE].astype(BF16)
            gl = slice(g * GROUP_LANES, (g + 1) * GROUP_LANES)
            cb = _dot(cg, bg_t)
            y_off = _dot(cg, state_b[:, gl]) * decay_out[:, gl]
            st_s[:, gl] = state[:, gl] * chunk_decay[:, gl] + _dot(bg_t, xw_b[:, gl])
            for pair in range(heads_per_group // 2):
                h0 = g * heads_per_group + 2 * pair
                xp = xdt_b[:, h0 * SSM_HEAD_DIM:(h0 + 2) * SSM_HEAD_DIM]
                outs = []
                for hh in (h0, h0 + 1):
                    seg = acs[:, hh:hh + 1] - acs_t[hh:hh + 1, :]
                    w = cb * jnp.exp(jnp.where(causal, seg, -jnp.inf))
                    outs.append(_dot(w.astype(BF16), xp))
                ys.append(jnp.where(lane < SSM_HEAD_DIM, outs[0], outs[1])
                          + y_off[:, 2 * pair * SSM_HEAD_DIM:(2 * pair + 2) * SSM_HEAD_DIM])
        y = jnp.concatenate(ys, axis=1) + dsk_ref[...] * xs
        y = y * _silu(z)
        parts = []
        for g in range(SSM_GROUPS):
            gl = slice(g * GROUP_LANES, (g + 1) * GROUP_LANES)
            parts.append(_rms(y[:, gl], nw[:, gl]))
        y = jnp.concatenate(parts, axis=1).astype(y_ref.dtype)
        if t_in < T:
            y_ref[...] = y[:t_in]
        else:
            y_ref[rows, :] = y

    @pl.when(c == n_chunks - 1)
    def _():
        cout_ref[...] = ext_s[SUBLANES - tail:SUBLANES, :]
        hout_ref[...] = st_s[...].T.reshape(SSM_HEADS, SSM_HEAD_DIM, SSM_STATE)


def _ssd(xbc, z, dt_raw, p, conv_init, h0, batch, seq, y_dtype):
    t_in = _tile(seq, SSD_CHUNK * SSD_CHUNKS_PER_STEP)
    n_chunks = seq // t_in
    assert t_in % SSD_CHUNK == 0 or n_chunks == 1
    tail = CONV_WIDTH - 1
    rows = lambda w: pl.BlockSpec((t_in, w), lambda b, c: (b * n_chunks + c, 0))
    per_b3 = lambda s: pl.BlockSpec((None,) + s, lambda b, c: (b,) + (0,) * len(s))
    consts = [p['conv_w'], p['conv_b'], p['dt_bias'], p['a_log'], p['d_skip'], p['ssm_norm_w'], p['expand']]
    return pl.pallas_call(
        functools.partial(_ssd_kernel, t_in=t_in, n_chunks=n_chunks),
        grid=(batch, n_chunks),
        in_specs=[rows(CONV_CH), rows(SSM_INNER), rows(LANES)]
                 + [pl.BlockSpec(a.shape, lambda b, c: (0, 0)) for a in consts]
                 + [per_b3((tail, CONV_CH)), per_b3((SSM_HEADS, SSM_HEAD_DIM, SSM_STATE))],
        out_specs=[rows(SSM_INNER), per_b3((tail, CONV_CH)), per_b3((SSM_HEADS, SSM_HEAD_DIM, SSM_STATE))],
        out_shape=[jax.ShapeDtypeStruct((batch * seq, SSM_INNER), y_dtype),
                   jax.ShapeDtypeStruct((batch, tail, CONV_CH), F32),
                   jax.ShapeDtypeStruct((batch, SSM_HEADS, SSM_HEAD_DIM, SSM_STATE), F32)],
        scratch_shapes=[pltpu.VMEM((SUBLANES + t_in, CONV_CH), F32), pltpu.VMEM((SSM_STATE, SSM_INNER), F32)],
        compiler_params=_cparams(("parallel", "arbitrary"), VMEM_LIMIT_SMALL),
        name="ssd",
    )(xbc, z, dt_raw, *consts, conv_init, h0)


def _pool_kernel(x_ref, nw_ref, init_ref, pw_ref, ps_ref, o_ref, pout_ref, ext_s, *, tp, n_tiles, pos0):
    t = pl.program_id(1)
    halo = POOL_HALO

    @pl.when(t == 0)
    def _():
        ext_s[0:halo, :] = jnp.zeros((halo, ext_s.shape[1]), F32)
        ext_s[halo - POOL_STATE_LEN:halo, :] = init_ref[...]

    x = x_ref[...]
    u = _rms(x, nw_ref[...])
    ext_s[halo:halo + tp, :] = u
    e = ext_s[...]
    gc = POOL_GROUP_CH
    sums = [e]
    for lvl in range(len(POOL_WINDOWS)):
        prev = sums[-1][:, gc:] if lvl else sums[-1]
        sums.append(prev + pltpu.roll(prev, 1 << lvl, 0))
    pos = pos0 + t * tp + lax.broadcasted_iota(jnp.int32, (tp, 1), 0)
    mixed = []
    for g, w in enumerate(POOL_WINDOWS):
        cnt = jnp.minimum(w, pos + 1).astype(F32)
        win = sums[g + 1][halo:, :gc]
        pooled = win / cnt - u[:, g * gc:(g + 1) * gc]
        mixed.append(_dot(pooled.astype(BF16), pw_ref[g]))
    o_ref[...] = x + jnp.concatenate(mixed, axis=1) * ps_ref[...]

    @pl.when(t == n_tiles - 1)
    def _():
        pout_ref[...] = ext_s[tp + halo - POOL_STATE_LEN:tp + halo, :]

    ext_s[0:halo, :] = ext_s[tp:tp + halo, :]


def _pool(x, nw, pool_init, pool_w, w_idx, pool_scale, batch, seq, pos0):
    d = x.shape[1]
    tp = _tile(seq, TOKEN_TILE)
    n_tiles = seq // tp
    rows = pl.BlockSpec((tp, d), lambda b, t: (b * n_tiles + t, 0))
    state = pl.BlockSpec((None, POOL_STATE_LEN, d), lambda b, t: (b, 0, 0))
    return pl.pallas_call(
        functools.partial(_pool_kernel, tp=tp, n_tiles=n_tiles, pos0=pos0),
        grid=(batch, n_tiles),
        in_specs=[rows, pl.BlockSpec(nw.shape, lambda b, t: (0, 0)), state,
                  _stacked_spec(pool_w, (w_idx,)), pl.BlockSpec(pool_scale.shape, lambda b, t: (0, 0))],
        out_specs=[rows, state],
        out_shape=[jax.ShapeDtypeStruct(x.shape, F32), jax.ShapeDtypeStruct((batch, POOL_STATE_LEN, d), F32)],
        scratch_shapes=[pltpu.VMEM((POOL_HALO + tp, d), F32)],
        compiler_params=_cparams(("parallel", "arbitrary"), VMEM_LIMIT_SMALL),
        name="pool",
    )(x, nw, pool_init, pool_w, pool_scale)


def _prep_params(norm_w, ffn_w_gate, ffn_w_up, ffn_w_down, ab_w_in, ab_w_out, ab_conv_w, ab_conv_b, ab_dt_bias,
                 ab_a_log, ab_d_skip, ab_ssm_norm_w, ab_lambda_q1, ab_lambda_k1, ab_lambda_q2, ab_lambda_k2,
                 ab_subln_w, pool_w, pool_scale, final_norm_w):
    def lane_pad(v):
        return jnp.pad(v, ((0, 0), (0, LANES - v.shape[1])))[:, None, :]

    expand = np.zeros((LANES, SSM_INNER), np.float32)
    for h in range(SSM_HEADS):
        expand[h, h * SSM_HEAD_DIM:(h + 1) * SSM_HEAD_DIM] = 1.0
    return dict(
        norm_w=norm_w[:, :, None, :],
        wg=ffn_w_gate.astype(BF16), wu=ffn_w_up.astype(BF16), wd=ffn_w_down.astype(BF16),
        w_in=ab_w_in.astype(BF16),
        w_dt=jnp.pad(ab_w_in[:, :, MAIN_PROJ:], ((0, 0), (0, 0), (0, LANES - SSM_HEADS))).astype(BF16),
        w_out=ab_w_out.astype(BF16),
        conv_w=ab_conv_w, conv_b=ab_conv_b[:, None, :],
        dt_bias=lane_pad(ab_dt_bias), a_log=lane_pad(ab_a_log),
        d_skip=jnp.repeat(ab_d_skip, SSM_HEAD_DIM, axis=1)[:, None, :],
        ssm_norm_w=ab_ssm_norm_w[:, None, :],
        lam=jnp.stack([ab_lambda_q1, ab_lambda_k1, ab_lambda_q2, ab_lambda_k2], axis=1),
        subln_w=ab_subln_w[:, None, :],
        pool_w=pool_w.astype(BF16), pool_scale=pool_scale[:, None, :],
        final_norm_w=final_norm_w[None, :],
        expand=jnp.asarray(expand),
    )


def _trunk(x, batch, seq, pos0, conv_init, ssm_init, pool_init, p, paged=None):
    depth = p['wg'].shape[0]
    m = x.shape[0]
    prompt = paged is None
    new_k, new_v, new_conv, new_ssm, new_pool = [], [], [], [], []
    cos, sin = _rope_tables(seq if prompt else m, pos0, seq)
    ffn_w = (p['wg'], p['wu'], p['wd'])
    mix = None
    for layer in range(depth):
        nw = p['norm_w'][layer]
        i = layer // 2
        x = _ffn(x, nw[0], *ffn_w, (layer, 0), mix=mix)
        mix = None
        if layer % 2 == 0:
            lambda_init = 0.8 - 0.6 * math.exp(-0.3 * layer)
            lam, sub_w = p['lam'][i], p['subln_w'][i]
            proj = _inproj(x, nw[1], p['w_in'], i, p['w_dt'][i], cos, sin, prompt_layout=prompt)
            if prompt:
                qt, kt, kb, v4, vt, z, xbc, dt_raw = proj
                attn = _attn_prompt(qt, kb, vt, lam, sub_w, lambda_init)
                k = jnp.transpose(kt.reshape(batch, DIFF_HEADS, 2, DIFF_HEAD_DIM, seq), (0, 4, 1, 2, 3))
                v = v4
            else:
                q, k, v, z, xbc, dt_raw = proj
                attn = _attn_sample(q, k, v, *paged, i, lam, sub_w, lambda_init)
            ssm_p = dict(conv_w=p['conv_w'][i], conv_b=p['conv_b'][i], dt_bias=p['dt_bias'][i], a_log=p['a_log'][i],
                         d_skip=p['d_skip'][i], ssm_norm_w=p['ssm_norm_w'][i], expand=p['expand'])
            y, cb, hs = _ssd(xbc, z, dt_raw, ssm_p, conv_init[i], ssm_init[i], batch, seq, BF16 if prompt else F32)
            new_k.append(k.reshape(batch, seq, DIFF_HEADS, 2, DIFF_HEAD_DIM))
            new_v.append(v.reshape(batch, seq, DIFF_HEADS, HEAD_LANES))
            new_conv.append(cb)
            new_ssm.append(hs)
            mix = (attn, y, p['w_out'], i)
        else:
            x, pb = _pool(x, nw[1], pool_init[i], p['pool_w'], i, p['pool_scale'][i], batch, seq, pos0)
            new_pool.append(pb)
        final_w = p['final_norm_w'] if layer == depth - 1 else None
        x = _ffn(x, nw[2], *ffn_w, (layer, 1), mix=mix, final_w=final_w)
        mix = None
    return x, jnp.stack(new_k), jnp.stack(new_v), jnp.stack(new_conv), jnp.stack(new_ssm), jnp.stack(new_pool)


def kernel(x_prompt, x_sample, cache_k, cache_v, state_conv, state_ssm, state_pool, page_table, norm_w, ffn_w_gate, ffn_w_up, ffn_w_down, ab_w_in, ab_w_out, ab_conv_w, ab_conv_b, ab_dt_bias, ab_a_log, ab_d_skip, ab_ssm_norm_w, ab_lambda_q1, ab_lambda_k1, ab_lambda_q2, ab_lambda_k2, ab_subln_w, pool_w, pool_scale, final_norm_w):
    p = _prep_params(norm_w, ffn_w_gate, ffn_w_up, ffn_w_down, ab_w_in, ab_w_out, ab_conv_w, ab_conv_b, ab_dt_bias,
                     ab_a_log, ab_d_skip, ab_ssm_norm_w, ab_lambda_q1, ab_lambda_k1, ab_lambda_q2, ab_lambda_k2,
                     ab_subln_w, pool_w, pool_scale, final_norm_w)
    n_ab, n_pool = state_conv.shape[0], state_pool.shape[0]
    bp, lp, d = x_prompt.shape
    bs, ls, _ = x_sample.shape

    zeros = lambda shape: jnp.zeros(shape, F32)
    yp, kp, vp, cp, sp, pp = _trunk(
        x_prompt.reshape(bp * lp, d), bp, lp, 0,
        zeros((n_ab, bp, CONV_WIDTH - 1, CONV_CH)), zeros((n_ab, bp, SSM_HEADS, SSM_HEAD_DIM, SSM_STATE)),
        zeros((n_pool, bp, POOL_STATE_LEN, d)), p)

    past_len = page_table.shape[1] * cache_k.shape[2]
    ys, ks, vs, cs, ss, ps = _trunk(
        x_sample.reshape(bs * ls, d), bs, ls, past_len, state_conv, state_ssm, state_pool, p,
        paged=(cache_k, cache_v, page_table))
    return (yp.reshape(bp, lp, d), ys.reshape(bs, ls, d), kp, vp, cp, sp, pp, ks, vs, cs, ss, ps)
```

```python
import functools
import math

import numpy as np
import jax
import jax.numpy as jnp
from jax import lax
from jax.experimental import pallas as pl
from jax.experimental.pallas import tpu as pltpu

F32 = jnp.float32
BF16 = jnp.bfloat16

RMS_EPS = 1e-6
ROPE_THETA = 10000.0
LOG2_E = 1.4426950408889634

DIFF_HEADS = 4
DIFF_HEAD_DIM = 64
HEAD_LANES = 2 * DIFF_HEAD_DIM
QK_WIDTH = DIFF_HEADS * HEAD_LANES
SSM_INNER = 512
SSM_HEAD_DIM = 64
SSM_HEADS = 8
SSM_GROUPS = 2
SSM_STATE = 128
GROUP_LANES = SSM_INNER // SSM_GROUPS
CONV_WIDTH = 4
CONV_CH = SSM_INNER + 2 * SSM_GROUPS * SSM_STATE
SSD_CHUNK = 128
SSD_CHUNKS_PER_STEP = 4
POOL_WINDOWS = (2, 4, 8, 16)
POOL_GROUP_CH = 256
POOL_HALO = 16
POOL_STATE_LEN = 15
MAIN_PROJ = 2 * QK_WIDTH + QK_WIDTH + SSM_INNER + CONV_CH

LANES = 128
SUBLANES = 8
MXU_DIM = 256
VMEM_LIMIT_BIG = 58 * 1024 * 1024
VMEM_LIMIT_SMALL = 40 * 1024 * 1024

FFN_CHUNK = MXU_DIM
TOKEN_TILE = 512
FFN_TOKEN_TILE = 1024
ATTN_Q_TILE = 512
ATTN_K_TILE = 256
ATTN_SUM_ROWS = 16
ATTN_LOOKAHEAD = 4


def _cparams(sem, vmem):
    return pltpu.CompilerParams(dimension_semantics=sem, vmem_limit_bytes=vmem)


def _tile(m, pref):
    t = min(m, pref)
    while m % t:
        t //= 2
    return t


def _rms(x, w):
    ms = jnp.mean(x * x, axis=-1, keepdims=True)
    return x * lax.rsqrt(ms + RMS_EPS) * w


def _silu(x):
    return x * jax.nn.sigmoid(x)


def _dot(a, b):
    return jnp.dot(a, b, preferred_element_type=F32)


def _dot_nt(a, b):
    return lax.dot_general(a, b, (((1,), (1,)), ((), ())), preferred_element_type=F32)


def _split3(a):
    hi = a.astype(BF16)
    r = a - hi.astype(F32)
    mid = r.astype(BF16)
    return hi, mid, (r - mid.astype(F32)).astype(BF16)


def _dot_exact(a, b, exact_lhs=False):
    if exact_lhs:
        a = a.astype(BF16)
        return sum(_dot(a, t) for t in _split3(b))
    b = b.astype(BF16)
    return sum(_dot(t, b) for t in _split3(a))


def _row_spec(tile, width):
    return pl.BlockSpec((tile, width), lambda i: (i, 0))


def _const_spec(shape):
    return pl.BlockSpec(shape, lambda *_: (0,) * len(shape))


def _rope_table_kernel(inv_ref, cos_ref, sin_ref, *, pos0, period):
    shape = cos_ref.shape
    row = lax.broadcasted_iota(jnp.int32, shape, 0)
    lane = lax.broadcasted_iota(jnp.int32, shape, 1)
    pos = pos0 + lax.rem(row, period)
    ang = pos.astype(F32) * inv_ref[...]
    cos_ref[...] = jnp.cos(ang)
    s = jnp.sin(ang)
    sin_ref[...] = jnp.where(lax.rem(lane, DIFF_HEAD_DIM) < DIFF_HEAD_DIM // 2, -s, s)


def _rope_tables(rows, pos0, period):
    half = DIFF_HEAD_DIM // 2
    inv = 1.0 / (ROPE_THETA ** (jnp.arange(0, DIFF_HEAD_DIM, 2, dtype=F32) / DIFF_HEAD_DIM))
    inv = jnp.tile(inv, LANES // half)[None, :]
    return pl.pallas_call(
        functools.partial(_rope_table_kernel, pos0=pos0, period=period),
        out_shape=(jax.ShapeDtypeStruct((rows, LANES), F32),) * 2,
        name="rope_table",
    )(inv)


def _ffn_kernel(*refs, n_chunks, has_mix, has_final):
    it = iter(refs)
    x_ref = next(it)
    if has_mix:
        a_ref, y_ref, wo_ref = next(it), next(it), next(it)
    nw_ref, wg_ref, wu_ref, wd_ref = next(it), next(it), next(it), next(it)
    fw_ref = next(it) if has_final else None
    o_ref = next(it)

    x = x_ref[...]
    if has_mix:
        half = a_ref.shape[1]
        x = x + _dot(a_ref[...].astype(BF16), wo_ref[:half, :]) + _dot(y_ref[...].astype(BF16), wo_ref[half:, :])
    u = _rms(x, nw_ref[...]).astype(BF16)
    acc = jnp.zeros(x.shape, F32)
    for c in range(n_chunks):
        sl = slice(c * FFN_CHUNK, (c + 1) * FFN_CHUNK)
        g = _dot(u, wg_ref[:, sl])
        up = _dot(u, wu_ref[:, sl])
        h = (_silu(g) * up).astype(BF16)
        acc = acc + _dot(h, wd_ref[sl, :])
    y = x + 0.5 * acc
    if has_final:
        y = _rms(y, fw_ref[...])
    o_ref[...] = y


def _stacked_spec(w, idx):
    n_lead = len(idx)
    return pl.BlockSpec((None,) * n_lead + w.shape[n_lead:], lambda *_: tuple(idx) + (0,) * (w.ndim - n_lead),
                        pipeline_mode=pl.Buffered(1))


def _ffn(x, nw, wg, wu, wd, w_idx, mix=None, final_w=None):
    m, d = x.shape
    hidden = wg.shape[-1]
    tm = _tile(m, FFN_TOKEN_TILE)
    args, specs = [x], [_row_spec(tm, d)]
    if mix is not None:
        a, y, wo, wo_idx = mix
        args += [a, y, wo]
        specs += [_row_spec(tm, a.shape[1]), _row_spec(tm, y.shape[1]), _stacked_spec(wo, (wo_idx,))]
    args += [nw, wg, wu, wd]
    specs += [_const_spec(nw.shape), _stacked_spec(wg, w_idx), _stacked_spec(wu, w_idx), _stacked_spec(wd, w_idx)]
    if final_w is not None:
        args.append(final_w)
        specs.append(_const_spec(final_w.shape))
    return pl.pallas_call(
        functools.partial(_ffn_kernel, n_chunks=hidden // FFN_CHUNK, has_mix=mix is not None,
                          has_final=final_w is not None),
        grid=(m // tm,),
        in_specs=specs,
        out_specs=_row_spec(tm, d),
        out_shape=jax.ShapeDtypeStruct((m, d), F32),
        compiler_params=_cparams(("parallel",), VMEM_LIMIT_BIG),
        name="ffn",
    )(*args)


def _block_diag_queries(q, t_new):
    qt = jnp.concatenate([q] * (2 * DIFF_HEADS), axis=0)
    row = lax.broadcasted_iota(jnp.int32, qt.shape, 0)
    col = lax.broadcasted_iota(jnp.int32, qt.shape, 1)
    same = lax.div(row, t_new) == lax.div(col, DIFF_HEAD_DIM)
    return jnp.where(same, qt, 0.0).astype(BF16)


def _softmax_update(scores, m_s, l_s):
    m = m_s[...]
    m_new = m
    for s in scores:
        m_new = jnp.maximum(m_new, jnp.max(s, axis=1, keepdims=True))
    alpha = jnp.exp(m - m_new)
    l = alpha * l_s[...]
    probs = []
    for s in scores:
        p = jnp.exp(s - m_new)
        l = l + jnp.sum(p, axis=1, keepdims=True)
        probs.append(p.astype(BF16))
    m_s[...] = m_new
    l_s[...] = l
    return alpha, probs


def _weighted_values(alpha, probs, values, acc_s, head_rows):
    acc = alpha * acc_s[...]
    for pb, v_heads in zip(probs, values):
        acc = acc + jnp.concatenate(
            [_dot(pb[h * head_rows:(h + 1) * head_rows], v_heads[h]) for h in range(DIFF_HEADS)], axis=0)
    acc_s[...] = acc


def _ffn_host_kernel(pt_ref, *refs, n_chunks, has_mix, has_final, n_pages, steps_per_seq, n_steps, seq0, layer):
    it = iter(refs)
    x_ref = next(it)
    if has_mix:
        a_ref, y_ref, wo_ref = next(it), next(it), next(it)
    nw_ref, wg_ref, wu_ref, wd_ref = next(it), next(it), next(it), next(it)
    fw_ref = next(it) if has_final else None
    q_ref, ck_ref, cv_ref = next(it), next(it), next(it)
    o_ref, m_out, l_out, acc_out = next(it), next(it), next(it), next(it)
    wq_s, m_s, l_s, acc_s, kbuf, vbuf, ksem, vsem = (next(it) for _ in range(8))
    step = pl.program_id(0)
    part = lax.rem(step, steps_per_seq)
    slot = lax.rem(step, 2)
    t_new = q_ref.shape[0]
    page = kbuf.shape[3]
    group = 2
    n_groups = n_pages // group

    def page_copies(of_step, into_slot):
        seq = seq0 + of_step // steps_per_seq
        first = lax.rem(of_step, steps_per_seq) * n_pages
        copies = []
        for g in range(n_pages):
            phys = pt_ref[seq, first + g]
            copies.append(pltpu.make_async_copy(ck_ref.at[layer, phys], kbuf.at[into_slot, g], ksem.at[into_slot, g]))
            copies.append(pltpu.make_async_copy(cv_ref.at[layer, phys], vbuf.at[into_slot, g], vsem.at[into_slot, g]))
        return copies

    @pl.when(step == 0)
    def _():
        for cp in page_copies(step, slot):
            cp.start()

    for cp in page_copies(step, slot):
        cp.wait()
    nxt = jnp.minimum(step + 1, n_steps - 1)
    for cp in page_copies(nxt, 1 - slot):
        cp.start()

    @pl.when(part == 0)
    def _():
        wq_s[...] = _block_diag_queries(q_ref[...], t_new)
        m_s[...] = jnp.full(m_s.shape, -jnp.inf, F32)
        l_s[...] = jnp.zeros(l_s.shape, F32)
        acc_s[...] = jnp.zeros(acc_s.shape, F32)

    def page_scores(g):
        wq = wq_s[...]
        return [_dot(wq, kbuf[slot, p].astype(BF16)) for p in range(g * group, (g + 1) * group)]

    def page_values(g):
        return [[vbuf[slot, p, pl.ds(h, page, stride=DIFF_HEADS), :].astype(BF16) for h in range(DIFF_HEADS)]
                for p in range(g * group, (g + 1) * group)]

    x = x_ref[...]
    if has_mix:
        half = a_ref.shape[1]
        x = x + _dot(a_ref[...].astype(BF16), wo_ref[:half, :]) + _dot(y_ref[...].astype(BF16), wo_ref[half:, :])
    u = _rms(x, nw_ref[...]).astype(BF16)
    acc = jnp.zeros(x.shape, F32)
    scores = None
    for c in range(n_chunks):
        sl = slice(c * FFN_CHUNK, (c + 1) * FFN_CHUNK)
        g = _dot(u, wg_ref[:, sl])
        up = _dot(u, wu_ref[:, sl])
        h = (_silu(g) * up).astype(BF16)
        acc = acc + _dot(h, wd_ref[sl, :])
        if scores is not None:
            alpha, probs = _softmax_update(scores, m_s, l_s)
            _weighted_values(alpha, probs, page_values(c - 1), acc_s, 2 * t_new)
        scores = page_scores(c) if c < n_groups else None
    assert scores is None
    y = x + 0.5 * acc
    if has_final:
        y = _rms(y, fw_ref[...])
    o_ref[...] = y

    @pl.when(part == steps_per_seq - 1)
    def _():
        m_out[...] = m_s[...]
        l_out[...] = l_s[...]
        acc_out[...] = acc_s[...]

    @pl.when(step == n_steps - 1)
    def _():
        for cp in page_copies(nxt, 1 - slot):
            cp.wait()


def _ffn_host(x, nw, wg, wu, wd, w_idx, paged, q, seq0, n_seq, mix=None, final_w=None):
    cache_k, cache_v, page_table, layer = paged
    m, d = x.shape
    hidden = wg.shape[-1]
    tm = _tile(m, TOKEN_TILE)
    n_steps = m // tm
    n_log = page_table.shape[1]
    steps_per_seq = n_steps // n_seq
    n_pages = n_log // steps_per_seq
    assert steps_per_seq * n_seq == n_steps and n_pages * steps_per_seq == n_log and n_pages % 2 == 0
    assert n_pages // 2 < hidden // FFN_CHUNK
    t_new = q.shape[0] // page_table.shape[0]
    n_rows = 2 * DIFF_HEADS * t_new
    n_layers, n_phys, page = cache_k.shape[:3]
    ck = jnp.transpose(cache_k, (0, 1, 3, 4, 5, 2)).reshape(n_layers, n_phys, QK_WIDTH, page)
    cv = cache_v.reshape(n_layers, n_phys, page * DIFF_HEADS, HEAD_LANES)

    rows = lambda w: pl.BlockSpec((tm, w), lambda i, pt: (i, 0))
    const = lambda a: pl.BlockSpec(a.shape, lambda i, pt: (0,) * a.ndim)

    def stacked(w, idx):
        n_lead = len(idx)
        return pl.BlockSpec((None,) * n_lead + w.shape[n_lead:], lambda i, pt: tuple(idx) + (0,) * (w.ndim - n_lead),
                            pipeline_mode=pl.Buffered(1))

    args, specs = [x], [rows(d)]
    if mix is not None:
        a, y, wo, wo_idx = mix
        args += [a, y, wo]
        specs += [rows(a.shape[1]), rows(y.shape[1]), stacked(wo, (wo_idx,))]
    args += [nw, wg, wu, wd]
    specs += [const(nw), stacked(wg, w_idx), stacked(wu, w_idx), stacked(wd, w_idx)]
    if final_w is not None:
        args.append(final_w)
        specs.append(const(final_w))
    args += [q, ck, cv]
    specs += [pl.BlockSpec((t_new, QK_WIDTH), lambda i, pt: (seq0 + i // steps_per_seq, 0)),
              pl.BlockSpec(memory_space=pl.ANY), pl.BlockSpec(memory_space=pl.ANY)]
    stat = lambda w: pl.BlockSpec((n_rows, w), lambda i, pt: (i // steps_per_seq, 0))
    grid_spec = pltpu.PrefetchScalarGridSpec(
        num_scalar_prefetch=1,
        grid=(n_steps,),
        in_specs=specs,
        out_specs=[rows(d), stat(1), stat(1), stat(HEAD_LANES)],
        scratch_shapes=[pltpu.VMEM((n_rows, QK_WIDTH), BF16), pltpu.VMEM((n_rows, 1), F32),
                        pltpu.VMEM((n_rows, 1), F32), pltpu.VMEM((n_rows, HEAD_LANES), F32),
                        pltpu.VMEM((2, n_pages, QK_WIDTH, page), F32),
                        pltpu.VMEM((2, n_pages, page * DIFF_HEADS, HEAD_LANES), F32),
                        pltpu.SemaphoreType.DMA((2, n_pages)), pltpu.SemaphoreType.DMA((2, n_pages))],
    )
    return pl.pallas_call(
        functools.partial(_ffn_host_kernel, n_chunks=hidden // FFN_CHUNK, has_mix=mix is not None,
                          has_final=final_w is not None, n_pages=n_pages, steps_per_seq=steps_per_seq,
                          n_steps=n_steps, seq0=seq0, layer=layer),
        grid_spec=grid_spec,
        out_shape=[jax.ShapeDtypeStruct((m, d), F32), jax.ShapeDtypeStruct((n_seq * n_rows, 1), F32),
                   jax.ShapeDtypeStruct((n_seq * n_rows, 1), F32), jax.ShapeDtypeStruct((n_seq * n_rows, HEAD_LANES), F32)],
        compiler_params=_cparams(("arbitrary",), VMEM_LIMIT_BIG),
        name="ffn_host",
    )(page_table, *args)


def _inproj_kernel(x_ref, nw_ref, w_ref, wdt_ref, cos_ref, sin_ref, *out_refs, prompt_layout):
    if prompt_layout:
        qt_ref, kt_ref, kb_ref, v4_ref, vt_ref, z_ref, xbc_ref, dt_ref = out_refs
    else:
        q_ref, k_ref, v_ref, z_ref, xbc_ref, dt_ref = out_refs
    u = _rms(x_ref[...], nw_ref[...]).astype(BF16)
    tm = u.shape[0]
    reps = QK_WIDTH // LANES
    cos = jnp.concatenate([cos_ref[...]] * reps, axis=1)
    sin = jnp.concatenate([sin_ref[...]] * reps, axis=1)
    lane = lax.broadcasted_iota(jnp.int32, (tm, QK_WIDTH), 1)
    low_half = lax.rem(lane, DIFF_HEAD_DIM) < DIFF_HEAD_DIM // 2
    half = DIFF_HEAD_DIM // 2

    def rope(t):
        up = pltpu.roll(t, QK_WIDTH - half, 1)
        down = pltpu.roll(t, half, 1)
        return t * cos + jnp.where(low_half, up, down) * sin

    q_scale = DIFF_HEAD_DIM ** -0.5 * (LOG2_E if prompt_layout else 1.0)
    q = rope(_dot(u, w_ref[:, 0:QK_WIDTH])) * q_scale
    k = rope(_dot(u, w_ref[:, QK_WIDTH:2 * QK_WIDTH]))
    v = _dot(u, w_ref[:, 2 * QK_WIDTH:3 * QK_WIDTH])
    if prompt_layout:
        def head_blocks(t_ref, t):
            tb = t_ref.shape[-1]
            for h in range(DIFF_HEADS):
                for jb in range(tm // tb):
                    t_ref[h, jb] = t[h * HEAD_LANES:(h + 1) * HEAD_LANES, jb * tb:(jb + 1) * tb].astype(BF16)

        head_blocks(qt_ref, q.T)
        kt_ref[...] = k.T
        kb_ref[...] = k.astype(BF16)
        for h in range(DIFF_HEADS):
            v4_ref[pl.ds(h, tm, stride=DIFF_HEADS), :] = v[:, h * HEAD_LANES:(h + 1) * HEAD_LANES]
        head_blocks(vt_ref, v.T)
    else:
        q_ref[...] = q
        k_ref[...] = k
        v_ref[...] = v
    z0 = 3 * QK_WIDTH
    z_ref[...] = _dot(u, w_ref[:, z0:z0 + SSM_INNER])
    xbc_ref[...] = _dot(u, w_ref[:, z0 + SSM_INNER:z0 + SSM_INNER + CONV_CH])
    dt_ref[...] = _dot(u, wdt_ref[...])


def _inproj(x, nw, w_in, w_idx, w_dt, cos, sin, prompt_layout):
    m, d = x.shape
    period = cos.shape[0]
    tm = _tile(period, TOKEN_TILE)
    n_per = period // tm
    tab_spec = pl.BlockSpec((tm, LANES), lambda i: (i % n_per, 0))
    rows = lambda w, dt: (_row_spec(tm, w), jax.ShapeDtypeStruct((m, w), dt))
    tail = [rows(SSM_INNER, F32), rows(CONV_CH, F32), rows(LANES, F32)]
    if prompt_layout:
        n_seq = m // period

        def blocked(tile):
            ta = _tile(tm, tile)
            return (pl.BlockSpec((None, DIFF_HEADS, tm // ta, HEAD_LANES, ta), lambda i: (i // n_per, 0, i % n_per, 0, 0)),
                    jax.ShapeDtypeStruct((n_seq, DIFF_HEADS, period // ta, HEAD_LANES, ta), BF16))

        kt_out = (pl.BlockSpec((None, QK_WIDTH, tm), lambda i: (i // n_per, 0, i % n_per)),
                  jax.ShapeDtypeStruct((n_seq, QK_WIDTH, period), F32))
        v4_out = (_row_spec(tm * DIFF_HEADS, HEAD_LANES), jax.ShapeDtypeStruct((m * DIFF_HEADS, HEAD_LANES), F32))
        outs = [blocked(ATTN_Q_TILE), kt_out, rows(QK_WIDTH, BF16), v4_out, blocked(ATTN_K_TILE)] + tail
    else:
        outs = [rows(QK_WIDTH, F32)] * 3 + tail
    return pl.pallas_call(
        functools.partial(_inproj_kernel, prompt_layout=prompt_layout),
        grid=(m // tm,),
        in_specs=[_row_spec(tm, d), _const_spec(nw.shape), pl.BlockSpec((None,) + w_in.shape[1:], lambda i: (w_idx, 0, 0)),
                  _const_spec(w_dt.shape), tab_spec, tab_spec],
        out_specs=[spec for spec, _ in outs],
        out_shape=[shape for _, shape in outs],
        compiler_params=_cparams(("parallel",), VMEM_LIMIT_SMALL),
        name="inproj",
    )(x, nw, w_in, w_dt, cos, sin)


def _lambda(lam_ref, lambda_init):
    v = lam_ref[...]
    d1 = jnp.sum(v[0:1] * v[1:2], axis=1, keepdims=True)
    d2 = jnp.sum(v[2:3] * v[3:4], axis=1, keepdims=True)
    return jnp.exp(d1) - jnp.exp(d2) + lambda_init


def _subln(d, w, lambda_init):
    return _rms(d, w) * (1.0 - lambda_init)


def _attn_prompt_kernel(lam_ref, sub_ref, qt_ref, k_ref, vt_ref, o_ref, m_s, acc_s, *, lambda_init):
    n_qb, _, tqb = qt_ref.shape
    n_kb, _, tk = vt_ref.shape
    strip = tk
    n_strips = n_qb * tqb // strip
    sub = lax.broadcasted_iota(jnp.int32, (HEAD_LANES, strip), 0)
    first_comp = sub < DIFF_HEAD_DIM
    krow = lax.broadcasted_iota(jnp.int32, (tk, strip), 0)
    qcol = lax.broadcasted_iota(jnp.int32, (tk, strip), 1)
    causal = krow <= qcol
    zero = jnp.zeros((HEAD_LANES, strip), BF16)

    def cols(qs, comp):
        return slice((comp * n_strips + qs) * strip, (comp * n_strips + qs + 1) * strip)

    units = [(kb, qs, comp) for kb in range(n_kb) for qs in range(kb, n_strips) for comp in range(2)]

    def scores(unit):
        kb, qs, comp = unit
        qb, off = divmod(qs * strip, tqb)
        qt = qt_ref[qb, :, off:off + strip]
        q = jnp.where(first_comp, qt, zero) if comp == 0 else jnp.where(first_comp, zero, qt)
        return _dot(k_ref[kb * tk:(kb + 1) * tk, :], q)

    pending = [scores(u) for u in units[:ATTN_LOOKAHEAD]]
    ones = jnp.ones((ATTN_SUM_ROWS, tk), BF16)
    for i, (kb, qs, comp) in enumerate(units):
        if i + ATTN_LOOKAHEAD < len(units):
            pending.append(scores(units[i + ATTN_LOOKAHEAD]))
        s = pending.pop(0)
        c = cols(qs, comp)
        if comp == 0 and qs == kb:
            vt = jnp.concatenate([vt_ref[kb], ones], axis=0)
        if kb == qs:
            s = jnp.where(causal, s, -jnp.inf)
        if kb == 0:
            m = jnp.max(s, axis=0, keepdims=True)
            acc_s[:, c] = _dot(vt, jnp.exp2(s - m).astype(BF16))
        else:
            m_old = m_s[:, c]
            m = jnp.maximum(m_old, jnp.max(s, axis=0, keepdims=True))
            acc_s[:, c] = jnp.exp2(m_old - m) * acc_s[:, c] + _dot(vt, jnp.exp2(s - m).astype(BF16))
        m_s[:, c] = m
    lam = _lambda(lam_ref, lambda_init)
    for qs in range(n_strips):
        o0, o1 = (acc_s[:HEAD_LANES, cols(qs, comp)] * (1.0 / acc_s[HEAD_LANES:HEAD_LANES + 1, cols(qs, comp)])
                  for comp in range(2))
        d = (o0 - lam * o1).T
        o_ref[qs * strip:(qs + 1) * strip, :] = _subln(d, sub_ref[...], lambda_init).astype(o_ref.dtype)


def _attn_prompt(qt, kb, vt, lam_vecs, sub_w, lambda_init):
    batch, _, nq, _, tq = qt.shape
    nk, tk = vt.shape[2], vt.shape[4]
    seq = nq * tq
    q_spec = pl.BlockSpec((None, None, nq, HEAD_LANES, tq), lambda b, h: (b, h, 0, 0, 0))
    k_spec = pl.BlockSpec((seq, HEAD_LANES), lambda b, h: (b, h))
    v_spec = pl.BlockSpec((None, None, nk, HEAD_LANES, tk), lambda b, h: (b, h, 0, 0, 0))
    return pl.pallas_call(
        functools.partial(_attn_prompt_kernel, lambda_init=lambda_init),
        grid=(batch, DIFF_HEADS),
        in_specs=[pl.BlockSpec(lam_vecs.shape, lambda b, h: (0, 0)), pl.BlockSpec(sub_w.shape, lambda b, h: (0, 0)),
                  q_spec, k_spec, v_spec],
        out_specs=k_spec,
        out_shape=jax.ShapeDtypeStruct(kb.shape, BF16),
        scratch_shapes=[pltpu.VMEM((1, 2 * seq), F32), pltpu.VMEM((HEAD_LANES + ATTN_SUM_ROWS, 2 * seq), F32)],
        compiler_params=_cparams(("parallel", "parallel"), VMEM_LIMIT_SMALL),
        name="attn_prompt",
    )(lam_vecs, sub_w, qt, kb, vt)


def _attn_merge_kernel(lam_ref, sub_ref, q_ref, kn_ref, vn_ref, m_ref, l_ref, acc_ref, o_ref, m_s, l_s, acc_s,
                       *, lambda_init):
    t_new = q_ref.shape[0]
    head_rows = 2 * t_new
    page = LANES
    wq = _block_diag_queries(q_ref[...], t_new)
    pad = jnp.zeros((page - t_new, QK_WIDTH), F32)
    kn = jnp.concatenate([kn_ref[...], pad], axis=0).astype(BF16)
    vn = jnp.concatenate([vn_ref[...], pad], axis=0).astype(BF16)
    s = _dot_nt(wq, kn)
    row = lax.broadcasted_iota(jnp.int32, s.shape, 0)
    col = lax.broadcasted_iota(jnp.int32, s.shape, 1)
    s = jnp.where(col <= lax.rem(row, t_new), s, -jnp.inf)
    m_s[...] = m_ref[...]
    l_s[...] = l_ref[...]
    acc_s[...] = acc_ref[...]
    alpha, probs = _softmax_update([s], m_s, l_s)
    _weighted_values(alpha, probs, [[vn[:, h * HEAD_LANES:(h + 1) * HEAD_LANES] for h in range(DIFF_HEADS)]],
                     acc_s, head_rows)
    o = acc_s[...] / l_s[...]
    lam = _lambda(lam_ref, lambda_init)
    for h in range(DIFF_HEADS):
        r0 = h * head_rows
        d = o[r0:r0 + t_new] - lam * o[r0 + t_new:r0 + head_rows]
        o_ref[:, h * HEAD_LANES:(h + 1) * HEAD_LANES] = _subln(d, sub_ref[...], lambda_init)


def _attn_merge(q, k_new, v_new, m_past, l_past, acc_past, batch, lam_vecs, sub_w, lambda_init):
    m = q.shape[0]
    t_new = m // batch
    n_rows = 2 * DIFF_HEADS * t_new
    tok = pl.BlockSpec((t_new, QK_WIDTH), lambda b: (b, 0))
    stat = lambda w: pl.BlockSpec((n_rows, w), lambda b: (b, 0))
    return pl.pallas_call(
        functools.partial(_attn_merge_kernel, lambda_init=lambda_init),
        grid=(batch,),
        in_specs=[_const_spec(lam_vecs.shape), _const_spec(sub_w.shape), tok, tok, tok, stat(1), stat(1), stat(HEAD_LANES)],
        out_specs=tok,
        out_shape=jax.ShapeDtypeStruct((m, QK_WIDTH), F32),
        scratch_shapes=[pltpu.VMEM((n_rows, 1), F32), pltpu.VMEM((n_rows, 1), F32), pltpu.VMEM((n_rows, HEAD_LANES), F32)],
        compiler_params=_cparams(("parallel",), VMEM_LIMIT_SMALL),
        name="attn_merge",
    )(lam_vecs, sub_w, q, k_new, v_new, m_past, l_past, acc_past)


def _ssd_kernel(xbc_ref, z_ref, dt_ref, cw_ref, cb_ref, dtb_ref, alog_ref, dsk_ref, nw_ref, exp_ref,
                cinit_ref, h0_ref, y_ref, cout_ref, hout_ref, ext_s, st_s, *, t_in, n_chunks):
    c = pl.program_id(1)
    T = SSD_CHUNK
    tail = CONV_WIDTH - 1

    @pl.when(c == 0)
    def _():
        ext_s[0:SUBLANES, :] = jnp.zeros((SUBLANES, CONV_CH), F32)
        ext_s[SUBLANES - tail:SUBLANES, :] = cinit_ref[...]
        st_s[...] = h0_ref[...].reshape(SSM_INNER, SSM_STATE).T

    u = xbc_ref[...]
    ext_s[SUBLANES:SUBLANES + t_in, :] = u
    cw = cw_ref[...]
    conv = cb_ref[...] + cw[tail:tail + 1] * u
    for j in range(tail):
        conv = conv + cw[j:j + 1] * ext_s[SUBLANES - tail + j:SUBLANES - tail + j + t_in, :]
    ext_s[0:SUBLANES, :] = ext_s[t_in:t_in + SUBLANES, :]
    xc_all = _silu(conv)
    dt_all = jax.nn.softplus(dt_ref[...] + dtb_ref[...])
    z_all = z_ref[...]
    if t_in < T:
        xc_all = jnp.concatenate([xc_all, jnp.zeros((T - t_in, CONV_CH), F32)], axis=0)
        dt_all = jnp.concatenate([dt_all, jnp.zeros((T - t_in, LANES), F32)], axis=0)
        z_all = jnp.concatenate([z_all, jnp.zeros((T - t_in, SSM_INNER), F32)], axis=0)

    a = -jnp.exp(alog_ref[...])
    row = lax.broadcasted_iota(jnp.int32, (T, T), 0)
    col = lax.broadcasted_iota(jnp.int32, (T, T), 1)
    causal = row >= col
    tri = causal.astype(F32)
    expand = exp_ref[...]
    lane = lax.broadcasted_iota(jnp.int32, (T, LANES), 1)
    heads_per_group = SSM_HEADS // SSM_GROUPS
    nw = nw_ref[...]
    for ci in range(max(1, t_in // T)):
        rows = slice(ci * T, (ci + 1) * T)
        xc, dt, z = xc_all[rows], dt_all[rows], z_all[rows]
        acs = _dot_exact(tri, dt * a, exact_lhs=True)
        acs_t = acs.T
        both_x = _dot_exact(jnp.concatenate([dt, acs], axis=0), expand)
        dt_x, acs_x = both_x[:T], both_x[T:]
        last = acs_x[T - 1:T, :]
        decay_out = jnp.exp(acs_x)
        decay_end = jnp.exp(last - acs_x)
        chunk_decay = jnp.exp(last)

        xs = xc[:, :SSM_INNER]
        xdt = xs * dt_x
        xdt_b = xdt.astype(BF16)
        xw_b = (xdt * decay_end).astype(BF16)
        state = st_s[...]
        state_b = state.astype(BF16)
        ys = []
        for g in range(SSM_GROUPS):
            b0 = SSM_INNER + g * SSM_STATE
            c0 = SSM_INNER + SSM_GROUPS * SSM_STATE + g * SSM_STATE
            bg_t = xc[:, b0:b0 + SSM_STATE].T.astype(BF16)
            cg = xc[:, c0:c0 + SSM_STATE].astype(BF16)
            gl = slice(g * GROUP_LANES, (g + 1) * GROUP_LANES)
            cb = _dot(cg, bg_t)
            y_off = _dot(cg, state_b[:, gl]) * decay_out[:, gl]
            st_s[:, gl] = state[:, gl] * chunk_decay[:, gl] + _dot(bg_t, xw_b[:, gl])
            for pair in range(heads_per_group // 2):
                h0 = g * heads_per_group + 2 * pair
                xp = xdt_b[:, h0 * SSM_HEAD_DIM:(h0 + 2) * SSM_HEAD_DIM]
                outs = []
                for hh in (h0, h0 + 1):
                    seg = acs[:, hh:hh + 1] - acs_t[hh:hh + 1, :]
                    w = cb * jnp.exp(jnp.where(causal, seg, -jnp.inf))
                    outs.append(_dot(w.astype(BF16), xp))
                ys.append(jnp.where(lane < SSM_HEAD_DIM, outs[0], outs[1])
                          + y_off[:, 2 * pair * SSM_HEAD_DIM:(2 * pair + 2) * SSM_HEAD_DIM])
        y = jnp.concatenate(ys, axis=1) + dsk_ref[...] * xs
        y = y * _silu(z)
        parts = []
        for g in range(SSM_GROUPS):
            gl = slice(g * GROUP_LANES, (g + 1) * GROUP_LANES)
            parts.append(_rms(y[:, gl], nw[:, gl]))
        y = jnp.concatenate(parts, axis=1).astype(y_ref.dtype)
        if t_in < T:
            y_ref[...] = y[:t_in]
        else:
            y_ref[rows, :] = y

    @pl.when(c == n_chunks - 1)
    def _():
        cout_ref[...] = ext_s[SUBLANES - tail:SUBLANES, :]
        hout_ref[...] = st_s[...].T.reshape(SSM_HEADS, SSM_HEAD_DIM, SSM_STATE)


def _ssd(xbc, z, dt_raw, p, conv_init, h0, batch, seq, y_dtype):
    t_in = _tile(seq, SSD_CHUNK * SSD_CHUNKS_PER_STEP)
    n_chunks = seq // t_in
    assert t_in % SSD_CHUNK == 0 or n_chunks == 1
    tail = CONV_WIDTH - 1
    rows = lambda w: pl.BlockSpec((t_in, w), lambda b, c: (b * n_chunks + c, 0))
    per_b3 = lambda s: pl.BlockSpec((None,) + s, lambda b, c: (b,) + (0,) * len(s))
    consts = [p['conv_w'], p['conv_b'], p['dt_bias'], p['a_log'], p['d_skip'], p['ssm_norm_w'], p['expand']]
    return pl.pallas_call(
        functools.partial(_ssd_kernel, t_in=t_in, n_chunks=n_chunks),
        grid=(batch, n_chunks),
        in_specs=[rows(CONV_CH), rows(SSM_INNER), rows(LANES)]
                 + [pl.BlockSpec(a.shape, lambda b, c: (0, 0)) for a in consts]
                 + [per_b3((tail, CONV_CH)), per_b3((SSM_HEADS, SSM_HEAD_DIM, SSM_STATE))],
        out_specs=[rows(SSM_INNER), per_b3((tail, CONV_CH)), per_b3((SSM_HEADS, SSM_HEAD_DIM, SSM_STATE))],
        out_shape=[jax.ShapeDtypeStruct((batch * seq, SSM_INNER), y_dtype),
                   jax.ShapeDtypeStruct((batch, tail, CONV_CH), F32),
                   jax.ShapeDtypeStruct((batch, SSM_HEADS, SSM_HEAD_DIM, SSM_STATE), F32)],
        scratch_shapes=[pltpu.VMEM((SUBLANES + t_in, CONV_CH), F32), pltpu.VMEM((SSM_STATE, SSM_INNER), F32)],
        compiler_params=_cparams(("parallel", "arbitrary"), VMEM_LIMIT_SMALL),
        name="ssd",
    )(xbc, z, dt_raw, *consts, conv_init, h0)


def _pool_kernel(x_ref, nw_ref, init_ref, pw_ref, ps_ref, o_ref, pout_ref, ext_s, *, tp, n_tiles, pos0):
    t = pl.program_id(1)
    halo = POOL_HALO

    @pl.when(t == 0)
    def _():
        ext_s[0:halo, :] = jnp.zeros((halo, ext_s.shape[1]), F32)
        ext_s[halo - POOL_STATE_LEN:halo, :] = init_ref[...]

    x = x_ref[...]
    u = _rms(x, nw_ref[...])
    ext_s[halo:halo + tp, :] = u
    e = ext_s[...]
    gc = POOL_GROUP_CH
    sums = [e]
    for lvl in range(len(POOL_WINDOWS)):
        prev = sums[-1][:, gc:] if lvl else sums[-1]
        sums.append(prev + pltpu.roll(prev, 1 << lvl, 0))
    pos = pos0 + t * tp + lax.broadcasted_iota(jnp.int32, (tp, 1), 0)
    mixed = []
    for g, w in enumerate(POOL_WINDOWS):
        cnt = jnp.minimum(w, pos + 1).astype(F32)
        win = sums[g + 1][halo:, :gc]
        pooled = win / cnt - u[:, g * gc:(g + 1) * gc]
        mixed.append(_dot(pooled.astype(BF16), pw_ref[g]))
    o_ref[...] = x + jnp.concatenate(mixed, axis=1) * ps_ref[...]

    @pl.when(t == n_tiles - 1)
    def _():
        pout_ref[...] = ext_s[tp + halo - POOL_STATE_LEN:tp + halo, :]

    ext_s[0:halo, :] = ext_s[tp:tp + halo, :]


def _pool(x, nw, pool_init, pool_w, w_idx, pool_scale, batch, seq, pos0):
    d = x.shape[1]
    tp = _tile(seq, TOKEN_TILE)
    n_tiles = seq // tp
    rows = pl.BlockSpec((tp, d), lambda b, t: (b * n_tiles + t, 0))
    state = pl.BlockSpec((None, POOL_STATE_LEN, d), lambda b, t: (b, 0, 0))
    return pl.pallas_call(
        functools.partial(_pool_kernel, tp=tp, n_tiles=n_tiles, pos0=pos0),
        grid=(batch, n_tiles),
        in_specs=[rows, pl.BlockSpec(nw.shape, lambda b, t: (0, 0)), state,
                  _stacked_spec(pool_w, (w_idx,)), pl.BlockSpec(pool_scale.shape, lambda b, t: (0, 0))],
        out_specs=[rows, state],
        out_shape=[jax.ShapeDtypeStruct(x.shape, F32), jax.ShapeDtypeStruct((batch, POOL_STATE_LEN, d), F32)],
        scratch_shapes=[pltpu.VMEM((POOL_HALO + tp, d), F32)],
        compiler_params=_cparams(("parallel", "arbitrary"), VMEM_LIMIT_SMALL),
        name="pool",
    )(x, nw, pool_init, pool_w, pool_scale)


def _prep_params(norm_w, ffn_w_gate, ffn_w_up, ffn_w_down, ab_w_in, ab_w_out, ab_conv_w, ab_conv_b, ab_dt_bias,
                 ab_a_log, ab_d_skip, ab_ssm_norm_w, ab_lambda_q1, ab_lambda_k1, ab_lambda_q2, ab_lambda_k2,
                 ab_subln_w, pool_w, pool_scale, final_norm_w):
    def lane_pad(v):
        return jnp.pad(v, ((0, 0), (0, LANES - v.shape[1])))[:, None, :]

    expand = np.zeros((LANES, SSM_INNER), np.float32)
    for h in range(SSM_HEADS):
        expand[h, h * SSM_HEAD_DIM:(h + 1) * SSM_HEAD_DIM] = 1.0
    return dict(
        norm_w=norm_w[:, :, None, :],
        wg=ffn_w_gate.astype(BF16), wu=ffn_w_up.astype(BF16), wd=ffn_w_down.astype(BF16),
        w_in=ab_w_in.astype(BF16),
        w_dt=jnp.pad(ab_w_in[:, :, MAIN_PROJ:], ((0, 0), (0, 0), (0, LANES - SSM_HEADS))).astype(BF16),
        w_out=ab_w_out.astype(BF16),
        conv_w=ab_conv_w, conv_b=ab_conv_b[:, None, :],
        dt_bias=lane_pad(ab_dt_bias), a_log=lane_pad(ab_a_log),
        d_skip=jnp.repeat(ab_d_skip, SSM_HEAD_DIM, axis=1)[:, None, :],
        ssm_norm_w=ab_ssm_norm_w[:, None, :],
        lam=jnp.stack([ab_lambda_q1, ab_lambda_k1, ab_lambda_q2, ab_lambda_k2], axis=1),
        subln_w=ab_subln_w[:, None, :],
        pool_w=pool_w.astype(BF16), pool_scale=pool_scale[:, None, :],
        final_norm_w=final_norm_w[None, :],
        expand=jnp.asarray(expand),
    )


def _ssm_params(p, i):
    return dict(conv_w=p['conv_w'][i], conv_b=p['conv_b'][i], dt_bias=p['dt_bias'][i], a_log=p['a_log'][i],
                d_skip=p['d_skip'][i], ssm_norm_w=p['ssm_norm_w'][i], expand=p['expand'])


def _lambda_init(layer):
    return 0.8 - 0.6 * math.exp(-0.3 * layer)


def kernel(x_prompt, x_sample, cache_k, cache_v, state_conv, state_ssm, state_pool, page_table, norm_w, ffn_w_gate, ffn_w_up, ffn_w_down, ab_w_in, ab_w_out, ab_conv_w, ab_conv_b, ab_dt_bias, ab_a_log, ab_d_skip, ab_ssm_norm_w, ab_lambda_q1, ab_lambda_k1, ab_lambda_q2, ab_lambda_k2, ab_subln_w, pool_w, pool_scale, final_norm_w):
    p = _prep_params(norm_w, ffn_w_gate, ffn_w_up, ffn_w_down, ab_w_in, ab_w_out, ab_conv_w, ab_conv_b, ab_dt_bias,
                     ab_a_log, ab_d_skip, ab_ssm_norm_w, ab_lambda_q1, ab_lambda_k1, ab_lambda_q2, ab_lambda_k2,
                     ab_subln_w, pool_w, pool_scale, final_norm_w)
    depth = ffn_w_gate.shape[0]
    n_ab, n_pool = state_conv.shape[0], state_pool.shape[0]
    assert n_ab == 1 and depth == 2
    bp, lp, d = x_prompt.shape
    bs, ls, _ = x_sample.shape
    ffn_w = (p['wg'], p['wu'], p['wd'])
    nw = p['norm_w']
    lam, sub_w, li = p['lam'][0], p['subln_w'][0], _lambda_init(0)
    past_len = page_table.shape[1] * cache_k.shape[2]
    n_hosts = 2 * depth
    host_seqs = bs // n_hosts
    assert host_seqs * n_hosts == bs
    paged = (cache_k, cache_v, page_table, 0)

    xs = x_sample.reshape(bs * ls, d)
    cos_s, sin_s = _rope_tables(bs * ls, past_len, ls)
    xs = _ffn(xs, nw[0][0], *ffn_w, (0, 0))
    qs, ks, vs, zs, xbc_s, dt_s = _inproj(xs, nw[0][1], p['w_in'], 0, p['w_dt'][0], cos_s, sin_s, prompt_layout=False)
    stats = []

    def host(x, norm, w_idx, **kw):
        x, *st = _ffn_host(x, norm, *ffn_w, w_idx, paged, qs, len(stats) * host_seqs, host_seqs, **kw)
        stats.append(st)
        return x

    zeros = lambda shape: jnp.zeros(shape, F32)
    xp = x_prompt.reshape(bp * lp, d)
    cos_p, sin_p = _rope_tables(lp, 0, lp)
    xp = host(xp, nw[0][0], (0, 0))
    qt, kt, kb, v4, vt, zp, xbc_p, dt_p = _inproj(xp, nw[0][1], p['w_in'], 0, p['w_dt'][0], cos_p, sin_p,
                                                  prompt_layout=True)
    attn_p = _attn_prompt(qt, kb, vt, lam, sub_w, li)
    y_p, conv_p, ssm_p = _ssd(xbc_p, zp, dt_p, _ssm_params(p, 0), zeros((bp, CONV_WIDTH - 1, CONV_CH)),
                              zeros((bp, SSM_HEADS, SSM_HEAD_DIM, SSM_STATE)), bp, lp, BF16)
    xp = host(xp, nw[0][2], (0, 1), mix=(attn_p, y_p, p['w_out'], 0))
    xp = host(xp, nw[1][0], (1, 0))
    xp, pool_p = _pool(xp, nw[1][1], zeros((bp, POOL_STATE_LEN, d)), p['pool_w'], 0, p['pool_scale'][0], bp, lp, 0)
    yp = host(xp, nw[1][2], (1, 1), final_w=p['final_norm_w'])
    k_p = jnp.transpose(kt.reshape(bp, DIFF_HEADS, 2, DIFF_HEAD_DIM, lp), (0, 4, 1, 2, 3))

    m_past, l_past, acc_past = (jnp.concatenate([st[j] for st in stats], axis=0) for j in range(3))
    attn_s = _attn_merge(qs, ks, vs, m_past, l_past, acc_past, bs, lam, sub_w, li)
    y_s, conv_s, ssm_s = _ssd(xbc_s, zs, dt_s, _ssm_params(p, 0), state_conv[0], state_ssm[0], bs, ls, F32)
    xs = _ffn(xs, nw[0][2], *ffn_w, (0, 1), mix=(attn_s, y_s, p['w_out'], 0))
    xs = _ffn(xs, nw[1][0], *ffn_w, (1, 0))
    xs, pool_s = _pool(xs, nw[1][1], state_pool[0], p['pool_w'], 0, p['pool_scale'][0], bs, ls, past_len)
    ys = _ffn(xs, nw[1][2], *ffn_w, (1, 1), final_w=p['final_norm_w'])

    lead = lambda t: t[None]
    return (yp.reshape(bp, lp, d), ys.reshape(bs, ls, d),
            lead(k_p.reshape(bp, lp, DIFF_HEADS, 2, DIFF_HEAD_DIM)), lead(v4.reshape(bp, lp, DIFF_HEADS, HEAD_LANES)),
            lead(conv_p), lead(ssm_p), lead(pool_p),
            lead(ks.reshape(bs, ls, DIFF_HEADS, 2, DIFF_HEAD_DIM)), lead(vs.reshape(bs, ls, DIFF_HEADS, HEAD_LANES)),
            lead(conv_s), lead(ssm_s), lead(pool_s))
```

```python
import functools
import math

import numpy as np
import jax
import jax.numpy as jnp
from jax import lax
from jax.experimental import pallas as pl
from jax.experimental.pallas import tpu as pltpu

F32 = jnp.float32
BF16 = jnp.bfloat16

RMS_EPS = 1e-6
ROPE_THETA = 10000.0
LOG2_E = 1.4426950408889634

DIFF_HEADS = 4
DIFF_HEAD_DIM = 64
HEAD_LANES = 2 * DIFF_HEAD_DIM
QK_WIDTH = DIFF_HEADS * HEAD_LANES
SSM_INNER = 512
SSM_HEAD_DIM = 64
SSM_HEADS = 8
SSM_GROUPS = 2
SSM_STATE = 128
GROUP_LANES = SSM_INNER // SSM_GROUPS
CONV_WIDTH = 4
CONV_CH = SSM_INNER + 2 * SSM_GROUPS * SSM_STATE
SSD_CHUNK = 128
SSD_CHUNKS_PER_STEP = 4
POOL_WINDOWS = (2, 4, 8, 16)
POOL_GROUP_CH = 256
POOL_HALO = 16
POOL_STATE_LEN = 15
MAIN_PROJ = 2 * QK_WIDTH + QK_WIDTH + SSM_INNER + CONV_CH

LANES = 128
SUBLANES = 8
MXU_DIM = 256
VMEM_LIMIT_BIG = 58 * 1024 * 1024
VMEM_LIMIT_SMALL = 40 * 1024 * 1024

FFN_CHUNK = MXU_DIM
TOKEN_TILE = 512
FFN_TOKEN_TILE = 1024
ATTN_Q_TILE = 512
ATTN_K_TILE = 256
ATTN_SUM_ROWS = 16
ATTN_LOOKAHEAD = 4


def _cparams(sem, vmem):
    return pltpu.CompilerParams(dimension_semantics=sem, vmem_limit_bytes=vmem)


def _tile(m, pref):
    t = min(m, pref)
    while m % t:
        t //= 2
    return t


def _rms(x, w):
    ms = jnp.mean(x * x, axis=-1, keepdims=True)
    return x * lax.rsqrt(ms + RMS_EPS) * w


def _silu(x):
    return x * jax.nn.sigmoid(x)


def _dot(a, b):
    return jnp.dot(a, b, preferred_element_type=F32)


def _dot_nt(a, b):
    return lax.dot_general(a, b, (((1,), (1,)), ((), ())), preferred_element_type=F32)


def _split3(a):
    hi = a.astype(BF16)
    r = a - hi.astype(F32)
    mid = r.astype(BF16)
    return hi, mid, (r - mid.astype(F32)).astype(BF16)


def _dot_exact(a, b, exact_lhs=False):
    if exact_lhs:
        a = a.astype(BF16)
        return sum(_dot(a, t) for t in _split3(b))
    b = b.astype(BF16)
    return sum(_dot(t, b) for t in _split3(a))


def _row_spec(tile, width):
    return pl.BlockSpec((tile, width), lambda i: (i, 0))


def _const_spec(shape):
    return pl.BlockSpec(shape, lambda *_: (0,) * len(shape))


def _rope_table_kernel(inv_ref, cos_ref, sin_ref, *, pos0, period):
    shape = cos_ref.shape
    row = lax.broadcasted_iota(jnp.int32, shape, 0)
    lane = lax.broadcasted_iota(jnp.int32, shape, 1)
    pos = pos0 + lax.rem(row, period)
    ang = pos.astype(F32) * inv_ref[...]
    cos_ref[...] = jnp.cos(ang)
    s = jnp.sin(ang)
    sin_ref[...] = jnp.where(lax.rem(lane, DIFF_HEAD_DIM) < DIFF_HEAD_DIM // 2, -s, s)


def _rope_tables(rows, pos0, period):
    half = DIFF_HEAD_DIM // 2
    inv = 1.0 / (ROPE_THETA ** (jnp.arange(0, DIFF_HEAD_DIM, 2, dtype=F32) / DIFF_HEAD_DIM))
    inv = jnp.tile(inv, LANES // half)[None, :]
    return pl.pallas_call(
        functools.partial(_rope_table_kernel, pos0=pos0, period=period),
        out_shape=(jax.ShapeDtypeStruct((rows, LANES), F32),) * 2,
        name="rope_table",
    )(inv)


def _ffn_kernel(*refs, n_chunks, has_mix, has_final):
    it = iter(refs)
    x_ref = next(it)
    if has_mix:
        a_ref, y_ref, wo_ref = next(it), next(it), next(it)
    nw_ref, wg_ref, wu_ref, wd_ref = next(it), next(it), next(it), next(it)
    fw_ref = next(it) if has_final else None
    o_ref = next(it)

    x = x_ref[...]
    if has_mix:
        half = a_ref.shape[1]
        x = x + _dot(a_ref[...].astype(BF16), wo_ref[:half, :]) + _dot(y_ref[...].astype(BF16), wo_ref[half:, :])
    u = _rms(x, nw_ref[...]).astype(BF16)
    acc = jnp.zeros(x.shape, F32)
    for c in range(n_chunks):
        sl = slice(c * FFN_CHUNK, (c + 1) * FFN_CHUNK)
        g = _dot(u, wg_ref[:, sl])
        up = _dot(u, wu_ref[:, sl])
        h = (_silu(g) * up).astype(BF16)
        acc = acc + _dot(h, wd_ref[sl, :])
    y = x + 0.5 * acc
    if has_final:
        y = _rms(y, fw_ref[...])
    o_ref[...] = y


def _stacked_spec(w, idx):
    n_lead = len(idx)
    return pl.BlockSpec((None,) * n_lead + w.shape[n_lead:], lambda *_: tuple(idx) + (0,) * (w.ndim - n_lead),
                        pipeline_mode=pl.Buffered(1))


def _ffn(x, nw, wg, wu, wd, w_idx, mix=None, final_w=None):
    m, d = x.shape
    hidden = wg.shape[-1]
    tm = _tile(m, FFN_TOKEN_TILE)
    args, specs = [x], [_row_spec(tm, d)]
    if mix is not None:
        a, y, wo, wo_idx = mix
        args += [a, y, wo]
        specs += [_row_spec(tm, a.shape[1]), _row_spec(tm, y.shape[1]), _stacked_spec(wo, (wo_idx,))]
    args += [nw, wg, wu, wd]
    specs += [_const_spec(nw.shape), _stacked_spec(wg, w_idx), _stacked_spec(wu, w_idx), _stacked_spec(wd, w_idx)]
    if final_w is not None:
        args.append(final_w)
        specs.append(_const_spec(final_w.shape))
    return pl.pallas_call(
        functools.partial(_ffn_kernel, n_chunks=hidden // FFN_CHUNK, has_mix=mix is not None,
                          has_final=final_w is not None),
        grid=(m // tm,),
        in_specs=specs,
        out_specs=_row_spec(tm, d),
        out_shape=jax.ShapeDtypeStruct((m, d), F32),
        compiler_params=_cparams(("parallel",), VMEM_LIMIT_BIG),
        name="ffn",
    )(*args)


def _block_diag_queries(q, t_new):
    qt = jnp.concatenate([q] * (2 * DIFF_HEADS), axis=0)
    row = lax.broadcasted_iota(jnp.int32, qt.shape, 0)
    col = lax.broadcasted_iota(jnp.int32, qt.shape, 1)
    same = lax.div(row, t_new) == lax.div(col, DIFF_HEAD_DIM)
    return jnp.where(same, qt, 0.0).astype(BF16)


def _softmax_update(scores, m_s, l_s):
    m = m_s[...]
    m_new = m
    for s in scores:
        m_new = jnp.maximum(m_new, jnp.max(s, axis=1, keepdims=True))
    alpha = jnp.exp(m - m_new)
    l = alpha * l_s[...]
    probs = []
    for s in scores:
        p = jnp.exp(s - m_new)
        l = l + jnp.sum(p, axis=1, keepdims=True)
        probs.append(p.astype(BF16))
    m_s[...] = m_new
    l_s[...] = l
    return alpha, probs


def _weighted_values(alpha, probs, values, acc_s, head_rows):
    acc = alpha * acc_s[...]
    for pb, v_heads in zip(probs, values):
        acc = acc + jnp.concatenate(
            [_dot(pb[h * head_rows:(h + 1) * head_rows], v_heads[h]) for h in range(DIFF_HEADS)], axis=0)
    acc_s[...] = acc


def _ffn_host_kernel(pt_ref, *refs, n_chunks, has_mix, has_final, n_pages, steps_per_seq, n_steps, seq0, layer):
    it = iter(refs)
    x_ref = next(it)
    if has_mix:
        a_ref, y_ref, wo_ref = next(it), next(it), next(it)
    nw_ref, wg_ref, wu_ref, wd_ref = next(it), next(it), next(it), next(it)
    fw_ref = next(it) if has_final else None
    q_ref, ck_ref, cv_ref = next(it), next(it), next(it)
    o_ref, m_out, l_out, acc_out = next(it), next(it), next(it), next(it)
    wq_s, m_s, l_s, acc_s, kbuf, vbuf, ksem, vsem = (next(it) for _ in range(8))
    step = pl.program_id(0)
    part = lax.rem(step, steps_per_seq)
    slot = lax.rem(step, 2)
    t_new = q_ref.shape[0]
    page = kbuf.shape[3]
    group = 2
    n_groups = n_pages // group

    def start_pages(of_step, into_slot):
        seq = seq0 + of_step // steps_per_seq
        first = lax.rem(of_step, steps_per_seq) * n_pages
        for g in range(n_pages):
            phys = pt_ref[seq, first + g]
            pltpu.make_async_copy(ck_ref.at[layer, phys], kbuf.at[into_slot, g], ksem.at[into_slot]).start()
            pltpu.make_async_copy(cv_ref.at[layer, phys], vbuf.at[into_slot, g], vsem.at[into_slot]).start()

    def wait_pages(in_slot):
        pltpu.make_async_copy(ck_ref.at[layer, pl.ds(0, n_pages)], kbuf.at[in_slot], ksem.at[in_slot]).wait()
        pltpu.make_async_copy(cv_ref.at[layer, pl.ds(0, n_pages)], vbuf.at[in_slot], vsem.at[in_slot]).wait()

    @pl.when(step == 0)
    def _():
        start_pages(step, slot)

    wait_pages(slot)
    start_pages(jnp.minimum(step + 1, n_steps - 1), 1 - slot)

    @pl.when(part == 0)
    def _():
        wq_s[...] = _block_diag_queries(q_ref[...], t_new)
        m_s[...] = jnp.full(m_s.shape, -jnp.inf, F32)
        l_s[...] = jnp.zeros(l_s.shape, F32)
        acc_s[...] = jnp.zeros(acc_s.shape, F32)

    def page_scores(g):
        wq = wq_s[...]
        return [_dot(wq, kbuf[slot, p].astype(BF16)) for p in range(g * group, (g + 1) * group)]

    def page_values(g):
        return [[vbuf[slot, p, pl.ds(h, page, stride=DIFF_HEADS), :].astype(BF16) for h in range(DIFF_HEADS)]
                for p in range(g * group, (g + 1) * group)]

    x = x_ref[...]
    if has_mix:
        half = a_ref.shape[1]
        x = x + _dot(a_ref[...].astype(BF16), wo_ref[:half, :]) + _dot(y_ref[...].astype(BF16), wo_ref[half:, :])
    u = _rms(x, nw_ref[...]).astype(BF16)
    acc = jnp.zeros(x.shape, F32)
    scores = None
    for c in range(n_chunks):
        sl = slice(c * FFN_CHUNK, (c + 1) * FFN_CHUNK)
        g = _dot(u, wg_ref[:, sl])
        up = _dot(u, wu_ref[:, sl])
        h = (_silu(g) * up).astype(BF16)
        acc = acc + _dot(h, wd_ref[sl, :])
        if scores is not None:
            alpha, probs = _softmax_update(scores, m_s, l_s)
            _weighted_values(alpha, probs, page_values(c - 1), acc_s, 2 * t_new)
        scores = page_scores(c) if c < n_groups else None
    assert scores is None
    y = x + 0.5 * acc
    if has_final:
        y = _rms(y, fw_ref[...])
    o_ref[...] = y

    @pl.when(part == steps_per_seq - 1)
    def _():
        m_out[...] = m_s[...]
        l_out[...] = l_s[...]
        acc_out[...] = acc_s[...]

    @pl.when(step == n_steps - 1)
    def _():
        wait_pages(1 - slot)


def _ffn_host(x, nw, wg, wu, wd, w_idx, paged, q, seq0, n_seq, mix=None, final_w=None):
    cache_k, cache_v, page_table, layer = paged
    m, d = x.shape
    hidden = wg.shape[-1]
    tm = _tile(m, TOKEN_TILE)
    n_steps = m // tm
    n_log = page_table.shape[1]
    steps_per_seq = n_steps // n_seq
    n_pages = n_log // steps_per_seq
    assert steps_per_seq * n_seq == n_steps and n_pages * steps_per_seq == n_log and n_pages % 2 == 0
    assert n_pages // 2 < hidden // FFN_CHUNK
    t_new = q.shape[0] // page_table.shape[0]
    n_rows = 2 * DIFF_HEADS * t_new
    n_layers, n_phys, page = cache_k.shape[:3]
    ck = jnp.transpose(cache_k, (0, 1, 3, 4, 5, 2)).reshape(n_layers, n_phys, QK_WIDTH, page)
    cv = cache_v.reshape(n_layers, n_phys, page * DIFF_HEADS, HEAD_LANES)

    rows = lambda w: pl.BlockSpec((tm, w), lambda i, pt: (i, 0))
    const = lambda a: pl.BlockSpec(a.shape, lambda i, pt: (0,) * a.ndim)

    def stacked(w, idx):
        n_lead = len(idx)
        return pl.BlockSpec((None,) * n_lead + w.shape[n_lead:], lambda i, pt: tuple(idx) + (0,) * (w.ndim - n_lead),
                            pipeline_mode=pl.Buffered(1))

    args, specs = [x], [rows(d)]
    if mix is not None:
        a, y, wo, wo_idx = mix
        args += [a, y, wo]
        specs += [rows(a.shape[1]), rows(y.shape[1]), stacked(wo, (wo_idx,))]
    args += [nw, wg, wu, wd]
    specs += [const(nw), stacked(wg, w_idx), stacked(wu, w_idx), stacked(wd, w_idx)]
    if final_w is not None:
        args.append(final_w)
        specs.append(const(final_w))
    args += [q, ck, cv]
    specs += [pl.BlockSpec((t_new, QK_WIDTH), lambda i, pt: (seq0 + i // steps_per_seq, 0)),
              pl.BlockSpec(memory_space=pl.ANY), pl.BlockSpec(memory_space=pl.ANY)]
    stat = lambda w: pl.BlockSpec((n_rows, w), lambda i, pt: (i // steps_per_seq, 0))
    grid_spec = pltpu.PrefetchScalarGridSpec(
        num_scalar_prefetch=1,
        grid=(n_steps,),
        in_specs=specs,
        out_specs=[rows(d), stat(1), stat(1), stat(HEAD_LANES)],
        scratch_shapes=[pltpu.VMEM((n_rows, QK_WIDTH), BF16), pltpu.VMEM((n_rows, 1), F32),
                        pltpu.VMEM((n_rows, 1), F32), pltpu.VMEM((n_rows, HEAD_LANES), F32),
                        pltpu.VMEM((2, n_pages, QK_WIDTH, page), F32),
                        pltpu.VMEM((2, n_pages, page * DIFF_HEADS, HEAD_LANES), F32),
                        pltpu.SemaphoreType.DMA((2,)), pltpu.SemaphoreType.DMA((2,))],
    )
    return pl.pallas_call(
        functools.partial(_ffn_host_kernel, n_chunks=hidden // FFN_CHUNK, has_mix=mix is not None,
                          has_final=final_w is not None, n_pages=n_pages, steps_per_seq=steps_per_seq,
                          n_steps=n_steps, seq0=seq0, layer=layer),
        grid_spec=grid_spec,
        out_shape=[jax.ShapeDtypeStruct((m, d), F32), jax.ShapeDtypeStruct((n_seq * n_rows, 1), F32),
                   jax.ShapeDtypeStruct((n_seq * n_rows, 1), F32), jax.ShapeDtypeStruct((n_seq * n_rows, HEAD_LANES), F32)],
        compiler_params=_cparams(("arbitrary",), VMEM_LIMIT_BIG),
        name="ffn_host",
    )(page_table, *args)


def _inproj_kernel(x_ref, nw_ref, w_ref, wdt_ref, cos_ref, sin_ref, *out_refs, prompt_layout):
    if prompt_layout:
        qt_ref, kt_ref, kb_ref, v4_ref, vt_ref, z_ref, xbc_ref, dt_ref = out_refs
    else:
        q_ref, k_ref, v_ref, z_ref, xbc_ref, dt_ref = out_refs
    u = _rms(x_ref[...], nw_ref[...]).astype(BF16)
    tm = u.shape[0]
    reps = QK_WIDTH // LANES
    cos = jnp.concatenate([cos_ref[...]] * reps, axis=1)
    sin = jnp.concatenate([sin_ref[...]] * reps, axis=1)
    lane = lax.broadcasted_iota(jnp.int32, (tm, QK_WIDTH), 1)
    low_half = lax.rem(lane, DIFF_HEAD_DIM) < DIFF_HEAD_DIM // 2
    half = DIFF_HEAD_DIM // 2

    def rope(t):
        up = pltpu.roll(t, QK_WIDTH - half, 1)
        down = pltpu.roll(t, half, 1)
        return t * cos + jnp.where(low_half, up, down) * sin

    q_scale = DIFF_HEAD_DIM ** -0.5 * (LOG2_E if prompt_layout else 1.0)
    q = rope(_dot(u, w_ref[:, 0:QK_WIDTH])) * q_scale
    k = rope(_dot(u, w_ref[:, QK_WIDTH:2 * QK_WIDTH]))
    v = _dot(u, w_ref[:, 2 * QK_WIDTH:3 * QK_WIDTH])
    if prompt_layout:
        def head_blocks(t_ref, t):
            tb = t_ref.shape[-1]
            for h in range(DIFF_HEADS):
                for jb in range(tm // tb):
                    t_ref[h, jb] = t[h * HEAD_LANES:(h + 1) * HEAD_LANES, jb * tb:(jb + 1) * tb].astype(BF16)

        head_blocks(qt_ref, q.T)
        kt_ref[...] = k.T
        kb_ref[...] = k.astype(BF16)
        for h in range(DIFF_HEADS):
            v4_ref[pl.ds(h, tm, stride=DIFF_HEADS), :] = v[:, h * HEAD_LANES:(h + 1) * HEAD_LANES]
        head_blocks(vt_ref, v.T)
    else:
        q_ref[...] = q
        k_ref[...] = k
        v_ref[...] = v
    z0 = 3 * QK_WIDTH
    z_ref[...] = _dot(u, w_ref[:, z0:z0 + SSM_INNER])
    xbc_ref[...] = _dot(u, w_ref[:, z0 + SSM_INNER:z0 + SSM_INNER + CONV_CH])
    dt_ref[...] = _dot(u, wdt_ref[...])


def _inproj(x, nw, w_in, w_idx, w_dt, cos, sin, prompt_layout):
    m, d = x.shape
    period = cos.shape[0]
    tm = _tile(period, TOKEN_TILE)
    n_per = period // tm
    tab_spec = pl.BlockSpec((tm, LANES), lambda i: (i % n_per, 0))
    rows = lambda w, dt: (_row_spec(tm, w), jax.ShapeDtypeStruct((m, w), dt))
    tail = [rows(SSM_INNER, F32), rows(CONV_CH, F32), rows(LANES, F32)]
    if prompt_layout:
        n_seq = m // period

        def blocked(tile):
            ta = _tile(tm, tile)
            return (pl.BlockSpec((None, DIFF_HEADS, tm // ta, HEAD_LANES, ta), lambda i: (i // n_per, 0, i % n_per, 0, 0)),
                    jax.ShapeDtypeStruct((n_seq, DIFF_HEADS, period // ta, HEAD_LANES, ta), BF16))

        kt_out = (pl.BlockSpec((None, QK_WIDTH, tm), lambda i: (i // n_per, 0, i % n_per)),
                  jax.ShapeDtypeStruct((n_seq, QK_WIDTH, period), F32))
        v4_out = (_row_spec(tm * DIFF_HEADS, HEAD_LANES), jax.ShapeDtypeStruct((m * DIFF_HEADS, HEAD_LANES), F32))
        outs = [blocked(ATTN_Q_TILE), kt_out, rows(QK_WIDTH, BF16), v4_out, blocked(ATTN_K_TILE)] + tail
    else:
        outs = [rows(QK_WIDTH, F32)] * 3 + tail
    return pl.pallas_call(
        functools.partial(_inproj_kernel, prompt_layout=prompt_layout),
        grid=(m // tm,),
        in_specs=[_row_spec(tm, d), _const_spec(nw.shape), pl.BlockSpec((None,) + w_in.shape[1:], lambda i: (w_idx, 0, 0)),
                  _const_spec(w_dt.shape), tab_spec, tab_spec],
        out_specs=[spec for spec, _ in outs],
        out_shape=[shape for _, shape in outs],
        compiler_params=_cparams(("parallel",), VMEM_LIMIT_SMALL),
        name="inproj",
    )(x, nw, w_in, w_dt, cos, sin)


def _lambda(lam_ref, lambda_init):
    v = lam_ref[...]
    d1 = jnp.sum(v[0:1] * v[1:2], axis=1, keepdims=True)
    d2 = jnp.sum(v[2:3] * v[3:4], axis=1, keepdims=True)
    return jnp.exp(d1) - jnp.exp(d2) + lambda_init


def _subln(d, w, lambda_init):
    return _rms(d, w) * (1.0 - lambda_init)


def _attn_prompt_kernel(lam_ref, sub_ref, qt_ref, k_ref, vt_ref, o_ref, m_s, acc_s, *, lambda_init):
    n_qb, _, tqb = qt_ref.shape
    n_kb, _, tk = vt_ref.shape
    strip = tk
    n_strips = n_qb * tqb // strip
    sub = lax.broadcasted_iota(jnp.int32, (HEAD_LANES, strip), 0)
    first_comp = sub < DIFF_HEAD_DIM
    krow = lax.broadcasted_iota(jnp.int32, (tk, strip), 0)
    qcol = lax.broadcasted_iota(jnp.int32, (tk, strip), 1)
    causal = krow <= qcol
    zero = jnp.zeros((HEAD_LANES, strip), BF16)

    def cols(qs, comp):
        return slice((comp * n_strips + qs) * strip, (comp * n_strips + qs + 1) * strip)

    units = [(kb, qs, comp) for kb in range(n_kb) for qs in range(kb, n_strips) for comp in range(2)]

    def scores(unit):
        kb, qs, comp = unit
        qb, off = divmod(qs * strip, tqb)
        qt = qt_ref[qb, :, off:off + strip]
        q = jnp.where(first_comp, qt, zero) if comp == 0 else jnp.where(first_comp, zero, qt)
        return _dot(k_ref[kb * tk:(kb + 1) * tk, :], q)

    pending = [scores(u) for u in units[:ATTN_LOOKAHEAD]]
    ones = jnp.ones((ATTN_SUM_ROWS, tk), BF16)
    for i, (kb, qs, comp) in enumerate(units):
        if i + ATTN_LOOKAHEAD < len(units):
            pending.append(scores(units[i + ATTN_LOOKAHEAD]))
        s = pending.pop(0)
        c = cols(qs, comp)
        if comp == 0 and qs == kb:
            vt = jnp.concatenate([vt_ref[kb], ones], axis=0)
        if kb == qs:
            s = jnp.where(causal, s, -jnp.inf)
        if kb == 0:
            m = jnp.max(s, axis=0, keepdims=True)
            acc_s[:, c] = _dot(vt, jnp.exp2(s - m).astype(BF16))
        else:
            m_old = m_s[:, c]
            m = jnp.maximum(m_old, jnp.max(s, axis=0, keepdims=True))
            acc_s[:, c] = jnp.exp2(m_old - m) * acc_s[:, c] + _dot(vt, jnp.exp2(s - m).astype(BF16))
        m_s[:, c] = m
    lam = _lambda(lam_ref, lambda_init)
    for qs in range(n_strips):
        o0, o1 = (acc_s[:HEAD_LANES, cols(qs, comp)] * (1.0 / acc_s[HEAD_LANES:HEAD_LANES + 1, cols(qs, comp)])
                  for comp in range(2))
        d = (o0 - lam * o1).T
        o_ref[qs * strip:(qs + 1) * strip, :] = _subln(d, sub_ref[...], lambda_init).astype(o_ref.dtype)


def _attn_prompt(qt, kb, vt, lam_vecs, sub_w, lambda_init):
    batch, _, nq, _, tq = qt.shape
    nk, tk = vt.shape[2], vt.shape[4]
    seq = nq * tq
    q_spec = pl.BlockSpec((None, None, nq, HEAD_LANES, tq), lambda b, h: (b, h, 0, 0, 0))
    k_spec = pl.BlockSpec((seq, HEAD_LANES), lambda b, h: (b, h))
    v_spec = pl.BlockSpec((None, None, nk, HEAD_LANES, tk), lambda b, h: (b, h, 0, 0, 0))
    return pl.pallas_call(
        functools.partial(_attn_prompt_kernel, lambda_init=lambda_init),
        grid=(batch, DIFF_HEADS),
        in_specs=[pl.BlockSpec(lam_vecs.shape, lambda b, h: (0, 0)), pl.BlockSpec(sub_w.shape, lambda b, h: (0, 0)),
                  q_spec, k_spec, v_spec],
        out_specs=k_spec,
        out_shape=jax.ShapeDtypeStruct(kb.shape, BF16),
        scratch_shapes=[pltpu.VMEM((1, 2 * seq), F32), pltpu.VMEM((HEAD_LANES + ATTN_SUM_ROWS, 2 * seq), F32)],
        compiler_params=_cparams(("parallel", "parallel"), VMEM_LIMIT_SMALL),
        name="attn_prompt",
    )(lam_vecs, sub_w, qt, kb, vt)


def _attn_merge_kernel(lam_ref, sub_ref, q_ref, kn_ref, vn_ref, m_ref, l_ref, acc_ref, o_ref, m_s, l_s, acc_s,
                       *, lambda_init):
    t_new = q_ref.shape[0]
    head_rows = 2 * t_new
    page = LANES
    wq = _block_diag_queries(q_ref[...], t_new)
    pad = jnp.zeros((page - t_new, QK_WIDTH), F32)
    kn = jnp.concatenate([kn_ref[...], pad], axis=0).astype(BF16)
    vn = jnp.concatenate([vn_ref[...], pad], axis=0).astype(BF16)
    s = _dot_nt(wq, kn)
    row = lax.broadcasted_iota(jnp.int32, s.shape, 0)
    col = lax.broadcasted_iota(jnp.int32, s.shape, 1)
    s = jnp.where(col <= lax.rem(row, t_new), s, -jnp.inf)
    m_s[...] = m_ref[...]
    l_s[...] = l_ref[...]
    acc_s[...] = acc_ref[...]
    alpha, probs = _softmax_update([s], m_s, l_s)
    _weighted_values(alpha, probs, [[vn[:, h * HEAD_LANES:(h + 1) * HEAD_LANES] for h in range(DIFF_HEADS)]],
                     acc_s, head_rows)
    o = acc_s[...] / l_s[...]
    lam = _lambda(lam_ref, lambda_init)
    for h in range(DIFF_HEADS):
        r0 = h * head_rows
        d = o[r0:r0 + t_new] - lam * o[r0 + t_new:r0 + head_rows]
        o_ref[:, h * HEAD_LANES:(h + 1) * HEAD_LANES] = _subln(d, sub_ref[...], lambda_init)


def _attn_merge(q, k_new, v_new, m_past, l_past, acc_past, batch, lam_vecs, sub_w, lambda_init):
    m = q.shape[0]
    t_new = m // batch
    n_rows = 2 * DIFF_HEADS * t_new
    tok = pl.BlockSpec((t_new, QK_WIDTH), lambda b: (b, 0))
    stat = lambda w: pl.BlockSpec((n_rows, w), lambda b: (b, 0))
    return pl.pallas_call(
        functools.partial(_attn_merge_kernel, lambda_init=lambda_init),
        grid=(batch,),
        in_specs=[_const_spec(lam_vecs.shape), _const_spec(sub_w.shape), tok, tok, tok, stat(1), stat(1), stat(HEAD_LANES)],
        out_specs=tok,
        out_shape=jax.ShapeDtypeStruct((m, QK_WIDTH), F32),
        scratch_shapes=[pltpu.VMEM((n_rows, 1), F32), pltpu.VMEM((n_rows, 1), F32), pltpu.VMEM((n_rows, HEAD_LANES), F32)],
        compiler_params=_cparams(("parallel",), VMEM_LIMIT_SMALL),
        name="attn_merge",
    )(lam_vecs, sub_w, q, k_new, v_new, m_past, l_past, acc_past)


def _ssd_kernel(xbc_ref, z_ref, dt_ref, cw_ref, cb_ref, dtb_ref, alog_ref, dsk_ref, nw_ref, exp_ref,
                cinit_ref, h0_ref, y_ref, cout_ref, hout_ref, ext_s, st_s, *, t_in, n_chunks):
    c = pl.program_id(1)
    T = SSD_CHUNK
    tail = CONV_WIDTH - 1

    @pl.when(c == 0)
    def _():
        ext_s[0:SUBLANES, :] = jnp.zeros((SUBLANES, CONV_CH), F32)
        ext_s[SUBLANES - tail:SUBLANES, :] = cinit_ref[...]
        st_s[...] = h0_ref[...].reshape(SSM_INNER, SSM_STATE).T

    u = xbc_ref[...]
    ext_s[SUBLANES:SUBLANES + t_in, :] = u
    cw = cw_ref[...]
    conv = cb_ref[...] + cw[tail:tail + 1] * u
    for j in range(tail):
        conv = conv + cw[j:j + 1] * ext_s[SUBLANES - tail + j:SUBLANES - tail + j + t_in, :]
    ext_s[0:SUBLANES, :] = ext_s[t_in:t_in + SUBLANES, :]
    xc_all = _silu(conv)
    dt_all = jax.nn.softplus(dt_ref[...] + dtb_ref[...])
    z_all = z_ref[...]
    if t_in < T:
        xc_all = jnp.concatenate([xc_all, jnp.zeros((T - t_in, CONV_CH), F32)], axis=0)
        dt_all = jnp.concatenate([dt_all, jnp.zeros((T - t_in, LANES), F32)], axis=0)
        z_all = jnp.concatenate([z_all, jnp.zeros((T - t_in, SSM_INNER), F32)], axis=0)

    a = -jnp.exp(alog_ref[...])
    row = lax.broadcasted_iota(jnp.int32, (T, T), 0)
    col = lax.broadcasted_iota(jnp.int32, (T, T), 1)
    causal = row >= col
    tri = causal.astype(F32)
    expand = exp_ref[...]
    lane = lax.broadcasted_iota(jnp.int32, (T, LANES), 1)
    heads_per_group = SSM_HEADS // SSM_GROUPS
    nw = nw_ref[...]
    for ci in range(max(1, t_in // T)):
        rows = slice(ci * T, (ci + 1) * T)
        xc, dt, z = xc_all[rows], dt_all[rows], z_all[rows]
        acs = _dot_exact(tri, dt * a, exact_lhs=True)
        acs_t = acs.T
        both_x = _dot_exact(jnp.concatenate([dt, acs], axis=0), expand)
        dt_x, acs_x = both_x[:T], both_x[T:]
        last = acs_x[T - 1:T, :]
        decay_out = jnp.exp(acs_x)
        decay_end = jnp.exp(last - acs_x)
        chunk_decay = jnp.exp(last)

        xs = xc[:, :SSM_INNER]
        xdt = xs * dt_x
        xdt_b = xdt.astype(BF16)
        xw_b = (xdt * decay_end).astype(BF16)
        state = st_s[...]
        state_b = state.astype(BF16)
        ys = []
        for g in range(SSM_GROUPS):
            b0 = SSM_INNER + g * SSM_STATE
            c0 = SSM_INNER + SSM_GROUPS * SSM_STATE + g * SSM_STATE
            bg_t = xc[:, b0:b0 + SSM_STATE].T.astype(BF16)
            cg = xc[:, c0:c0 + SSM_STATE].astype(BF16)
            gl = slice(g * GROUP_LANES, (g + 1) * GROUP_LANES)
            cb = _dot(cg, bg_t)
            y_off = _dot(cg, state_b[:, gl]) * decay_out[:, gl]
            st_s[:, gl] = state[:, gl] * chunk_decay[:, gl] + _dot(bg_t, xw_b[:, gl])
            for pair in range(heads_per_group // 2):
                h0 = g * heads_per_group + 2 * pair
                xp = xdt_b[:, h0 * SSM_HEAD_DIM:(h0 + 2) * SSM_HEAD_DIM]
                outs = []
                for hh in (h0, h0 + 1):
                    seg = acs[:, hh:hh + 1] - acs_t[hh:hh + 1, :]
                    w = cb * jnp.exp(jnp.where(causal, seg, -jnp.inf))
                    outs.append(_dot(w.astype(BF16), xp))
                ys.append(jnp.where(lane < SSM_HEAD_DIM, outs[0], outs[1])
                          + y_off[:, 2 * pair * SSM_HEAD_DIM:(2 * pair + 2) * SSM_HEAD_DIM])
        y = jnp.concatenate(ys, axis=1) + dsk_ref[...] * xs
        y = y * _silu(z)
        parts = []
        for g in range(SSM_GROUPS):
            gl = slice(g * GROUP_LANES, (g + 1) * GROUP_LANES)
            parts.append(_rms(y[:, gl], nw[:, gl]))
        y = jnp.concatenate(parts, axis=1).astype(y_ref.dtype)
        if t_in < T:
            y_ref[...] = y[:t_in]
        else:
            y_ref[rows, :] = y

    @pl.when(c == n_chunks - 1)
    def _():
        cout_ref[...] = ext_s[SUBLANES - tail:SUBLANES, :]
        hout_ref[...] = st_s[...].T.reshape(SSM_HEADS, SSM_HEAD_DIM, SSM_STATE)


def _ssd(xbc, z, dt_raw, p, conv_init, h0, batch, seq, y_dtype):
    t_in = _tile(seq, SSD_CHUNK * SSD_CHUNKS_PER_STEP)
    n_chunks = seq // t_in
    assert t_in % SSD_CHUNK == 0 or n_chunks == 1
    tail = CONV_WIDTH - 1
    rows = lambda w: pl.BlockSpec((t_in, w), lambda b, c: (b * n_chunks + c, 0))
    per_b3 = lambda s: pl.BlockSpec((None,) + s, lambda b, c: (b,) + (0,) * len(s))
    consts = [p['conv_w'], p['conv_b'], p['dt_bias'], p['a_log'], p['d_skip'], p['ssm_norm_w'], p['expand']]
    return pl.pallas_call(
        functools.partial(_ssd_kernel, t_in=t_in, n_chunks=n_chunks),
        grid=(batch, n_chunks),
        in_specs=[rows(CONV_CH), rows(SSM_INNER), rows(LANES)]
                 + [pl.BlockSpec(a.shape, lambda b, c: (0, 0)) for a in consts]
                 + [per_b3((tail, CONV_CH)), per_b3((SSM_HEADS, SSM_HEAD_DIM, SSM_STATE))],
        out_specs=[rows(SSM_INNER), per_b3((tail, CONV_CH)), per_b3((SSM_HEADS, SSM_HEAD_DIM, SSM_STATE))],
        out_shape=[jax.ShapeDtypeStruct((batch * seq, SSM_INNER), y_dtype),
                   jax.ShapeDtypeStruct((batch, tail, CONV_CH), F32),
                   jax.ShapeDtypeStruct((batch, SSM_HEADS, SSM_HEAD_DIM, SSM_STATE), F32)],
        scratch_shapes=[pltpu.VMEM((SUBLANES + t_in, CONV_CH), F32), pltpu.VMEM((SSM_STATE, SSM_INNER), F32)],
        compiler_params=_cparams(("parallel", "arbitrary"), VMEM_LIMIT_SMALL),
        name="ssd",
    )(xbc, z, dt_raw, *consts, conv_init, h0)


def _pool_kernel(x_ref, nw_ref, init_ref, pw_ref, ps_ref, o_ref, pout_ref, ext_s, *, tp, n_tiles, pos0):
    t = pl.program_id(1)
    halo = POOL_HALO

    @pl.when(t == 0)
    def _():
        ext_s[0:halo, :] = jnp.zeros((halo, ext_s.shape[1]), F32)
        ext_s[halo - POOL_STATE_LEN:halo, :] = init_ref[...]

    x = x_ref[...]
    u = _rms(x, nw_ref[...])
    ext_s[halo:halo + tp, :] = u
    e = ext_s[...]
    gc = POOL_GROUP_CH
    sums = [e]
    for lvl in range(len(POOL_WINDOWS)):
        prev = sums[-1][:, gc:] if lvl else sums[-1]
        sums.append(prev + pltpu.roll(prev, 1 << lvl, 0))
    pos = pos0 + t * tp + lax.broadcasted_iota(jnp.int32, (tp, 1), 0)
    mixed = []
    for g, w in enumerate(POOL_WINDOWS):
        cnt = jnp.minimum(w, pos + 1).astype(F32)
        win = sums[g + 1][halo:, :gc]
        pooled = win / cnt - u[:, g * gc:(g + 1) * gc]
        mixed.append(_dot(pooled.astype(BF16), pw_ref[g]))
    o_ref[...] = x + jnp.concatenate(mixed, axis=1) * ps_ref[...]

    @pl.when(t == n_tiles - 1)
    def _():
        pout_ref[...] = ext_s[tp + halo - POOL_STATE_LEN:tp + halo, :]

    ext_s[0:halo, :] = ext_s[tp:tp + halo, :]


def _pool(x, nw, pool_init, pool_w, w_idx, pool_scale, batch, seq, pos0):
    d = x.shape[1]
    tp = _tile(seq, TOKEN_TILE)
    n_tiles = seq // tp
    rows = pl.BlockSpec((tp, d), lambda b, t: (b * n_tiles + t, 0))
    state = pl.BlockSpec((None, POOL_STATE_LEN, d), lambda b, t: (b, 0, 0))
    return pl.pallas_call(
        functools.partial(_pool_kernel, tp=tp, n_tiles=n_tiles, pos0=pos0),
        grid=(batch, n_tiles),
        in_specs=[rows, pl.BlockSpec(nw.shape, lambda b, t: (0, 0)), state,
                  _stacked_spec(pool_w, (w_idx,)), pl.BlockSpec(pool_scale.shape, lambda b, t: (0, 0))],
        out_specs=[rows, state],
        out_shape=[jax.ShapeDtypeStruct(x.shape, F32), jax.ShapeDtypeStruct((batch, POOL_STATE_LEN, d), F32)],
        scratch_shapes=[pltpu.VMEM((POOL_HALO + tp, d), F32)],
        compiler_params=_cparams(("parallel", "arbitrary"), VMEM_LIMIT_SMALL),
        name="pool",
    )(x, nw, pool_init, pool_w, pool_scale)


def _prep_params(norm_w, ffn_w_gate, ffn_w_up, ffn_w_down, ab_w_in, ab_w_out, ab_conv_w, ab_conv_b, ab_dt_bias,
                 ab_a_log, ab_d_skip, ab_ssm_norm_w, ab_lambda_q1, ab_lambda_k1, ab_lambda_q2, ab_lambda_k2,
                 ab_subln_w, pool_w, pool_scale, final_norm_w):
    def lane_pad(v):
        return jnp.pad(v, ((0, 0), (0, LANES - v.shape[1])))[:, None, :]

    expand = np.zeros((LANES, SSM_INNER), np.float32)
    for h in range(SSM_HEADS):
        expand[h, h * SSM_HEAD_DIM:(h + 1) * SSM_HEAD_DIM] = 1.0
    return dict(
        norm_w=norm_w[:, :, None, :],
        wg=ffn_w_gate.astype(BF16), wu=ffn_w_up.astype(BF16), wd=ffn_w_down.astype(BF16),
        w_in=ab_w_in.astype(BF16),
        w_dt=jnp.pad(ab_w_in[:, :, MAIN_PROJ:], ((0, 0), (0, 0), (0, LANES - SSM_HEADS))).astype(BF16),
        w_out=ab_w_out.astype(BF16),
        conv_w=ab_conv_w, conv_b=ab_conv_b[:, None, :],
        dt_bias=lane_pad(ab_dt_bias), a_log=lane_pad(ab_a_log),
        d_skip=jnp.repeat(ab_d_skip, SSM_HEAD_DIM, axis=1)[:, None, :],
        ssm_norm_w=ab_ssm_norm_w[:, None, :],
        lam=jnp.stack([ab_lambda_q1, ab_lambda_k1, ab_lambda_q2, ab_lambda_k2], axis=1),
        subln_w=ab_subln_w[:, None, :],
        pool_w=pool_w.astype(BF16), pool_scale=pool_scale[:, None, :],
        final_norm_w=final_norm_w[None, :],
        expand=jnp.asarray(expand),
    )


def _ssm_params(p, i):
    return dict(conv_w=p['conv_w'][i], conv_b=p['conv_b'][i], dt_bias=p['dt_bias'][i], a_log=p['a_log'][i],
                d_skip=p['d_skip'][i], ssm_norm_w=p['ssm_norm_w'][i], expand=p['expand'])


def _lambda_init(layer):
    return 0.8 - 0.6 * math.exp(-0.3 * layer)


def kernel(x_prompt, x_sample, cache_k, cache_v, state_conv, state_ssm, state_pool, page_table, norm_w, ffn_w_gate, ffn_w_up, ffn_w_down, ab_w_in, ab_w_out, ab_conv_w, ab_conv_b, ab_dt_bias, ab_a_log, ab_d_skip, ab_ssm_norm_w, ab_lambda_q1, ab_lambda_k1, ab_lambda_q2, ab_lambda_k2, ab_subln_w, pool_w, pool_scale, final_norm_w):
    p = _prep_params(norm_w, ffn_w_gate, ffn_w_up, ffn_w_down, ab_w_in, ab_w_out, ab_conv_w, ab_conv_b, ab_dt_bias,
                     ab_a_log, ab_d_skip, ab_ssm_norm_w, ab_lambda_q1, ab_lambda_k1, ab_lambda_q2, ab_lambda_k2,
                     ab_subln_w, pool_w, pool_scale, final_norm_w)
    depth = ffn_w_gate.shape[0]
    n_ab, n_pool = state_conv.shape[0], state_pool.shape[0]
    assert n_ab == 1 and depth == 2
    bp, lp, d = x_prompt.shape
    bs, ls, _ = x_sample.shape
    ffn_w = (p['wg'], p['wu'], p['wd'])
    nw = p['norm_w']
    lam, sub_w, li = p['lam'][0], p['subln_w'][0], _lambda_init(0)
    past_len = page_table.shape[1] * cache_k.shape[2]
    n_hosts = 2 * depth
    host_seqs = bs // n_hosts
    assert host_seqs * n_hosts == bs
    paged = (cache_k, cache_v, page_table, 0)

    xs = x_sample.reshape(bs * ls, d)
    cos_s, sin_s = _rope_tables(bs * ls, past_len, ls)
    xs = _ffn(xs, nw[0][0], *ffn_w, (0, 0))
    qs, ks, vs, zs, xbc_s, dt_s = _inproj(xs, nw[0][1], p['w_in'], 0, p['w_dt'][0], cos_s, sin_s, prompt_layout=False)
    stats = []

    def host(x, norm, w_idx, **kw):
        x, *st = _ffn_host(x, norm, *ffn_w, w_idx, paged, qs, len(stats) * host_seqs, host_seqs, **kw)
        stats.append(st)
        return x

    zeros = lambda shape: jnp.zeros(shape, F32)
    xp = x_prompt.reshape(bp * lp, d)
    cos_p, sin_p = _rope_tables(lp, 0, lp)
    xp = host(xp, nw[0][0], (0, 0))
    qt, kt, kb, v4, vt, zp, xbc_p, dt_p = _inproj(xp, nw[0][1], p['w_in'], 0, p['w_dt'][0], cos_p, sin_p,
                                                  prompt_layout=True)
    attn_p = _attn_prompt(qt, kb, vt, lam, sub_w, li)
    y_p, conv_p, ssm_p = _ssd(xbc_p, zp, dt_p, _ssm_params(p, 0), zeros((bp, CONV_WIDTH - 1, CONV_CH)),
                              zeros((bp, SSM_HEADS, SSM_HEAD_DIM, SSM_STATE)), bp, lp, BF16)
    xp = host(xp, nw[0][2], (0, 1), mix=(attn_p, y_p, p['w_out'], 0))
    xp = host(xp, nw[1][0], (1, 0))
    xp, pool_p = _pool(xp, nw[1][1], zeros((bp, POOL_STATE_LEN, d)), p['pool_w'], 0, p['pool_scale'][0], bp, lp, 0)
    yp = host(xp, nw[1][2], (1, 1), final_w=p['final_norm_w'])
    k_p = jnp.transpose(kt.reshape(bp, DIFF_HEADS, 2, DIFF_HEAD_DIM, lp), (0, 4, 1, 2, 3))

    m_past, l_past, acc_past = (jnp.concatenate([st[j] for st in stats], axis=0) for j in range(3))
    attn_s = _attn_merge(qs, ks, vs, m_past, l_past, acc_past, bs, lam, sub_w, li)
    y_s, conv_s, ssm_s = _ssd(xbc_s, zs, dt_s, _ssm_params(p, 0), state_conv[0], state_ssm[0], bs, ls, F32)
    xs = _ffn(xs, nw[0][2], *ffn_w, (0, 1), mix=(attn_s, y_s, p['w_out'], 0))
    xs = _ffn(xs, nw[1][0], *ffn_w, (1, 0))
    xs, pool_s = _pool(xs, nw[1][1], state_pool[0], p['pool_w'], 0, p['pool_scale'][0], bs, ls, past_len)
    ys = _ffn(xs, nw[1][2], *ffn_w, (1, 1), final_w=p['final_norm_w'])

    lead = lambda t: t[None]
    return (yp.reshape(bp, lp, d), ys.reshape(bs, ls, d),
            lead(k_p.reshape(bp, lp, DIFF_HEADS, 2, DIFF_HEAD_DIM)), lead(v4.reshape(bp, lp, DIFF_HEADS, HEAD_LANES)),
            lead(conv_p), lead(ssm_p), lead(pool_p),
            lead(ks.reshape(bs, ls, DIFF_HEADS, 2, DIFF_HEAD_DIM)), lead(vs.reshape(bs, ls, DIFF_HEADS, HEAD_LANES)),
            lead(conv_s), lead(ssm_s), lead(pool_s))
```

```python
import functools
import math

import numpy as np
import jax
import jax.numpy as jnp
from jax import lax
from jax.experimental import pallas as pl
from jax.experimental.pallas import tpu as pltpu

F32 = jnp.float32
BF16 = jnp.bfloat16

RMS_EPS = 1e-6
ROPE_THETA = 10000.0
LOG2_E = 1.4426950408889634

DIFF_HEADS = 4
DIFF_HEAD_DIM = 64
HEAD_LANES = 2 * DIFF_HEAD_DIM
QK_WIDTH = DIFF_HEADS * HEAD_LANES
SSM_INNER = 512
SSM_HEAD_DIM = 64
SSM_HEADS = 8
SSM_GROUPS = 2
SSM_STATE = 128
GROUP_LANES = SSM_INNER // SSM_GROUPS
CONV_WIDTH = 4
CONV_CH = SSM_INNER + 2 * SSM_GROUPS * SSM_STATE
SSD_CHUNK = 128
SSD_CHUNKS_PER_STEP = 4
POOL_WINDOWS = (2, 4, 8, 16)
POOL_GROUP_CH = 256
POOL_HALO = 16
POOL_STATE_LEN = 15
MAIN_PROJ = 2 * QK_WIDTH + QK_WIDTH + SSM_INNER + CONV_CH

LANES = 128
SUBLANES = 8
MXU_DIM = 256
VMEM_LIMIT_BIG = 58 * 1024 * 1024
VMEM_LIMIT_SMALL = 40 * 1024 * 1024

FFN_CHUNK = MXU_DIM
TOKEN_TILE = 512
FFN_TOKEN_TILE = 1024
ATTN_Q_TILE = 512
ATTN_K_TILE = 256
ATTN_SUM_ROWS = 16
ATTN_LOOKAHEAD = 4


def _cparams(sem, vmem):
    return pltpu.CompilerParams(dimension_semantics=sem, vmem_limit_bytes=vmem)


def _tile(m, pref):
    t = min(m, pref)
    while m % t:
        t //= 2
    return t


def _rms(x, w):
    ms = jnp.mean(x * x, axis=-1, keepdims=True)
    return x * lax.rsqrt(ms + RMS_EPS) * w


def _silu(x):
    return x * jax.nn.sigmoid(x)


def _dot(a, b):
    return jnp.dot(a, b, preferred_element_type=F32)


def _dot_nt(a, b):
    return lax.dot_general(a, b, (((1,), (1,)), ((), ())), preferred_element_type=F32)


def _split3(a):
    hi = a.astype(BF16)
    r = a - hi.astype(F32)
    mid = r.astype(BF16)
    return hi, mid, (r - mid.astype(F32)).astype(BF16)


def _dot_exact(a, b, exact_lhs=False):
    if exact_lhs:
        a = a.astype(BF16)
        return sum(_dot(a, t) for t in _split3(b))
    b = b.astype(BF16)
    return sum(_dot(t, b) for t in _split3(a))


def _row_spec(tile, width):
    return pl.BlockSpec((tile, width), lambda i: (i, 0))


def _const_spec(shape):
    return pl.BlockSpec(shape, lambda *_: (0,) * len(shape))


def _rope_table_kernel(inv_ref, cos_ref, sin_ref, *, pos0, period):
    shape = cos_ref.shape
    row = lax.broadcasted_iota(jnp.int32, shape, 0)
    lane = lax.broadcasted_iota(jnp.int32, shape, 1)
    pos = pos0 + lax.rem(row, period)
    ang = pos.astype(F32) * inv_ref[...]
    cos_ref[...] = jnp.cos(ang)
    s = jnp.sin(ang)
    sin_ref[...] = jnp.where(lax.rem(lane, DIFF_HEAD_DIM) < DIFF_HEAD_DIM // 2, -s, s)


def _rope_tables(rows, pos0, period):
    half = DIFF_HEAD_DIM // 2
    inv = 1.0 / (ROPE_THETA ** (jnp.arange(0, DIFF_HEAD_DIM, 2, dtype=F32) / DIFF_HEAD_DIM))
    inv = jnp.tile(inv, LANES // half)[None, :]
    return pl.pallas_call(
        functools.partial(_rope_table_kernel, pos0=pos0, period=period),
        out_shape=(jax.ShapeDtypeStruct((rows, LANES), F32),) * 2,
        name="rope_table",
    )(inv)


def _ffn_kernel(*refs, n_chunks, has_mix, has_final):
    it = iter(refs)
    x_ref = next(it)
    if has_mix:
        a_ref, y_ref, wo_ref = next(it), next(it), next(it)
    nw_ref, wg_ref, wu_ref, wd_ref = next(it), next(it), next(it), next(it)
    fw_ref = next(it) if has_final else None
    o_ref = next(it)

    x = x_ref[...]
    if has_mix:
        half = a_ref.shape[1]
        x = x + _dot(a_ref[...].astype(BF16), wo_ref[:half, :]) + _dot(y_ref[...].astype(BF16), wo_ref[half:, :])
    u = _rms(x, nw_ref[...]).astype(BF16)
    acc = jnp.zeros(x.shape, F32)
    for c in range(n_chunks):
        sl = slice(c * FFN_CHUNK, (c + 1) * FFN_CHUNK)
        g = _dot(u, wg_ref[:, sl])
        up = _dot(u, wu_ref[:, sl])
        h = (_silu(g) * up).astype(BF16)
        acc = acc + _dot(h, wd_ref[sl, :])
    y = x + 0.5 * acc
    if has_final:
        y = _rms(y, fw_ref[...])
    o_ref[...] = y


def _stacked_spec(w, idx):
    n_lead = len(idx)
    return pl.BlockSpec((None,) * n_lead + w.shape[n_lead:], lambda *_: tuple(idx) + (0,) * (w.ndim - n_lead),
                        pipeline_mode=pl.Buffered(1))


def _ffn(x, nw, wg, wu, wd, w_idx, mix=None, final_w=None):
    m, d = x.shape
    hidden = wg.shape[-1]
    tm = _tile(m, FFN_TOKEN_TILE)
    args, specs = [x], [_row_spec(tm, d)]
    if mix is not None:
        a, y, wo, wo_idx = mix
        args += [a, y, wo]
        specs += [_row_spec(tm, a.shape[1]), _row_spec(tm, y.shape[1]), _stacked_spec(wo, (wo_idx,))]
    args += [nw, wg, wu, wd]
    specs += [_const_spec(nw.shape), _stacked_spec(wg, w_idx), _stacked_spec(wu, w_idx), _stacked_spec(wd, w_idx)]
    if final_w is not None:
        args.append(final_w)
        specs.append(_const_spec(final_w.shape))
    return pl.pallas_call(
        functools.partial(_ffn_kernel, n_chunks=hidden // FFN_CHUNK, has_mix=mix is not None,
                          has_final=final_w is not None),
        grid=(m // tm,),
        in_specs=specs,
        out_specs=_row_spec(tm, d),
        out_shape=jax.ShapeDtypeStruct((m, d), F32),
        compiler_params=_cparams(("parallel",), VMEM_LIMIT_BIG),
        name="ffn",
    )(*args)


def _block_diag_queries(q, t_new):
    qt = jnp.concatenate([q] * (2 * DIFF_HEADS), axis=0)
    row = lax.broadcasted_iota(jnp.int32, qt.shape, 0)
    col = lax.broadcasted_iota(jnp.int32, qt.shape, 1)
    same = lax.div(row, t_new) == lax.div(col, DIFF_HEAD_DIM)
    return jnp.where(same, qt, 0.0).astype(BF16)


def _softmax_update(scores, m_s, l_s):
    m = m_s[...]
    m_new = m
    for s in scores:
        m_new = jnp.maximum(m_new, jnp.max(s, axis=1, keepdims=True))
    alpha = jnp.exp(m - m_new)
    l = alpha * l_s[...]
    probs = []
    for s in scores:
        p = jnp.exp(s - m_new)
        l = l + jnp.sum(p, axis=1, keepdims=True)
        probs.append(p.astype(BF16))
    m_s[...] = m_new
    l_s[...] = l
    return alpha, probs


def _weighted_values(alpha, probs, values, acc_s, head_rows):
    acc = alpha * acc_s[...]
    for pb, v_heads in zip(probs, values):
        acc = acc + jnp.concatenate(
            [_dot(pb[h * head_rows:(h + 1) * head_rows], v_heads[h]) for h in range(DIFF_HEADS)], axis=0)
    acc_s[...] = acc


def _ffn_host_kernel(pt_ref, *refs, n_chunks, has_mix, has_final, n_pages, steps_per_seq, n_steps, seq0, layer):
    it = iter(refs)
    x_ref = next(it)
    if has_mix:
        a_ref, y_ref, wo_ref = next(it), next(it), next(it)
    nw_ref, wg_ref, wu_ref, wd_ref = next(it), next(it), next(it), next(it)
    fw_ref = next(it) if has_final else None
    q_ref, ck_ref, cv_ref = next(it), next(it), next(it)
    o_ref, m_out, l_out, acc_out = next(it), next(it), next(it), next(it)
    wq_s, m_s, l_s, acc_s, kbuf, vbuf, ksem, vsem = (next(it) for _ in range(8))
    step = pl.program_id(0)
    part = lax.rem(step, steps_per_seq)
    slot = lax.rem(step, 2)
    t_new = q_ref.shape[0]
    page = kbuf.shape[3]
    group = 2
    n_groups = n_pages // group

    def start_pages(of_step, into_slot, pages=None):
        seq = seq0 + of_step // steps_per_seq
        first = lax.rem(of_step, steps_per_seq) * n_pages
        for g in (range(n_pages) if pages is None else pages):
            phys = pt_ref[seq, first + g]
            pltpu.make_async_copy(ck_ref.at[layer, phys], kbuf.at[into_slot, g], ksem.at[into_slot]).start()
            pltpu.make_async_copy(cv_ref.at[layer, phys], vbuf.at[into_slot, g], vsem.at[into_slot]).start()

    def wait_pages(in_slot):
        pltpu.make_async_copy(ck_ref.at[layer, pl.ds(0, n_pages)], kbuf.at[in_slot], ksem.at[in_slot]).wait()
        pltpu.make_async_copy(cv_ref.at[layer, pl.ds(0, n_pages)], vbuf.at[in_slot], vsem.at[in_slot]).wait()

    @pl.when(step == 0)
    def _():
        start_pages(step, slot)

    wait_pages(slot)
    nxt = jnp.minimum(step + 1, n_steps - 1)

    @pl.when(part == 0)
    def _():
        wq_s[...] = _block_diag_queries(q_ref[...], t_new)
        m_s[...] = jnp.full(m_s.shape, -jnp.inf, F32)
        l_s[...] = jnp.zeros(l_s.shape, F32)
        acc_s[...] = jnp.zeros(acc_s.shape, F32)

    def page_scores(g):
        wq = wq_s[...]
        return [_dot(wq, kbuf[slot, p].astype(BF16)) for p in range(g * group, (g + 1) * group)]

    def page_values(g):
        return [[vbuf[slot, p, pl.ds(h, page, stride=DIFF_HEADS), :].astype(BF16) for h in range(DIFF_HEADS)]
                for p in range(g * group, (g + 1) * group)]

    x = x_ref[...]
    if has_mix:
        half = a_ref.shape[1]
        x = x + _dot(a_ref[...].astype(BF16), wo_ref[:half, :]) + _dot(y_ref[...].astype(BF16), wo_ref[half:, :])
    u = _rms(x, nw_ref[...]).astype(BF16)
    acc = jnp.zeros(x.shape, F32)
    scores = None
    for c in range(n_chunks):
        sl = slice(c * FFN_CHUNK, (c + 1) * FFN_CHUNK)
        g = _dot(u, wg_ref[:, sl])
        up = _dot(u, wu_ref[:, sl])
        h = (_silu(g) * up).astype(BF16)
        acc = acc + _dot(h, wd_ref[sl, :])
        if scores is not None:
            alpha, probs = _softmax_update(scores, m_s, l_s)
            _weighted_values(alpha, probs, page_values(c - 1), acc_s, 2 * t_new)
        scores = page_scores(c) if c < n_groups else None
        if c < n_groups:
            start_pages(nxt, 1 - slot, range(c * group, (c + 1) * group))
    assert scores is None
    y = x + 0.5 * acc
    if has_final:
        y = _rms(y, fw_ref[...])
    o_ref[...] = y

    @pl.when(part == steps_per_seq - 1)
    def _():
        m_out[...] = m_s[...]
        l_out[...] = l_s[...]
        acc_out[...] = acc_s[...]

    @pl.when(step == n_steps - 1)
    def _():
        wait_pages(1 - slot)


def _ffn_host(x, nw, wg, wu, wd, w_idx, paged, q, seq0, n_seq, mix=None, final_w=None):
    cache_k, cache_v, page_table, layer = paged
    m, d = x.shape
    hidden = wg.shape[-1]
    tm = _tile(m, TOKEN_TILE)
    n_steps = m // tm
    n_log = page_table.shape[1]
    steps_per_seq = n_steps // n_seq
    n_pages = n_log // steps_per_seq
    assert steps_per_seq * n_seq == n_steps and n_pages * steps_per_seq == n_log and n_pages % 2 == 0
    assert n_pages // 2 < hidden // FFN_CHUNK
    t_new = q.shape[0] // page_table.shape[0]
    n_rows = 2 * DIFF_HEADS * t_new
    n_layers, n_phys, page = cache_k.shape[:3]
    ck = jnp.transpose(cache_k, (0, 1, 3, 4, 5, 2)).reshape(n_layers, n_phys, QK_WIDTH, page)
    cv = cache_v.reshape(n_layers, n_phys, page * DIFF_HEADS, HEAD_LANES)

    rows = lambda w: pl.BlockSpec((tm, w), lambda i, pt: (i, 0))
    const = lambda a: pl.BlockSpec(a.shape, lambda i, pt: (0,) * a.ndim)

    def stacked(w, idx):
        n_lead = len(idx)
        return pl.BlockSpec((None,) * n_lead + w.shape[n_lead:], lambda i, pt: tuple(idx) + (0,) * (w.ndim - n_lead),
                            pipeline_mode=pl.Buffered(1))

    args, specs = [x], [rows(d)]
    if mix is not None:
        a, y, wo, wo_idx = mix
        args += [a, y, wo]
        specs += [rows(a.shape[1]), rows(y.shape[1]), stacked(wo, (wo_idx,))]
    args += [nw, wg, wu, wd]
    specs += [const(nw), stacked(wg, w_idx), stacked(wu, w_idx), stacked(wd, w_idx)]
    if final_w is not None:
        args.append(final_w)
        specs.append(const(final_w))
    args += [q, ck, cv]
    specs += [pl.BlockSpec((t_new, QK_WIDTH), lambda i, pt: (seq0 + i // steps_per_seq, 0)),
              pl.BlockSpec(memory_space=pl.ANY), pl.BlockSpec(memory_space=pl.ANY)]
    stat = lambda w: pl.BlockSpec((n_rows, w), lambda i, pt: (i // steps_per_seq, 0))
    grid_spec = pltpu.PrefetchScalarGridSpec(
        num_scalar_prefetch=1,
        grid=(n_steps,),
        in_specs=specs,
        out_specs=[rows(d), stat(1), stat(1), stat(HEAD_LANES)],
        scratch_shapes=[pltpu.VMEM((n_rows, QK_WIDTH), BF16), pltpu.VMEM((n_rows, 1), F32),
                        pltpu.VMEM((n_rows, 1), F32), pltpu.VMEM((n_rows, HEAD_LANES), F32),
                        pltpu.VMEM((2, n_pages, QK_WIDTH, page), F32),
                        pltpu.VMEM((2, n_pages, page * DIFF_HEADS, HEAD_LANES), F32),
                        pltpu.SemaphoreType.DMA((2,)), pltpu.SemaphoreType.DMA((2,))],
    )
    return pl.pallas_call(
        functools.partial(_ffn_host_kernel, n_chunks=hidden // FFN_CHUNK, has_mix=mix is not None,
                          has_final=final_w is not None, n_pages=n_pages, steps_per_seq=steps_per_seq,
                          n_steps=n_steps, seq0=seq0, layer=layer),
        grid_spec=grid_spec,
        out_shape=[jax.ShapeDtypeStruct((m, d), F32), jax.ShapeDtypeStruct((n_seq * n_rows, 1), F32),
                   jax.ShapeDtypeStruct((n_seq * n_rows, 1), F32), jax.ShapeDtypeStruct((n_seq * n_rows, HEAD_LANES), F32)],
        compiler_params=_cparams(("arbitrary",), VMEM_LIMIT_BIG),
        name="ffn_host",
    )(page_table, *args)


def _inproj_kernel(x_ref, nw_ref, w_ref, wdt_ref, cos_ref, sin_ref, *out_refs, prompt_layout):
    if prompt_layout:
        qt_ref, kt_ref, kb_ref, v4_ref, vt_ref, z_ref, xbc_ref, dt_ref = out_refs
    else:
        q_ref, k_ref, v_ref, z_ref, xbc_ref, dt_ref = out_refs
    u = _rms(x_ref[...], nw_ref[...]).astype(BF16)
    tm = u.shape[0]
    reps = QK_WIDTH // LANES
    cos = jnp.concatenate([cos_ref[...]] * reps, axis=1)
    sin = jnp.concatenate([sin_ref[...]] * reps, axis=1)
    lane = lax.broadcasted_iota(jnp.int32, (tm, QK_WIDTH), 1)
    low_half = lax.rem(lane, DIFF_HEAD_DIM) < DIFF_HEAD_DIM // 2
    half = DIFF_HEAD_DIM // 2

    def rope(t):
        up = pltpu.roll(t, QK_WIDTH - half, 1)
        down = pltpu.roll(t, half, 1)
        return t * cos + jnp.where(low_half, up, down) * sin

    q_scale = DIFF_HEAD_DIM ** -0.5 * (LOG2_E if prompt_layout else 1.0)
    q = rope(_dot(u, w_ref[:, 0:QK_WIDTH])) * q_scale
    k = rope(_dot(u, w_ref[:, QK_WIDTH:2 * QK_WIDTH]))
    v = _dot(u, w_ref[:, 2 * QK_WIDTH:3 * QK_WIDTH])
    if prompt_layout:
        def head_blocks(t_ref, t):
            tb = t_ref.shape[-1]
            for h in range(DIFF_HEADS):
                for jb in range(tm // tb):
                    t_ref[h, jb] = t[h * HEAD_LANES:(h + 1) * HEAD_LANES, jb * tb:(jb + 1) * tb].astype(BF16)

        head_blocks(qt_ref, q.T)
        kt_ref[...] = k.T
        kb_ref[...] = k.astype(BF16)
        for h in range(DIFF_HEADS):
            v4_ref[pl.ds(h, tm, stride=DIFF_HEADS), :] = v[:, h * HEAD_LANES:(h + 1) * HEAD_LANES]
        head_blocks(vt_ref, v.T)
    else:
        q_ref[...] = q
        k_ref[...] = k
        v_ref[...] = v
    z0 = 3 * QK_WIDTH
    z_ref[...] = _dot(u, w_ref[:, z0:z0 + SSM_INNER])
    xbc_ref[...] = _dot(u, w_ref[:, z0 + SSM_INNER:z0 + SSM_INNER + CONV_CH])
    dt_ref[...] = _dot(u, wdt_ref[...])


def _inproj(x, nw, w_in, w_idx, w_dt, cos, sin, prompt_layout):
    m, d = x.shape
    period = cos.shape[0]
    tm = _tile(period, TOKEN_TILE)
    n_per = period // tm
    tab_spec = pl.BlockSpec((tm, LANES), lambda i: (i % n_per, 0))
    rows = lambda w, dt: (_row_spec(tm, w), jax.ShapeDtypeStruct((m, w), dt))
    tail = [rows(SSM_INNER, F32), rows(CONV_CH, F32), rows(LANES, F32)]
    if prompt_layout:
        n_seq = m // period

        def blocked(tile):
            ta = _tile(tm, tile)
            return (pl.BlockSpec((None, DIFF_HEADS, tm // ta, HEAD_LANES, ta), lambda i: (i // n_per, 0, i % n_per, 0, 0)),
                    jax.ShapeDtypeStruct((n_seq, DIFF_HEADS, period // ta, HEAD_LANES, ta), BF16))

        kt_out = (pl.BlockSpec((None, QK_WIDTH, tm), lambda i: (i // n_per, 0, i % n_per)),
                  jax.ShapeDtypeStruct((n_seq, QK_WIDTH, period), F32))
        v4_out = (_row_spec(tm * DIFF_HEADS, HEAD_LANES), jax.ShapeDtypeStruct((m * DIFF_HEADS, HEAD_LANES), F32))
        outs = [blocked(ATTN_Q_TILE), kt_out, rows(QK_WIDTH, BF16), v4_out, blocked(ATTN_K_TILE)] + tail
    else:
        outs = [rows(QK_WIDTH, F32)] * 3 + tail
    return pl.pallas_call(
        functools.partial(_inproj_kernel, prompt_layout=prompt_layout),
        grid=(m // tm,),
        in_specs=[_row_spec(tm, d), _const_spec(nw.shape), pl.BlockSpec((None,) + w_in.shape[1:], lambda i: (w_idx, 0, 0)),
                  _const_spec(w_dt.shape), tab_spec, tab_spec],
        out_specs=[spec for spec, _ in outs],
        out_shape=[shape for _, shape in outs],
        compiler_params=_cparams(("parallel",), VMEM_LIMIT_SMALL),
        name="inproj",
    )(x, nw, w_in, w_dt, cos, sin)


def _lambda(lam_ref, lambda_init):
    v = lam_ref[...]
    d1 = jnp.sum(v[0:1] * v[1:2], axis=1, keepdims=True)
    d2 = jnp.sum(v[2:3] * v[3:4], axis=1, keepdims=True)
    return jnp.exp(d1) - jnp.exp(d2) + lambda_init


def _subln(d, w, lambda_init):
    return _rms(d, w) * (1.0 - lambda_init)


def _attn_prompt_kernel(lam_ref, sub_ref, qt_ref, k_ref, vt_ref, o_ref, m_s, acc_s, *, lambda_init):
    n_qb, _, tqb = qt_ref.shape
    n_kb, _, tk = vt_ref.shape
    strip = tk
    n_strips = n_qb * tqb // strip
    sub = lax.broadcasted_iota(jnp.int32, (HEAD_LANES, strip), 0)
    first_comp = sub < DIFF_HEAD_DIM
    krow = lax.broadcasted_iota(jnp.int32, (tk, strip), 0)
    qcol = lax.broadcasted_iota(jnp.int32, (tk, strip), 1)
    causal = krow <= qcol
    zero = jnp.zeros((HEAD_LANES, strip), BF16)

    def cols(qs, comp):
        return slice((comp * n_strips + qs) * strip, (comp * n_strips + qs + 1) * strip)

    units = [(kb, qs, comp) for kb in range(n_kb) for qs in range(kb, n_strips) for comp in range(2)]

    def scores(unit):
        kb, qs, comp = unit
        qb, off = divmod(qs * strip, tqb)
        qt = qt_ref[qb, :, off:off + strip]
        q = jnp.where(first_comp, qt, zero) if comp == 0 else jnp.where(first_comp, zero, qt)
        return _dot(k_ref[kb * tk:(kb + 1) * tk, :], q)

    pending = [scores(u) for u in units[:ATTN_LOOKAHEAD]]
    ones = jnp.ones((ATTN_SUM_ROWS, tk), BF16)
    for i, (kb, qs, comp) in enumerate(units):
        if i + ATTN_LOOKAHEAD < len(units):
            pending.append(scores(units[i + ATTN_LOOKAHEAD]))
        s = pending.pop(0)
        c = cols(qs, comp)
        if comp == 0 and qs == kb:
            vt = jnp.concatenate([vt_ref[kb], ones], axis=0)
        if kb == qs:
            s = jnp.where(causal, s, -jnp.inf)
        if kb == 0:
            m = jnp.max(s, axis=0, keepdims=True)
            acc_s[:, c] = _dot(vt, jnp.exp2(s - m).astype(BF16))
        else:
            m_old = m_s[:, c]
            m = jnp.maximum(m_old, jnp.max(s, axis=0, keepdims=True))
            acc_s[:, c] = jnp.exp2(m_old - m) * acc_s[:, c] + _dot(vt, jnp.exp2(s - m).astype(BF16))
        m_s[:, c] = m
    lam = _lambda(lam_ref, lambda_init)
    for qs in range(n_strips):
        o0, o1 = (acc_s[:HEAD_LANES, cols(qs, comp)] * (1.0 / acc_s[HEAD_LANES:HEAD_LANES + 1, cols(qs, comp)])
                  for comp in range(2))
        d = (o0 - lam * o1).T
        o_ref[qs * strip:(qs + 1) * strip, :] = _subln(d, sub_ref[...], lambda_init).astype(o_ref.dtype)


def _attn_prompt(qt, kb, vt, lam_vecs, sub_w, lambda_init):
    batch, _, nq, _, tq = qt.shape
    nk, tk = vt.shape[2], vt.shape[4]
    seq = nq * tq
    q_spec = pl.BlockSpec((None, None, nq, HEAD_LANES, tq), lambda b, h: (b, h, 0, 0, 0))
    k_spec = pl.BlockSpec((seq, HEAD_LANES), lambda b, h: (b, h))
    v_spec = pl.BlockSpec((None, None, nk, HEAD_LANES, tk), lambda b, h: (b, h, 0, 0, 0))
    return pl.pallas_call(
        functools.partial(_attn_prompt_kernel, lambda_init=lambda_init),
        grid=(batch, DIFF_HEADS),
        in_specs=[pl.BlockSpec(lam_vecs.shape, lambda b, h: (0, 0)), pl.BlockSpec(sub_w.shape, lambda b, h: (0, 0)),
                  q_spec, k_spec, v_spec],
        out_specs=k_spec,
        out_shape=jax.ShapeDtypeStruct(kb.shape, BF16),
        scratch_shapes=[pltpu.VMEM((1, 2 * seq), F32), pltpu.VMEM((HEAD_LANES + ATTN_SUM_ROWS, 2 * seq), F32)],
        compiler_params=_cparams(("parallel", "parallel"), VMEM_LIMIT_SMALL),
        name="attn_prompt",
    )(lam_vecs, sub_w, qt, kb, vt)


def _attn_merge_kernel(lam_ref, sub_ref, q_ref, kn_ref, vn_ref, m_ref, l_ref, acc_ref, o_ref, m_s, l_s, acc_s,
                       *, lambda_init):
    t_new = q_ref.shape[0]
    head_rows = 2 * t_new
    page = LANES
    wq = _block_diag_queries(q_ref[...], t_new)
    pad = jnp.zeros((page - t_new, QK_WIDTH), F32)
    kn = jnp.concatenate([kn_ref[...], pad], axis=0).astype(BF16)
    vn = jnp.concatenate([vn_ref[...], pad], axis=0).astype(BF16)
    s = _dot_nt(wq, kn)
    row = lax.broadcasted_iota(jnp.int32, s.shape, 0)
    col = lax.broadcasted_iota(jnp.int32, s.shape, 1)
    s = jnp.where(col <= lax.rem(row, t_new), s, -jnp.inf)
    m_s[...] = m_ref[...]
    l_s[...] = l_ref[...]
    acc_s[...] = acc_ref[...]
    alpha, probs = _softmax_update([s], m_s, l_s)
    _weighted_values(alpha, probs, [[vn[:, h * HEAD_LANES:(h + 1) * HEAD_LANES] for h in range(DIFF_HEADS)]],
                     acc_s, head_rows)
    o = acc_s[...] / l_s[...]
    lam = _lambda(lam_ref, lambda_init)
    for h in range(DIFF_HEADS):
        r0 = h * head_rows
        d = o[r0:r0 + t_new] - lam * o[r0 + t_new:r0 + head_rows]
        o_ref[:, h * HEAD_LANES:(h + 1) * HEAD_LANES] = _subln(d, sub_ref[...], lambda_init)


def _attn_merge(q, k_new, v_new, m_past, l_past, acc_past, batch, lam_vecs, sub_w, lambda_init):
    m = q.shape[0]
    t_new = m // batch
    n_rows = 2 * DIFF_HEADS * t_new
    tok = pl.BlockSpec((t_new, QK_WIDTH), lambda b: (b, 0))
    stat = lambda w: pl.BlockSpec((n_rows, w), lambda b: (b, 0))
    return pl.pallas_call(
        functools.partial(_attn_merge_kernel, lambda_init=lambda_init),
        grid=(batch,),
        in_specs=[_const_spec(lam_vecs.shape), _const_spec(sub_w.shape), tok, tok, tok, stat(1), stat(1), stat(HEAD_LANES)],
        out_specs=tok,
        out_shape=jax.ShapeDtypeStruct((m, QK_WIDTH), F32),
        scratch_shapes=[pltpu.VMEM((n_rows, 1), F32), pltpu.VMEM((n_rows, 1), F32), pltpu.VMEM((n_rows, HEAD_LANES), F32)],
        compiler_params=_cparams(("parallel",), VMEM_LIMIT_SMALL),
        name="attn_merge",
    )(lam_vecs, sub_w, q, k_new, v_new, m_past, l_past, acc_past)


def _ssd_kernel(xbc_ref, z_ref, dt_ref, cw_ref, cb_ref, dtb_ref, alog_ref, dsk_ref, nw_ref, exp_ref,
                cinit_ref, h0_ref, y_ref, cout_ref, hout_ref, ext_s, st_s, *, t_in, n_chunks):
    c = pl.program_id(1)
    T = SSD_CHUNK
    tail = CONV_WIDTH - 1

    @pl.when(c == 0)
    def _():
        ext_s[0:SUBLANES, :] = jnp.zeros((SUBLANES, CONV_CH), F32)
        ext_s[SUBLANES - tail:SUBLANES, :] = cinit_ref[...]
        st_s[...] = h0_ref[...].reshape(SSM_INNER, SSM_STATE).T

    u = xbc_ref[...]
    ext_s[SUBLANES:SUBLANES + t_in, :] = u
    cw = cw_ref[...]
    conv = cb_ref[...] + cw[tail:tail + 1] * u
    for j in range(tail):
        conv = conv + cw[j:j + 1] * ext_s[SUBLANES - tail + j:SUBLANES - tail + j + t_in, :]
    ext_s[0:SUBLANES, :] = ext_s[t_in:t_in + SUBLANES, :]
    xc_all = _silu(conv)
    dt_all = jax.nn.softplus(dt_ref[...] + dtb_ref[...])
    z_all = z_ref[...]
    if t_in < T:
        xc_all = jnp.concatenate([xc_all, jnp.zeros((T - t_in, CONV_CH), F32)], axis=0)
        dt_all = jnp.concatenate([dt_all, jnp.zeros((T - t_in, LANES), F32)], axis=0)
        z_all = jnp.concatenate([z_all, jnp.zeros((T - t_in, SSM_INNER), F32)], axis=0)

    a = -jnp.exp(alog_ref[...])
    row = lax.broadcasted_iota(jnp.int32, (T, T), 0)
    col = lax.broadcasted_iota(jnp.int32, (T, T), 1)
    causal = row >= col
    tri = causal.astype(F32)
    expand = exp_ref[...]
    lane = lax.broadcasted_iota(jnp.int32, (T, LANES), 1)
    heads_per_group = SSM_HEADS // SSM_GROUPS
    nw = nw_ref[...]
    for ci in range(max(1, t_in // T)):
        rows = slice(ci * T, (ci + 1) * T)
        xc, dt, z = xc_all[rows], dt_all[rows], z_all[rows]
        acs = _dot_exact(tri, dt * a, exact_lhs=True)
        acs_t = acs.T
        both_x = _dot_exact(jnp.concatenate([dt, acs], axis=0), expand)
        dt_x, acs_x = both_x[:T], both_x[T:]
        last = acs_x[T - 1:T, :]
        decay_out = jnp.exp(acs_x)
        decay_end = jnp.exp(last - acs_x)
        chunk_decay = jnp.exp(last)

        xs = xc[:, :SSM_INNER]
        xdt = xs * dt_x
        xdt_b = xdt.astype(BF16)
        xw_b = (xdt * decay_end).astype(BF16)
        state = st_s[...]
        state_b = state.astype(BF16)
        ys = []
        for g in range(SSM_GROUPS):
            b0 = SSM_INNER + g * SSM_STATE
            c0 = SSM_INNER + SSM_GROUPS * SSM_STATE + g * SSM_STATE
            bg_t = xc[:, b0:b0 + SSM_STATE].T.astype(BF16)
            cg = xc[:, c0:c0 + SSM_STATE].astype(BF16)
            gl = slice(g * GROUP_LANES, (g + 1) * GROUP_LANES)
            cb = _dot(cg, bg_t)
            y_off = _dot(cg, state_b[:, gl]) * decay_out[:, gl]
            st_s[:, gl] = state[:, gl] * chunk_decay[:, gl] + _dot(bg_t, xw_b[:, gl])
            for pair in range(heads_per_group // 2):
                h0 = g * heads_per_group + 2 * pair
                xp = xdt_b[:, h0 * SSM_HEAD_DIM:(h0 + 2) * SSM_HEAD_DIM]
                outs = []
                for hh in (h0, h0 + 1):
                    seg = acs[:, hh:hh + 1] - acs_t[hh:hh + 1, :]
                    w = cb * jnp.exp(jnp.where(causal, seg, -jnp.inf))
                    outs.append(_dot(w.astype(BF16), xp))
                ys.append(jnp.where(lane < SSM_HEAD_DIM, outs[0], outs[1])
                          + y_off[:, 2 * pair * SSM_HEAD_DIM:(2 * pair + 2) * SSM_HEAD_DIM])
        y = jnp.concatenate(ys, axis=1) + dsk_ref[...] * xs
        y = y * _silu(z)
        parts = []
        for g in range(SSM_GROUPS):
            gl = slice(g * GROUP_LANES, (g + 1) * GROUP_LANES)
            parts.append(_rms(y[:, gl], nw[:, gl]))
        y = jnp.concatenate(parts, axis=1).astype(y_ref.dtype)
        if t_in < T:
            y_ref[...] = y[:t_in]
        else:
            y_ref[rows, :] = y

    @pl.when(c == n_chunks - 1)
    def _():
        cout_ref[...] = ext_s[SUBLANES - tail:SUBLANES, :]
        hout_ref[...] = st_s[...].T.reshape(SSM_HEADS, SSM_HEAD_DIM, SSM_STATE)


def _ssd(xbc, z, dt_raw, p, conv_init, h0, batch, seq, y_dtype):
    t_in = _tile(seq, SSD_CHUNK * SSD_CHUNKS_PER_STEP)
    n_chunks = seq // t_in
    assert t_in % SSD_CHUNK == 0 or n_chunks == 1
    tail = CONV_WIDTH - 1
    rows = lambda w: pl.BlockSpec((t_in, w), lambda b, c: (b * n_chunks + c, 0))
    per_b3 = lambda s: pl.BlockSpec((None,) + s, lambda b, c: (b,) + (0,) * len(s))
    consts = [p['conv_w'], p['conv_b'], p['dt_bias'], p['a_log'], p['d_skip'], p['ssm_norm_w'], p['expand']]
    return pl.pallas_call(
        functools.partial(_ssd_kernel, t_in=t_in, n_chunks=n_chunks),
        grid=(batch, n_chunks),
        in_specs=[rows(CONV_CH), rows(SSM_INNER), rows(LANES)]
                 + [pl.BlockSpec(a.shape, lambda b, c: (0, 0)) for a in consts]
                 + [per_b3((tail, CONV_CH)), per_b3((SSM_HEADS, SSM_HEAD_DIM, SSM_STATE))],
        out_specs=[rows(SSM_INNER), per_b3((tail, CONV_CH)), per_b3((SSM_HEADS, SSM_HEAD_DIM, SSM_STATE))],
        out_shape=[jax.ShapeDtypeStruct((batch * seq, SSM_INNER), y_dtype),
                   jax.ShapeDtypeStruct((batch, tail, CONV_CH), F32),
                   jax.ShapeDtypeStruct((batch, SSM_HEADS, SSM_HEAD_DIM, SSM_STATE), F32)],
        scratch_shapes=[pltpu.VMEM((SUBLANES + t_in, CONV_CH), F32), pltpu.VMEM((SSM_STATE, SSM_INNER), F32)],
        compiler_params=_cparams(("parallel", "arbitrary"), VMEM_LIMIT_SMALL),
        name="ssd",
    )(xbc, z, dt_raw, *consts, conv_init, h0)


def _pool_kernel(x_ref, nw_ref, init_ref, pw_ref, ps_ref, o_ref, pout_ref, ext_s, *, tp, n_tiles, pos0):
    t = pl.program_id(1)
    halo = POOL_HALO

    @pl.when(t == 0)
    def _():
        ext_s[0:halo, :] = jnp.zeros((halo, ext_s.shape[1]), F32)
        ext_s[halo - POOL_STATE_LEN:halo, :] = init_ref[...]

    x = x_ref[...]
    u = _rms(x, nw_ref[...])
    ext_s[halo:halo + tp, :] = u
    e = ext_s[...]
    gc = POOL_GROUP_CH
    sums = [e]
    for lvl in range(len(POOL_WINDOWS)):
        prev = sums[-1][:, gc:] if lvl else sums[-1]
        sums.append(prev + pltpu.roll(prev, 1 << lvl, 0))
    pos = pos0 + t * tp + lax.broadcasted_iota(jnp.int32, (tp, 1), 0)
    mixed = []
    for g, w in enumerate(POOL_WINDOWS):
        cnt = jnp.minimum(w, pos + 1).astype(F32)
        win = sums[g + 1][halo:, :gc]
        pooled = win / cnt - u[:, g * gc:(g + 1) * gc]
        mixed.append(_dot(pooled.astype(BF16), pw_ref[g]))
    o_ref[...] = x + jnp.concatenate(mixed, axis=1) * ps_ref[...]

    @pl.when(t == n_tiles - 1)
    def _():
        pout_ref[...] = ext_s[tp + halo - POOL_STATE_LEN:tp + halo, :]

    ext_s[0:halo, :] = ext_s[tp:tp + halo, :]


def _pool(x, nw, pool_init, pool_w, w_idx, pool_scale, batch, seq, pos0):
    d = x.shape[1]
    tp = _tile(seq, TOKEN_TILE)
    n_tiles = seq // tp
    rows = pl.BlockSpec((tp, d), lambda b, t: (b * n_tiles + t, 0))
    state = pl.BlockSpec((None, POOL_STATE_LEN, d), lambda b, t: (b, 0, 0))
    return pl.pallas_call(
        functools.partial(_pool_kernel, tp=tp, n_tiles=n_tiles, pos0=pos0),
        grid=(batch, n_tiles),
        in_specs=[rows, pl.BlockSpec(nw.shape, lambda b, t: (0, 0)), state,
                  _stacked_spec(pool_w, (w_idx,)), pl.BlockSpec(pool_scale.shape, lambda b, t: (0, 0))],
        out_specs=[rows, state],
        out_shape=[jax.ShapeDtypeStruct(x.shape, F32), jax.ShapeDtypeStruct((batch, POOL_STATE_LEN, d), F32)],
        scratch_shapes=[pltpu.VMEM((POOL_HALO + tp, d), F32)],
        compiler_params=_cparams(("parallel", "arbitrary"), VMEM_LIMIT_SMALL),
        name="pool",
    )(x, nw, pool_init, pool_w, pool_scale)


def _prep_params(norm_w, ffn_w_gate, ffn_w_up, ffn_w_down, ab_w_in, ab_w_out, ab_conv_w, ab_conv_b, ab_dt_bias,
                 ab_a_log, ab_d_skip, ab_ssm_norm_w, ab_lambda_q1, ab_lambda_k1, ab_lambda_q2, ab_lambda_k2,
                 ab_subln_w, pool_w, pool_scale, final_norm_w):
    def lane_pad(v):
        return jnp.pad(v, ((0, 0), (0, LANES - v.shape[1])))[:, None, :]

    expand = np.zeros((LANES, SSM_INNER), np.float32)
    for h in range(SSM_HEADS):
        expand[h, h * SSM_HEAD_DIM:(h + 1) * SSM_HEAD_DIM] = 1.0
    return dict(
        norm_w=norm_w[:, :, None, :],
        wg=ffn_w_gate.astype(BF16), wu=ffn_w_up.astype(BF16), wd=ffn_w_down.astype(BF16),
        w_in=ab_w_in.astype(BF16),
        w_dt=jnp.pad(ab_w_in[:, :, MAIN_PROJ:], ((0, 0), (0, 0), (0, LANES - SSM_HEADS))).astype(BF16),
        w_out=ab_w_out.astype(BF16),
        conv_w=ab_conv_w, conv_b=ab_conv_b[:, None, :],
        dt_bias=lane_pad(ab_dt_bias), a_log=lane_pad(ab_a_log),
        d_skip=jnp.repeat(ab_d_skip, SSM_HEAD_DIM, axis=1)[:, None, :],
        ssm_norm_w=ab_ssm_norm_w[:, None, :],
        lam=jnp.stack([ab_lambda_q1, ab_lambda_k1, ab_lambda_q2, ab_lambda_k2], axis=1),
        subln_w=ab_subln_w[:, None, :],
        pool_w=pool_w.astype(BF16), pool_scale=pool_scale[:, None, :],
        final_norm_w=final_norm_w[None, :],
        expand=jnp.asarray(expand),
    )


def _ssm_params(p, i):
    return dict(conv_w=p['conv_w'][i], conv_b=p['conv_b'][i], dt_bias=p['dt_bias'][i], a_log=p['a_log'][i],
                d_skip=p['d_skip'][i], ssm_norm_w=p['ssm_norm_w'][i], expand=p['expand'])


def _lambda_init(layer):
    return 0.8 - 0.6 * math.exp(-0.3 * layer)


def kernel(x_prompt, x_sample, cache_k, cache_v, state_conv, state_ssm, state_pool, page_table, norm_w, ffn_w_gate, ffn_w_up, ffn_w_down, ab_w_in, ab_w_out, ab_conv_w, ab_conv_b, ab_dt_bias, ab_a_log, ab_d_skip, ab_ssm_norm_w, ab_lambda_q1, ab_lambda_k1, ab_lambda_q2, ab_lambda_k2, ab_subln_w, pool_w, pool_scale, final_norm_w):
    p = _prep_params(norm_w, ffn_w_gate, ffn_w_up, ffn_w_down, ab_w_in, ab_w_out, ab_conv_w, ab_conv_b, ab_dt_bias,
                     ab_a_log, ab_d_skip, ab_ssm_norm_w, ab_lambda_q1, ab_lambda_k1, ab_lambda_q2, ab_lambda_k2,
                     ab_subln_w, pool_w, pool_scale, final_norm_w)
    depth = ffn_w_gate.shape[0]
    n_ab, n_pool = state_conv.shape[0], state_pool.shape[0]
    assert n_ab == 1 and depth == 2
    bp, lp, d = x_prompt.shape
    bs, ls, _ = x_sample.shape
    ffn_w = (p['wg'], p['wu'], p['wd'])
    nw = p['norm_w']
    lam, sub_w, li = p['lam'][0], p['subln_w'][0], _lambda_init(0)
    past_len = page_table.shape[1] * cache_k.shape[2]
    n_hosts = 2 * depth
    host_seqs = bs // n_hosts
    assert host_seqs * n_hosts == bs
    paged = (cache_k, cache_v, page_table, 0)

    xs = x_sample.reshape(bs * ls, d)
    cos_s, sin_s = _rope_tables(bs * ls, past_len, ls)
    xs = _ffn(xs, nw[0][0], *ffn_w, (0, 0))
    qs, ks, vs, zs, xbc_s, dt_s = _inproj(xs, nw[0][1], p['w_in'], 0, p['w_dt'][0], cos_s, sin_s, prompt_layout=False)
    stats = []

    def host(x, norm, w_idx, **kw):
        x, *st = _ffn_host(x, norm, *ffn_w, w_idx, paged, qs, len(stats) * host_seqs, host_seqs, **kw)
        stats.append(st)
        return x

    zeros = lambda shape: jnp.zeros(shape, F32)
    xp = x_prompt.reshape(bp * lp, d)
    cos_p, sin_p = _rope_tables(lp, 0, lp)
    xp = host(xp, nw[0][0], (0, 0))
    qt, kt, kb, v4, vt, zp, xbc_p, dt_p = _inproj(xp, nw[0][1], p['w_in'], 0, p['w_dt'][0], cos_p, sin_p,
                                                  prompt_layout=True)
    attn_p = _attn_prompt(qt, kb, vt, lam, sub_w, li)
    y_p, conv_p, ssm_p = _ssd(xbc_p, zp, dt_p, _ssm_params(p, 0), zeros((bp, CONV_WIDTH - 1, CONV_CH)),
                              zeros((bp, SSM_HEADS, SSM_HEAD_DIM, SSM_STATE)), bp, lp, BF16)
    xp = host(xp, nw[0][2], (0, 1), mix=(attn_p, y_p, p['w_out'], 0))
    xp = host(xp, nw[1][0], (1, 0))
    xp, pool_p = _pool(xp, nw[1][1], zeros((bp, POOL_STATE_LEN, d)), p['pool_w'], 0, p['pool_scale'][0], bp, lp, 0)
    yp = host(xp, nw[1][2], (1, 1), final_w=p['final_norm_w'])
    k_p = jnp.transpose(kt.reshape(bp, DIFF_HEADS, 2, DIFF_HEAD_DIM, lp), (0, 4, 1, 2, 3))

    m_past, l_past, acc_past = (jnp.concatenate([st[j] for st in stats], axis=0) for j in range(3))
    attn_s = _attn_merge(qs, ks, vs, m_past, l_past, acc_past, bs, lam, sub_w, li)
    y_s, conv_s, ssm_s = _ssd(xbc_s, zs, dt_s, _ssm_params(p, 0), state_conv[0], state_ssm[0], bs, ls, F32)
    xs = _ffn(xs, nw[0][2], *ffn_w, (0, 1), mix=(attn_s, y_s, p['w_out'], 0))
    xs = _ffn(xs, nw[1][0], *ffn_w, (1, 0))
    xs, pool_s = _pool(xs, nw[1][1], state_pool[0], p['pool_w'], 0, p['pool_scale'][0], bs, ls, past_len)
    ys = _ffn(xs, nw[1][2], *ffn_w, (1, 1), final_w=p['final_norm_w'])

    lead = lambda t: t[None]
    return (yp.reshape(bp, lp, d), ys.reshape(bs, ls, d),
            lead(k_p.reshape(bp, lp, DIFF_HEADS, 2, DIFF_HEAD_DIM)), lead(v4.reshape(bp, lp, DIFF_HEADS, HEAD_LANES)),
            lead(conv_p), lead(ssm_p), lead(pool_p),
            lead(ks.reshape(bs, ls, DIFF_HEADS, 2, DIFF_HEAD_DIM)), lead(vs.reshape(bs, ls, DIFF_HEADS, HEAD_LANES)),
            lead(conv_s), lead(ssm_s), lead(pool_s))
```

```python
import functools
import math

import numpy as np
import jax
import jax.numpy as jnp
from jax import lax
from jax.experimental import pallas as pl
from jax.experimental.pallas import tpu as pltpu

F32 = jnp.float32
BF16 = jnp.bfloat16

RMS_EPS = 1e-6
ROPE_THETA = 10000.0
LOG2_E = 1.4426950408889634

DIFF_HEADS = 4
DIFF_HEAD_DIM = 64
HEAD_LANES = 2 * DIFF_HEAD_DIM
QK_WIDTH = DIFF_HEADS * HEAD_LANES
SSM_INNER = 512
SSM_HEAD_DIM = 64
SSM_HEADS = 8
SSM_GROUPS = 2
SSM_STATE = 128
GROUP_LANES = SSM_INNER // SSM_GROUPS
CONV_WIDTH = 4
CONV_CH = SSM_INNER + 2 * SSM_GROUPS * SSM_STATE
SSD_CHUNK = 128
SSD_CHUNKS_PER_STEP = 8
POOL_WINDOWS = (2, 4, 8, 16)
POOL_GROUP_CH = 256
POOL_HALO = 16
POOL_STATE_LEN = 15
MAIN_PROJ = 2 * QK_WIDTH + QK_WIDTH + SSM_INNER + CONV_CH

LANES = 128
SUBLANES = 8
MXU_DIM = 256
VMEM_LIMIT_BIG = 58 * 1024 * 1024
VMEM_LIMIT_SMALL = 40 * 1024 * 1024

FFN_CHUNK = MXU_DIM
TOKEN_TILE = 512
FFN_TOKEN_TILE = 1024
ATTN_Q_TILE = 512
ATTN_K_TILE = 256
ATTN_SUM_ROWS = 16
ATTN_LOOKAHEAD = 5


def _cparams(sem, vmem):
    return pltpu.CompilerParams(dimension_semantics=sem, vmem_limit_bytes=vmem)


def _tile(m, pref):
    t = min(m, pref)
    while m % t:
        t //= 2
    return t


def _rms(x, w):
    ms = jnp.mean(x * x, axis=-1, keepdims=True)
    return x * lax.rsqrt(ms + RMS_EPS) * w


def _silu(x):
    return x * jax.nn.sigmoid(x)


def _dot(a, b):
    return jnp.dot(a, b, preferred_element_type=F32)


def _dot_nt(a, b):
    return lax.dot_general(a, b, (((1,), (1,)), ((), ())), preferred_element_type=F32)


def _split3(a):
    hi = a.astype(BF16)
    r = a - hi.astype(F32)
    mid = r.astype(BF16)
    return hi, mid, (r - mid.astype(F32)).astype(BF16)


def _dot_exact(a, b, exact_lhs=False):
    if exact_lhs:
        a = a.astype(BF16)
        return sum(_dot(a, t) for t in _split3(b))
    b = b.astype(BF16)
    return sum(_dot(t, b) for t in _split3(a))


def _row_spec(tile, width):
    return pl.BlockSpec((tile, width), lambda i: (i, 0))


def _const_spec(shape):
    return pl.BlockSpec(shape, lambda *_: (0,) * len(shape))


def _rope_table_kernel(inv_ref, cos_ref, sin_ref, *, pos0, period):
    shape = cos_ref.shape
    row = lax.broadcasted_iota(jnp.int32, shape, 0)
    lane = lax.broadcasted_iota(jnp.int32, shape, 1)
    pos = pos0 + lax.rem(row, period)
    ang = pos.astype(F32) * inv_ref[...]
    cos_ref[...] = jnp.cos(ang)
    s = jnp.sin(ang)
    sin_ref[...] = jnp.where(lax.rem(lane, DIFF_HEAD_DIM) < DIFF_HEAD_DIM // 2, -s, s)


def _rope_tables(rows, pos0, period):
    half = DIFF_HEAD_DIM // 2
    inv = 1.0 / (ROPE_THETA ** (jnp.arange(0, DIFF_HEAD_DIM, 2, dtype=F32) / DIFF_HEAD_DIM))
    inv = jnp.tile(inv, LANES // half)[None, :]
    return pl.pallas_call(
        functools.partial(_rope_table_kernel, pos0=pos0, period=period),
        out_shape=(jax.ShapeDtypeStruct((rows, LANES), F32),) * 2,
        name="rope_table",
    )(inv)


def _ffn_kernel(*refs, n_chunks, has_mix, has_final):
    it = iter(refs)
    x_ref = next(it)
    if has_mix:
        a_ref, y_ref, wo_ref = next(it), next(it), next(it)
    nw_ref, wg_ref, wu_ref, wd_ref = next(it), next(it), next(it), next(it)
    fw_ref = next(it) if has_final else None
    o_ref = next(it)

    x = x_ref[...]
    if has_mix:
        half = a_ref.shape[1]
        x = x + _dot(a_ref[...].astype(BF16), wo_ref[:half, :]) + _dot(y_ref[...].astype(BF16), wo_ref[half:, :])
    u = _rms(x, nw_ref[...]).astype(BF16)
    acc = jnp.zeros(x.shape, F32)
    for c in range(n_chunks):
        sl = slice(c * FFN_CHUNK, (c + 1) * FFN_CHUNK)
        g = _dot(u, wg_ref[:, sl])
        up = _dot(u, wu_ref[:, sl])
        h = (_silu(g) * up).astype(BF16)
        acc = acc + _dot(h, wd_ref[sl, :])
    y = x + 0.5 * acc
    if has_final:
        y = _rms(y, fw_ref[...])
    o_ref[...] = y


def _stacked_spec(w, idx):
    n_lead = len(idx)
    return pl.BlockSpec((None,) * n_lead + w.shape[n_lead:], lambda *_: tuple(idx) + (0,) * (w.ndim - n_lead),
                        pipeline_mode=pl.Buffered(1))


def _ffn(x, nw, wg, wu, wd, w_idx, mix=None, final_w=None):
    m, d = x.shape
    hidden = wg.shape[-1]
    tm = _tile(m, FFN_TOKEN_TILE)
    args, specs = [x], [_row_spec(tm, d)]
    if mix is not None:
        a, y, wo, wo_idx = mix
        args += [a, y, wo]
        specs += [_row_spec(tm, a.shape[1]), _row_spec(tm, y.shape[1]), _stacked_spec(wo, (wo_idx,))]
    args += [nw, wg, wu, wd]
    specs += [_const_spec(nw.shape), _stacked_spec(wg, w_idx), _stacked_spec(wu, w_idx), _stacked_spec(wd, w_idx)]
    if final_w is not None:
        args.append(final_w)
        specs.append(_const_spec(final_w.shape))
    return pl.pallas_call(
        functools.partial(_ffn_kernel, n_chunks=hidden // FFN_CHUNK, has_mix=mix is not None,
                          has_final=final_w is not None),
        grid=(m // tm,),
        in_specs=specs,
        out_specs=_row_spec(tm, d),
        out_shape=jax.ShapeDtypeStruct((m, d), F32),
        compiler_params=_cparams(("parallel",), VMEM_LIMIT_BIG),
        name="ffn",
    )(*args)


def _block_diag_queries(q, t_new):
    qt = jnp.concatenate([q] * (2 * DIFF_HEADS), axis=0)
    row = lax.broadcasted_iota(jnp.int32, qt.shape, 0)
    col = lax.broadcasted_iota(jnp.int32, qt.shape, 1)
    same = lax.div(row, t_new) == lax.div(col, DIFF_HEAD_DIM)
    return jnp.where(same, qt, 0.0).astype(BF16)


def _softmax_update(scores, m_s, l_s):
    m = m_s[...]
    m_new = m
    for s in scores:
        m_new = jnp.maximum(m_new, jnp.max(s, axis=1, keepdims=True))
    alpha = jnp.exp(m - m_new)
    l = alpha * l_s[...]
    probs = []
    for s in scores:
        p = jnp.exp(s - m_new)
        l = l + jnp.sum(p, axis=1, keepdims=True)
        probs.append(p.astype(BF16))
    m_s[...] = m_new
    l_s[...] = l
    return alpha, probs


def _weighted_values(alpha, probs, values, acc_s, head_rows):
    acc = alpha * acc_s[...]
    for pb, v_heads in zip(probs, values):
        acc = acc + jnp.concatenate(
            [_dot(pb[h * head_rows:(h + 1) * head_rows], v_heads[h]) for h in range(DIFF_HEADS)], axis=0)
    acc_s[...] = acc


def _ffn_host_kernel(pt_ref, *refs, n_chunks, has_mix, has_final, n_pages, steps_per_seq, n_steps, seq0, layer):
    it = iter(refs)
    x_ref = next(it)
    if has_mix:
        a_ref, y_ref, wo_ref = next(it), next(it), next(it)
    nw_ref, wg_ref, wu_ref, wd_ref = next(it), next(it), next(it), next(it)
    fw_ref = next(it) if has_final else None
    q_ref, ck_ref, cv_ref = next(it), next(it), next(it)
    o_ref, m_out, l_out, acc_out = next(it), next(it), next(it), next(it)
    wq_s, m_s, l_s, acc_s, kbuf, vbuf, ksem, vsem = (next(it) for _ in range(8))
    step = pl.program_id(0)
    part = lax.rem(step, steps_per_seq)
    slot = lax.rem(step, 2)
    t_new = q_ref.shape[0]
    page = kbuf.shape[3]
    group = 2
    n_groups = n_pages // group

    def start_pages(of_step, into_slot):
        seq = seq0 + of_step // steps_per_seq
        first = lax.rem(of_step, steps_per_seq) * n_pages
        for g in range(n_pages):
            phys = pt_ref[seq, first + g]
            pltpu.make_async_copy(ck_ref.at[layer, phys], kbuf.at[into_slot, g], ksem.at[into_slot]).start()
            pltpu.make_async_copy(cv_ref.at[layer, phys], vbuf.at[into_slot, g], vsem.at[into_slot]).start()

    def wait_pages(in_slot):
        pltpu.make_async_copy(ck_ref.at[layer, pl.ds(0, n_pages)], kbuf.at[in_slot], ksem.at[in_slot]).wait()
        pltpu.make_async_copy(cv_ref.at[layer, pl.ds(0, n_pages)], vbuf.at[in_slot], vsem.at[in_slot]).wait()

    @pl.when(step == 0)
    def _():
        start_pages(step, slot)

    wait_pages(slot)
    start_pages(jnp.minimum(step + 1, n_steps - 1), 1 - slot)

    @pl.when(part == 0)
    def _():
        wq_s[...] = _block_diag_queries(q_ref[...], t_new)
        m_s[...] = jnp.full(m_s.shape, -jnp.inf, F32)
        l_s[...] = jnp.zeros(l_s.shape, F32)
        acc_s[...] = jnp.zeros(acc_s.shape, F32)

    def page_scores(g):
        wq = wq_s[...]
        return [_dot(wq, kbuf[slot, p].astype(BF16)) for p in range(g * group, (g + 1) * group)]

    def page_values(g):
        return [[vbuf[slot, p, pl.ds(h, page, stride=DIFF_HEADS), :].astype(BF16) for h in range(DIFF_HEADS)]
                for p in range(g * group, (g + 1) * group)]

    x = x_ref[...]
    if has_mix:
        half = a_ref.shape[1]
        x = x + _dot(a_ref[...].astype(BF16), wo_ref[:half, :]) + _dot(y_ref[...].astype(BF16), wo_ref[half:, :])
    u = _rms(x, nw_ref[...]).astype(BF16)
    acc = jnp.zeros(x.shape, F32)
    scores = None
    for c in range(n_chunks):
        sl = slice(c * FFN_CHUNK, (c + 1) * FFN_CHUNK)
        g = _dot(u, wg_ref[:, sl])
        up = _dot(u, wu_ref[:, sl])
        h = (_silu(g) * up).astype(BF16)
        acc = acc + _dot(h, wd_ref[sl, :])
        if scores is not None:
            alpha, probs = _softmax_update(scores, m_s, l_s)
            _weighted_values(alpha, probs, page_values(c - 1), acc_s, 2 * t_new)
        scores = page_scores(c) if c < n_groups else None
    assert scores is None
    y = x + 0.5 * acc
    if has_final:
        y = _rms(y, fw_ref[...])
    o_ref[...] = y

    @pl.when(part == steps_per_seq - 1)
    def _():
        m_out[...] = m_s[...]
        l_out[...] = l_s[...]
        acc_out[...] = acc_s[...]

    @pl.when(step == n_steps - 1)
    def _():
        wait_pages(1 - slot)


def _ffn_host(x, nw, wg, wu, wd, w_idx, paged, q, seq0, n_seq, mix=None, final_w=None):
    cache_k, cache_v, page_table, layer = paged
    m, d = x.shape
    hidden = wg.shape[-1]
    tm = _tile(m, TOKEN_TILE)
    n_steps = m // tm
    n_log = page_table.shape[1]
    steps_per_seq = n_steps // n_seq
    n_pages = n_log // steps_per_seq
    assert steps_per_seq * n_seq == n_steps and n_pages * steps_per_seq == n_log and n_pages % 2 == 0
    assert n_pages // 2 < hidden // FFN_CHUNK
    t_new = q.shape[0] // page_table.shape[0]
    n_rows = 2 * DIFF_HEADS * t_new
    n_layers, n_phys, page = cache_k.shape[:3]
    ck = jnp.transpose(cache_k, (0, 1, 3, 4, 5, 2)).reshape(n_layers, n_phys, QK_WIDTH, page)
    cv = cache_v.reshape(n_layers, n_phys, page * DIFF_HEADS, HEAD_LANES)

    rows = lambda w: pl.BlockSpec((tm, w), lambda i, pt: (i, 0))
    const = lambda a: pl.BlockSpec(a.shape, lambda i, pt: (0,) * a.ndim)

    def stacked(w, idx):
        n_lead = len(idx)
        return pl.BlockSpec((None,) * n_lead + w.shape[n_lead:], lambda i, pt: tuple(idx) + (0,) * (w.ndim - n_lead),
                            pipeline_mode=pl.Buffered(1))

    args, specs = [x], [rows(d)]
    if mix is not None:
        a, y, wo, wo_idx = mix
        args += [a, y, wo]
        specs += [rows(a.shape[1]), rows(y.shape[1]), stacked(wo, (wo_idx,))]
    args += [nw, wg, wu, wd]
    specs += [const(nw), stacked(wg, w_idx), stacked(wu, w_idx), stacked(wd, w_idx)]
    if final_w is not None:
        args.append(final_w)
        specs.append(const(final_w))
    args += [q, ck, cv]
    specs += [pl.BlockSpec((t_new, QK_WIDTH), lambda i, pt: (seq0 + i // steps_per_seq, 0)),
              pl.BlockSpec(memory_space=pl.ANY), pl.BlockSpec(memory_space=pl.ANY)]
    stat = lambda w: pl.BlockSpec((n_rows, w), lambda i, pt: (i // steps_per_seq, 0))
    grid_spec = pltpu.PrefetchScalarGridSpec(
        num_scalar_prefetch=1,
        grid=(n_steps,),
        in_specs=specs,
        out_specs=[rows(d), stat(1), stat(1), stat(HEAD_LANES)],
        scratch_shapes=[pltpu.VMEM((n_rows, QK_WIDTH), BF16), pltpu.VMEM((n_rows, 1), F32),
                        pltpu.VMEM((n_rows, 1), F32), pltpu.VMEM((n_rows, HEAD_LANES), F32),
                        pltpu.VMEM((2, n_pages, QK_WIDTH, page), F32),
                        pltpu.VMEM((2, n_pages, page * DIFF_HEADS, HEAD_LANES), F32),
                        pltpu.SemaphoreType.DMA((2,)), pltpu.SemaphoreType.DMA((2,))],
    )
    return pl.pallas_call(
        functools.partial(_ffn_host_kernel, n_chunks=hidden // FFN_CHUNK, has_mix=mix is not None,
                          has_final=final_w is not None, n_pages=n_pages, steps_per_seq=steps_per_seq,
                          n_steps=n_steps, seq0=seq0, layer=layer),
        grid_spec=grid_spec,
        out_shape=[jax.ShapeDtypeStruct((m, d), F32), jax.ShapeDtypeStruct((n_seq * n_rows, 1), F32),
                   jax.ShapeDtypeStruct((n_seq * n_rows, 1), F32), jax.ShapeDtypeStruct((n_seq * n_rows, HEAD_LANES), F32)],
        compiler_params=_cparams(("arbitrary",), VMEM_LIMIT_BIG),
        name="ffn_host",
    )(page_table, *args)


def _inproj_kernel(x_ref, nw_ref, w_ref, wdt_ref, cos_ref, sin_ref, *out_refs, prompt_layout):
    if prompt_layout:
        qt_ref, kt_ref, kb_ref, v4_ref, vt_ref, z_ref, xbc_ref, dt_ref = out_refs
    else:
        q_ref, k_ref, v_ref, z_ref, xbc_ref, dt_ref = out_refs
    u = _rms(x_ref[...], nw_ref[...]).astype(BF16)
    tm = u.shape[0]
    reps = QK_WIDTH // LANES
    cos = jnp.concatenate([cos_ref[...]] * reps, axis=1)
    sin = jnp.concatenate([sin_ref[...]] * reps, axis=1)
    lane = lax.broadcasted_iota(jnp.int32, (tm, QK_WIDTH), 1)
    low_half = lax.rem(lane, DIFF_HEAD_DIM) < DIFF_HEAD_DIM // 2
    half = DIFF_HEAD_DIM // 2

    def rope(t):
        up = pltpu.roll(t, QK_WIDTH - half, 1)
        down = pltpu.roll(t, half, 1)
        return t * cos + jnp.where(low_half, up, down) * sin

    q_scale = DIFF_HEAD_DIM ** -0.5 * (LOG2_E if prompt_layout else 1.0)
    q = rope(_dot(u, w_ref[:, 0:QK_WIDTH])) * q_scale
    k = rope(_dot(u, w_ref[:, QK_WIDTH:2 * QK_WIDTH]))
    v = _dot(u, w_ref[:, 2 * QK_WIDTH:3 * QK_WIDTH])
    if prompt_layout:
        def head_blocks(t_ref, t):
            tb = t_ref.shape[-1]
            for h in range(DIFF_HEADS):
                for jb in range(tm // tb):
                    t_ref[h, jb] = t[h * HEAD_LANES:(h + 1) * HEAD_LANES, jb * tb:(jb + 1) * tb].astype(BF16)

        head_blocks(qt_ref, q.T)
        kt_ref[...] = k.T
        kb_ref[...] = k.astype(BF16)
        for h in range(DIFF_HEADS):
            v4_ref[pl.ds(h, tm, stride=DIFF_HEADS), :] = v[:, h * HEAD_LANES:(h + 1) * HEAD_LANES]
        head_blocks(vt_ref, v.T)
    else:
        q_ref[...] = q
        k_ref[...] = k
        v_ref[...] = v
    z0 = 3 * QK_WIDTH
    z_ref[...] = _dot(u, w_ref[:, z0:z0 + SSM_INNER])
    xbc_ref[...] = _dot(u, w_ref[:, z0 + SSM_INNER:z0 + SSM_INNER + CONV_CH])
    dt_ref[...] = _dot(u, wdt_ref[...])


def _inproj(x, nw, w_in, w_idx, w_dt, cos, sin, prompt_layout):
    m, d = x.shape
    period = cos.shape[0]
    tm = _tile(period, TOKEN_TILE)
    n_per = period // tm
    tab_spec = pl.BlockSpec((tm, LANES), lambda i: (i % n_per, 0))
    rows = lambda w, dt: (_row_spec(tm, w), jax.ShapeDtypeStruct((m, w), dt))
    tail = [rows(SSM_INNER, F32), rows(CONV_CH, F32), rows(LANES, F32)]
    if prompt_layout:
        n_seq = m // period

        def blocked(tile):
            ta = _tile(tm, tile)
            return (pl.BlockSpec((None, DIFF_HEADS, tm // ta, HEAD_LANES, ta), lambda i: (i // n_per, 0, i % n_per, 0, 0)),
                    jax.ShapeDtypeStruct((n_seq, DIFF_HEADS, period // ta, HEAD_LANES, ta), BF16))

        kt_out = (pl.BlockSpec((None, QK_WIDTH, tm), lambda i: (i // n_per, 0, i % n_per)),
                  jax.ShapeDtypeStruct((n_seq, QK_WIDTH, period), F32))
        v4_out = (_row_spec(tm * DIFF_HEADS, HEAD_LANES), jax.ShapeDtypeStruct((m * DIFF_HEADS, HEAD_LANES), F32))
        outs = [blocked(ATTN_Q_TILE), kt_out, rows(QK_WIDTH, BF16), v4_out, blocked(ATTN_K_TILE)] + tail
    else:
        outs = [rows(QK_WIDTH, F32)] * 3 + tail
    return pl.pallas_call(
        functools.partial(_inproj_kernel, prompt_layout=prompt_layout),
        grid=(m // tm,),
        in_specs=[_row_spec(tm, d), _const_spec(nw.shape), pl.BlockSpec((None,) + w_in.shape[1:], lambda i: (w_idx, 0, 0)),
                  _const_spec(w_dt.shape), tab_spec, tab_spec],
        out_specs=[spec for spec, _ in outs],
        out_shape=[shape for _, shape in outs],
        compiler_params=_cparams(("parallel",), VMEM_LIMIT_SMALL),
        name="inproj",
    )(x, nw, w_in, w_dt, cos, sin)


def _lambda(lam_ref, lambda_init):
    v = lam_ref[...]
    d1 = jnp.sum(v[0:1] * v[1:2], axis=1, keepdims=True)
    d2 = jnp.sum(v[2:3] * v[3:4], axis=1, keepdims=True)
    return jnp.exp(d1) - jnp.exp(d2) + lambda_init


def _subln(d, w, lambda_init):
    return _rms(d, w) * (1.0 - lambda_init)


def _attn_prompt_kernel(lam_ref, sub_ref, qt_ref, k_ref, vt_ref, o_ref, m_s, acc_s, *, lambda_init):
    n_qb, _, tqb = qt_ref.shape
    n_kb, _, tk = vt_ref.shape
    strip = tk
    n_strips = n_qb * tqb // strip
    sub = lax.broadcasted_iota(jnp.int32, (HEAD_LANES, strip), 0)
    first_comp = sub < DIFF_HEAD_DIM
    krow = lax.broadcasted_iota(jnp.int32, (tk, strip), 0)
    qcol = lax.broadcasted_iota(jnp.int32, (tk, strip), 1)
    causal = krow <= qcol
    zero = jnp.zeros((HEAD_LANES, strip), BF16)

    def cols(qs, comp):
        return slice((comp * n_strips + qs) * strip, (comp * n_strips + qs + 1) * strip)

    units = [(kb, qs, comp) for kb in range(n_kb) for qs in range(kb, n_strips) for comp in range(2)]

    def scores(unit):
        kb, qs, comp = unit
        qb, off = divmod(qs * strip, tqb)
        qt = qt_ref[qb, :, off:off + strip]
        q = jnp.where(first_comp, qt, zero) if comp == 0 else jnp.where(first_comp, zero, qt)
        return _dot(k_ref[kb * tk:(kb + 1) * tk, :], q)

    pending = [scores(u) for u in units[:ATTN_LOOKAHEAD]]
    ones = jnp.ones((ATTN_SUM_ROWS, tk), BF16)
    for i, (kb, qs, comp) in enumerate(units):
        if i + ATTN_LOOKAHEAD < len(units):
            pending.append(scores(units[i + ATTN_LOOKAHEAD]))
        s = pending.pop(0)
        c = cols(qs, comp)
        if comp == 0 and qs == kb:
            vt = jnp.concatenate([vt_ref[kb], ones], axis=0)
        if kb == qs:
            s = jnp.where(causal, s, -jnp.inf)
        if kb == 0:
            m = jnp.max(s, axis=0, keepdims=True)
            acc_s[:, c] = _dot(vt, jnp.exp2(s - m).astype(BF16))
        else:
            m_old = m_s[:, c]
            m = jnp.maximum(m_old, jnp.max(s, axis=0, keepdims=True))
            acc_s[:, c] = jnp.exp2(m_old - m) * acc_s[:, c] + _dot(vt, jnp.exp2(s - m).astype(BF16))
        m_s[:, c] = m
    lam = _lambda(lam_ref, lambda_init)
    for qs in range(n_strips):
        o0, o1 = (acc_s[:HEAD_LANES, cols(qs, comp)] * (1.0 / acc_s[HEAD_LANES:HEAD_LANES + 1, cols(qs, comp)])
                  for comp in range(2))
        d = (o0 - lam * o1).T
        o_ref[qs * strip:(qs + 1) * strip, :] = _subln(d, sub_ref[...], lambda_init).astype(o_ref.dtype)


def _attn_prompt(qt, kb, vt, lam_vecs, sub_w, lambda_init):
    batch, _, nq, _, tq = qt.shape
    nk, tk = vt.shape[2], vt.shape[4]
    seq = nq * tq
    q_spec = pl.BlockSpec((None, None, nq, HEAD_LANES, tq), lambda b, h: (b, h, 0, 0, 0))
    k_spec = pl.BlockSpec((seq, HEAD_LANES), lambda b, h: (b, h))
    v_spec = pl.BlockSpec((None, None, nk, HEAD_LANES, tk), lambda b, h: (b, h, 0, 0, 0))
    return pl.pallas_call(
        functools.partial(_attn_prompt_kernel, lambda_init=lambda_init),
        grid=(batch, DIFF_HEADS),
        in_specs=[pl.BlockSpec(lam_vecs.shape, lambda b, h: (0, 0)), pl.BlockSpec(sub_w.shape, lambda b, h: (0, 0)),
                  q_spec, k_spec, v_spec],
        out_specs=k_spec,
        out_shape=jax.ShapeDtypeStruct(kb.shape, BF16),
        scratch_shapes=[pltpu.VMEM((1, 2 * seq), F32), pltpu.VMEM((HEAD_LANES + ATTN_SUM_ROWS, 2 * seq), F32)],
        compiler_params=_cparams(("parallel", "parallel"), VMEM_LIMIT_SMALL),
        name="attn_prompt",
    )(lam_vecs, sub_w, qt, kb, vt)


def _attn_merge_kernel(lam_ref, sub_ref, q_ref, kn_ref, vn_ref, m_ref, l_ref, acc_ref, o_ref, m_s, l_s, acc_s,
                       *, lambda_init):
    t_new = q_ref.shape[0]
    head_rows = 2 * t_new
    page = LANES
    wq = _block_diag_queries(q_ref[...], t_new)
    pad = jnp.zeros((page - t_new, QK_WIDTH), F32)
    kn = jnp.concatenate([kn_ref[...], pad], axis=0).astype(BF16)
    vn = jnp.concatenate([vn_ref[...], pad], axis=0).astype(BF16)
    s = _dot_nt(wq, kn)
    row = lax.broadcasted_iota(jnp.int32, s.shape, 0)
    col = lax.broadcasted_iota(jnp.int32, s.shape, 1)
    s = jnp.where(col <= lax.rem(row, t_new), s, -jnp.inf)
    m_s[...] = m_ref[...]
    l_s[...] = l_ref[...]
    acc_s[...] = acc_ref[...]
    alpha, probs = _softmax_update([s], m_s, l_s)
    _weighted_values(alpha, probs, [[vn[:, h * HEAD_LANES:(h + 1) * HEAD_LANES] for h in range(DIFF_HEADS)]],
                     acc_s, head_rows)
    o = acc_s[...] / l_s[...]
    lam = _lambda(lam_ref, lambda_init)
    for h in range(DIFF_HEADS):
        r0 = h * head_rows
        d = o[r0:r0 + t_new] - lam * o[r0 + t_new:r0 + head_rows]
        o_ref[:, h * HEAD_LANES:(h + 1) * HEAD_LANES] = _subln(d, sub_ref[...], lambda_init)


def _attn_merge(q, k_new, v_new, m_past, l_past, acc_past, batch, lam_vecs, sub_w, lambda_init):
    m = q.shape[0]
    t_new = m // batch
    n_rows = 2 * DIFF_HEADS * t_new
    tok = pl.BlockSpec((t_new, QK_WIDTH), lambda b: (b, 0))
    stat = lambda w: pl.BlockSpec((n_rows, w), lambda b: (b, 0))
    return pl.pallas_call(
        functools.partial(_attn_merge_kernel, lambda_init=lambda_init),
        grid=(batch,),
        in_specs=[_const_spec(lam_vecs.shape), _const_spec(sub_w.shape), tok, tok, tok, stat(1), stat(1), stat(HEAD_LANES)],
        out_specs=tok,
        out_shape=jax.ShapeDtypeStruct((m, QK_WIDTH), F32),
        scratch_shapes=[pltpu.VMEM((n_rows, 1), F32), pltpu.VMEM((n_rows, 1), F32), pltpu.VMEM((n_rows, HEAD_LANES), F32)],
        compiler_params=_cparams(("parallel",), VMEM_LIMIT_SMALL),
        name="attn_merge",
    )(lam_vecs, sub_w, q, k_new, v_new, m_past, l_past, acc_past)


def _ssd_kernel(xbc_ref, z_ref, dt_ref, cw_ref, cb_ref, dtb_ref, alog_ref, dsk_ref, nw_ref, exp_ref,
                cinit_ref, h0_ref, y_ref, cout_ref, hout_ref, ext_s, st_s, *, t_in, n_chunks):
    c = pl.program_id(1)
    T = SSD_CHUNK
    tail = CONV_WIDTH - 1

    @pl.when(c == 0)
    def _():
        ext_s[0:SUBLANES, :] = jnp.zeros((SUBLANES, CONV_CH), F32)
        ext_s[SUBLANES - tail:SUBLANES, :] = cinit_ref[...]
        st_s[...] = h0_ref[...].reshape(SSM_INNER, SSM_STATE).T

    u = xbc_ref[...]
    ext_s[SUBLANES:SUBLANES + t_in, :] = u
    cw = cw_ref[...]
    conv = cb_ref[...] + cw[tail:tail + 1] * u
    for j in range(tail):
        conv = conv + cw[j:j + 1] * ext_s[SUBLANES - tail + j:SUBLANES - tail + j + t_in, :]
    ext_s[0:SUBLANES, :] = ext_s[t_in:t_in + SUBLANES, :]
    xc_all = _silu(conv)
    dt_all = jax.nn.softplus(dt_ref[...] + dtb_ref[...])
    z_all = z_ref[...]
    if t_in < T:
        xc_all = jnp.concatenate([xc_all, jnp.zeros((T - t_in, CONV_CH), F32)], axis=0)
        dt_all = jnp.concatenate([dt_all, jnp.zeros((T - t_in, LANES), F32)], axis=0)
        z_all = jnp.concatenate([z_all, jnp.zeros((T - t_in, SSM_INNER), F32)], axis=0)

    a = -jnp.exp(alog_ref[...])
    row = lax.broadcasted_iota(jnp.int32, (T, T), 0)
    col = lax.broadcasted_iota(jnp.int32, (T, T), 1)
    causal = row >= col
    tri = causal.astype(F32)
    expand = exp_ref[...]
    lane = lax.broadcasted_iota(jnp.int32, (T, LANES), 1)
    heads_per_group = SSM_HEADS // SSM_GROUPS
    nw = nw_ref[...]
    for ci in range(max(1, t_in // T)):
        rows = slice(ci * T, (ci + 1) * T)
        xc, dt, z = xc_all[rows], dt_all[rows], z_all[rows]
        acs = _dot_exact(tri, dt * a, exact_lhs=True)
        acs_t = acs.T
        both_x = _dot_exact(jnp.concatenate([dt, acs], axis=0), expand)
        dt_x, acs_x = both_x[:T], both_x[T:]
        last = acs_x[T - 1:T, :]
        decay_out = jnp.exp(acs_x)
        decay_end = jnp.exp(last - acs_x)
        chunk_decay = jnp.exp(last)

        xs = xc[:, :SSM_INNER]
        xdt = xs * dt_x
        xdt_b = xdt.astype(BF16)
        xw_b = (xdt * decay_end).astype(BF16)
        state = st_s[...]
        state_b = state.astype(BF16)
        ys = []
        for g in range(SSM_GROUPS):
            b0 = SSM_INNER + g * SSM_STATE
            c0 = SSM_INNER + SSM_GROUPS * SSM_STATE + g * SSM_STATE
            bg_t = xc[:, b0:b0 + SSM_STATE].T.astype(BF16)
            cg = xc[:, c0:c0 + SSM_STATE].astype(BF16)
            gl = slice(g * GROUP_LANES, (g + 1) * GROUP_LANES)
            cb = _dot(cg, bg_t)
            y_off = _dot(cg, state_b[:, gl]) * decay_out[:, gl]
            st_s[:, gl] = state[:, gl] * chunk_decay[:, gl] + _dot(bg_t, xw_b[:, gl])
            for pair in range(heads_per_group // 2):
                h0 = g * heads_per_group + 2 * pair
                xp = xdt_b[:, h0 * SSM_HEAD_DIM:(h0 + 2) * SSM_HEAD_DIM]
                outs = []
                for hh in (h0, h0 + 1):
                    seg = acs[:, hh:hh + 1] - acs_t[hh:hh + 1, :]
                    w = cb * jnp.exp(jnp.where(causal, seg, -jnp.inf))
                    outs.append(_dot(w.astype(BF16), xp))
                ys.append(jnp.where(lane < SSM_HEAD_DIM, outs[0], outs[1])
                          + y_off[:, 2 * pair * SSM_HEAD_DIM:(2 * pair + 2) * SSM_HEAD_DIM])
        y = jnp.concatenate(ys, axis=1) + dsk_ref[...] * xs
        y = y * _silu(z)
        parts = []
        for g in range(SSM_GROUPS):
            gl = slice(g * GROUP_LANES, (g + 1) * GROUP_LANES)
            parts.append(_rms(y[:, gl], nw[:, gl]))
        y = jnp.concatenate(parts, axis=1).astype(y_ref.dtype)
        if t_in < T:
            y_ref[...] = y[:t_in]
        else:
            y_ref[rows, :] = y

    @pl.when(c == n_chunks - 1)
    def _():
        cout_ref[...] = ext_s[SUBLANES - tail:SUBLANES, :]
        hout_ref[...] = st_s[...].T.reshape(SSM_HEADS, SSM_HEAD_DIM, SSM_STATE)


def _ssd(xbc, z, dt_raw, p, conv_init, h0, batch, seq, y_dtype):
    t_in = _tile(seq, SSD_CHUNK * SSD_CHUNKS_PER_STEP)
    n_chunks = seq // t_in
    assert t_in % SSD_CHUNK == 0 or n_chunks == 1
    tail = CONV_WIDTH - 1
    rows = lambda w: pl.BlockSpec((t_in, w), lambda b, c: (b * n_chunks + c, 0))
    per_b3 = lambda s: pl.BlockSpec((None,) + s, lambda b, c: (b,) + (0,) * len(s))
    consts = [p['conv_w'], p['conv_b'], p['dt_bias'], p['a_log'], p['d_skip'], p['ssm_norm_w'], p['expand']]
    return pl.pallas_call(
        functools.partial(_ssd_kernel, t_in=t_in, n_chunks=n_chunks),
        grid=(batch, n_chunks),
        in_specs=[rows(CONV_CH), rows(SSM_INNER), rows(LANES)]
                 + [pl.BlockSpec(a.shape, lambda b, c: (0, 0)) for a in consts]
                 + [per_b3((tail, CONV_CH)), per_b3((SSM_HEADS, SSM_HEAD_DIM, SSM_STATE))],
        out_specs=[rows(SSM_INNER), per_b3((tail, CONV_CH)), per_b3((SSM_HEADS, SSM_HEAD_DIM, SSM_STATE))],
        out_shape=[jax.ShapeDtypeStruct((batch * seq, SSM_INNER), y_dtype),
                   jax.ShapeDtypeStruct((batch, tail, CONV_CH), F32),
                   jax.ShapeDtypeStruct((batch, SSM_HEADS, SSM_HEAD_DIM, SSM_STATE), F32)],
        scratch_shapes=[pltpu.VMEM((SUBLANES + t_in, CONV_CH), F32), pltpu.VMEM((SSM_STATE, SSM_INNER), F32)],
        compiler_params=_cparams(("parallel", "arbitrary"), VMEM_LIMIT_SMALL),
        name="ssd",
    )(xbc, z, dt_raw, *consts, conv_init, h0)


def _pool_kernel(x_ref, nw_ref, init_ref, pw_ref, ps_ref, o_ref, pout_ref, ext_s, *, tp, n_tiles, pos0):
    t = pl.program_id(1)
    halo = POOL_HALO

    @pl.when(t == 0)
    def _():
        ext_s[0:halo, :] = jnp.zeros((halo, ext_s.shape[1]), F32)
        ext_s[halo - POOL_STATE_LEN:halo, :] = init_ref[...]

    x = x_ref[...]
    u = _rms(x, nw_ref[...])
    ext_s[halo:halo + tp, :] = u
    e = ext_s[...]
    gc = POOL_GROUP_CH
    sums = [e]
    for lvl in range(len(POOL_WINDOWS)):
        prev = sums[-1][:, gc:] if lvl else sums[-1]
        sums.append(prev + pltpu.roll(prev, 1 << lvl, 0))
    pos = pos0 + t * tp + lax.broadcasted_iota(jnp.int32, (tp, 1), 0)
    mixed = []
    for g, w in enumerate(POOL_WINDOWS):
        cnt = jnp.minimum(w, pos + 1).astype(F32)
        win = sums[g + 1][halo:, :gc]
        pooled = win / cnt - u[:, g * gc:(g + 1) * gc]
        mixed.append(_dot(pooled.astype(BF16), pw_ref[g]))
    o_ref[...] = x + jnp.concatenate(mixed, axis=1) * ps_ref[...]

    @pl.when(t == n_tiles - 1)
    def _():
        pout_ref[...] = ext_s[tp + halo - POOL_STATE_LEN:tp + halo, :]

    ext_s[0:halo, :] = ext_s[tp:tp + halo, :]


def _pool(x, nw, pool_init, pool_w, w_idx, pool_scale, batch, seq, pos0):
    d = x.shape[1]
    tp = _tile(seq, TOKEN_TILE)
    n_tiles = seq // tp
    rows = pl.BlockSpec((tp, d), lambda b, t: (b * n_tiles + t, 0))
    state = pl.BlockSpec((None, POOL_STATE_LEN, d), lambda b, t: (b, 0, 0))
    return pl.pallas_call(
        functools.partial(_pool_kernel, tp=tp, n_tiles=n_tiles, pos0=pos0),
        grid=(batch, n_tiles),
        in_specs=[rows, pl.BlockSpec(nw.shape, lambda b, t: (0, 0)), state,
                  _stacked_spec(pool_w, (w_idx,)), pl.BlockSpec(pool_scale.shape, lambda b, t: (0, 0))],
        out_specs=[rows, state],
        out_shape=[jax.ShapeDtypeStruct(x.shape, F32), jax.ShapeDtypeStruct((batch, POOL_STATE_LEN, d), F32)],
        scratch_shapes=[pltpu.VMEM((POOL_HALO + tp, d), F32)],
        compiler_params=_cparams(("parallel", "arbitrary"), VMEM_LIMIT_SMALL),
        name="pool",
    )(x, nw, pool_init, pool_w, pool_scale)


def _prep_params(norm_w, ffn_w_gate, ffn_w_up, ffn_w_down, ab_w_in, ab_w_out, ab_conv_w, ab_conv_b, ab_dt_bias,
                 ab_a_log, ab_d_skip, ab_ssm_norm_w, ab_lambda_q1, ab_lambda_k1, ab_lambda_q2, ab_lambda_k2,
                 ab_subln_w, pool_w, pool_scale, final_norm_w):
    def lane_pad(v):
        return jnp.pad(v, ((0, 0), (0, LANES - v.shape[1])))[:, None, :]

    expand = np.zeros((LANES, SSM_INNER), np.float32)
    for h in range(SSM_HEADS):
        expand[h, h * SSM_HEAD_DIM:(h + 1) * SSM_HEAD_DIM] = 1.0
    return dict(
        norm_w=norm_w[:, :, None, :],
        wg=ffn_w_gate.astype(BF16), wu=ffn_w_up.astype(BF16), wd=ffn_w_down.astype(BF16),
        w_in=ab_w_in.astype(BF16),
        w_dt=jnp.pad(ab_w_in[:, :, MAIN_PROJ:], ((0, 0), (0, 0), (0, LANES - SSM_HEADS))).astype(BF16),
        w_out=ab_w_out.astype(BF16),
        conv_w=ab_conv_w, conv_b=ab_conv_b[:, None, :],
        dt_bias=lane_pad(ab_dt_bias), a_log=lane_pad(ab_a_log),
        d_skip=jnp.repeat(ab_d_skip, SSM_HEAD_DIM, axis=1)[:, None, :],
        ssm_norm_w=ab_ssm_norm_w[:, None, :],
        lam=jnp.stack([ab_lambda_q1, ab_lambda_k1, ab_lambda_q2, ab_lambda_k2], axis=1),
        subln_w=ab_subln_w[:, None, :],
        pool_w=pool_w.astype(BF16), pool_scale=pool_scale[:, None, :],
        final_norm_w=final_norm_w[None, :],
        expand=jnp.asarray(expand),
    )


def _ssm_params(p, i):
    return dict(conv_w=p['conv_w'][i], conv_b=p['conv_b'][i], dt_bias=p['dt_bias'][i], a_log=p['a_log'][i],
                d_skip=p['d_skip'][i], ssm_norm_w=p['ssm_norm_w'][i], expand=p['expand'])


def _lambda_init(layer):
    return 0.8 - 0.6 * math.exp(-0.3 * layer)


def kernel(x_prompt, x_sample, cache_k, cache_v, state_conv, state_ssm, state_pool, page_table, norm_w, ffn_w_gate, ffn_w_up, ffn_w_down, ab_w_in, ab_w_out, ab_conv_w, ab_conv_b, ab_dt_bias, ab_a_log, ab_d_skip, ab_ssm_norm_w, ab_lambda_q1, ab_lambda_k1, ab_lambda_q2, ab_lambda_k2, ab_subln_w, pool_w, pool_scale, final_norm_w):
    p = _prep_params(norm_w, ffn_w_gate, ffn_w_up, ffn_w_down, ab_w_in, ab_w_out, ab_conv_w, ab_conv_b, ab_dt_bias,
                     ab_a_log, ab_d_skip, ab_ssm_norm_w, ab_lambda_q1, ab_lambda_k1, ab_lambda_q2, ab_lambda_k2,
                     ab_subln_w, pool_w, pool_scale, final_norm_w)
    depth = ffn_w_gate.shape[0]
    n_ab, n_pool = state_conv.shape[0], state_pool.shape[0]
    assert n_ab == 1 and depth == 2
    bp, lp, d = x_prompt.shape
    bs, ls, _ = x_sample.shape
    ffn_w = (p['wg'], p['wu'], p['wd'])
    nw = p['norm_w']
    lam, sub_w, li = p['lam'][0], p['subln_w'][0], _lambda_init(0)
    past_len = page_table.shape[1] * cache_k.shape[2]
    n_hosts = 2 * depth
    host_seqs = bs // n_hosts
    assert host_seqs * n_hosts == bs
    paged = (cache_k, cache_v, page_table, 0)

    xs = x_sample.reshape(bs * ls, d)
    cos_s, sin_s = _rope_tables(bs * ls, past_len, ls)
    xs = _ffn(xs, nw[0][0], *ffn_w, (0, 0))
    qs, ks, vs, zs, xbc_s, dt_s = _inproj(xs, nw[0][1], p['w_in'], 0, p['w_dt'][0], cos_s, sin_s, prompt_layout=False)
    stats = []

    def host(x, norm, w_idx, **kw):
        x, *st = _ffn_host(x, norm, *ffn_w, w_idx, paged, qs, len(stats) * host_seqs, host_seqs, **kw)
        stats.append(st)
        return x

    zeros = lambda shape: jnp.zeros(shape, F32)
    xp = x_prompt.reshape(bp * lp, d)
    cos_p, sin_p = _rope_tables(lp, 0, lp)
    xp = host(xp, nw[0][0], (0, 0))
    qt, kt, kb, v4, vt, zp, xbc_p, dt_p = _inproj(xp, nw[0][1], p['w_in'], 0, p['w_dt'][0], cos_p, sin_p,
                                                  prompt_layout=True)
    attn_p = _attn_prompt(qt, kb, vt, lam, sub_w, li)
    y_p, conv_p, ssm_p = _ssd(xbc_p, zp, dt_p, _ssm_params(p, 0), zeros((bp, CONV_WIDTH - 1, CONV_CH)),
                              zeros((bp, SSM_HEADS, SSM_HEAD_DIM, SSM_STATE)), bp, lp, BF16)
    xp = host(xp, nw[0][2], (0, 1), mix=(attn_p, y_p, p['w_out'], 0))
    xp = host(xp, nw[1][0], (1, 0))
    xp, pool_p = _pool(xp, nw[1][1], zeros((bp, POOL_STATE_LEN, d)), p['pool_w'], 0, p['pool_scale'][0], bp, lp, 0)
    yp = host(xp, nw[1][2], (1, 1), final_w=p['final_norm_w'])
    k_p = jnp.transpose(kt.reshape(bp, DIFF_HEADS, 2, DIFF_HEAD_DIM, lp), (0, 4, 1, 2, 3))

    m_past, l_past, acc_past = (jnp.concatenate([st[j] for st in stats], axis=0) for j in range(3))
    attn_s = _attn_merge(qs, ks, vs, m_past, l_past, acc_past, bs, lam, sub_w, li)
    y_s, conv_s, ssm_s = _ssd(xbc_s, zs, dt_s, _ssm_params(p, 0), state_conv[0], state_ssm[0], bs, ls, F32)
    xs = _ffn(xs, nw[0][2], *ffn_w, (0, 1), mix=(attn_s, y_s, p['w_out'], 0))
    xs = _ffn(xs, nw[1][0], *ffn_w, (1, 0))
    xs, pool_s = _pool(xs, nw[1][1], state_pool[0], p['pool_w'], 0, p['pool_scale'][0], bs, ls, past_len)
    ys = _ffn(xs, nw[1][2], *ffn_w, (1, 1), final_w=p['final_norm_w'])

    lead = lambda t: t[None]
    return (yp.reshape(bp, lp, d), ys.reshape(bs, ls, d),
            lead(k_p.reshape(bp, lp, DIFF_HEADS, 2, DIFF_HEAD_DIM)), lead(v4.reshape(bp, lp, DIFF_HEADS, HEAD_LANES)),
            lead(conv_p), lead(ssm_p), lead(pool_p),
            lead(ks.reshape(bs, ls, DIFF_HEADS, 2, DIFF_HEAD_DIM)), lead(vs.reshape(bs, ls, DIFF_HEADS, HEAD_LANES)),
            lead(conv_s), lead(ssm_s), lead(pool_s))
```

```python
import functools
import math

import numpy as np
import jax
import jax.numpy as jnp
from jax import lax
from jax.experimental import pallas as pl
from jax.experimental.pallas import tpu as pltpu

F32 = jnp.float32
BF16 = jnp.bfloat16

RMS_EPS = 1e-6
ROPE_THETA = 10000.0
LOG2_E = 1.4426950408889634

DIFF_HEADS = 4
DIFF_HEAD_DIM = 64
HEAD_LANES = 2 * DIFF_HEAD_DIM
QK_WIDTH = DIFF_HEADS * HEAD_LANES
SSM_INNER = 512
SSM_HEAD_DIM = 64
SSM_HEADS = 8
SSM_GROUPS = 2
SSM_STATE = 128
GROUP_LANES = SSM_INNER // SSM_GROUPS
CONV_WIDTH = 4
CONV_CH = SSM_INNER + 2 * SSM_GROUPS * SSM_STATE
SSD_CHUNK = 128
SSD_CHUNKS_PER_STEP = 8
POOL_WINDOWS = (2, 4, 8, 16)
POOL_GROUP_CH = 256
POOL_HALO = 16
POOL_STATE_LEN = 15
MAIN_PROJ = 2 * QK_WIDTH + QK_WIDTH + SSM_INNER + CONV_CH

LANES = 128
SUBLANES = 8
MXU_DIM = 256
VMEM_LIMIT_BIG = 58 * 1024 * 1024
VMEM_LIMIT_SMALL = 40 * 1024 * 1024

FFN_CHUNK = MXU_DIM
TOKEN_TILE = 512
FFN_TOKEN_TILE = 1024
ATTN_Q_TILE = 512
ATTN_K_TILE = 256
ATTN_SUM_ROWS = 16
ATTN_LOOKAHEAD = 5


def _cparams(sem, vmem):
    return pltpu.CompilerParams(dimension_semantics=sem, vmem_limit_bytes=vmem)


def _tile(m, pref):
    t = min(m, pref)
    while m % t:
        t //= 2
    return t


def _rms(x, w):
    ms = jnp.mean(x * x, axis=-1, keepdims=True)
    return x * lax.rsqrt(ms + RMS_EPS) * w


def _silu(x):
    return x * jax.nn.sigmoid(x)


def _dot(a, b):
    return jnp.dot(a, b, preferred_element_type=F32)


def _dot_nt(a, b):
    return lax.dot_general(a, b, (((1,), (1,)), ((), ())), preferred_element_type=F32)


def _split3(a):
    hi = a.astype(BF16)
    r = a - hi.astype(F32)
    mid = r.astype(BF16)
    return hi, mid, (r - mid.astype(F32)).astype(BF16)


def _dot_exact(a, b, exact_lhs=False):
    if exact_lhs:
        a = a.astype(BF16)
        return sum(_dot(a, t) for t in _split3(b))
    b = b.astype(BF16)
    return sum(_dot(t, b) for t in _split3(a))


def _row_spec(tile, width):
    return pl.BlockSpec((tile, width), lambda i: (i, 0))


def _const_spec(shape):
    return pl.BlockSpec(shape, lambda *_: (0,) * len(shape))


def _rope_table_kernel(inv_ref, cos_ref, sin_ref, *, pos0, period):
    shape = cos_ref.shape
    row = lax.broadcasted_iota(jnp.int32, shape, 0)
    lane = lax.broadcasted_iota(jnp.int32, shape, 1)
    pos = pos0 + lax.rem(row, period)
    ang = pos.astype(F32) * inv_ref[...]
    cos_ref[...] = jnp.cos(ang)
    s = jnp.sin(ang)
    sin_ref[...] = jnp.where(lax.rem(lane, DIFF_HEAD_DIM) < DIFF_HEAD_DIM // 2, -s, s)


def _rope_tables(rows, pos0, period):
    half = DIFF_HEAD_DIM // 2
    inv = 1.0 / (ROPE_THETA ** (jnp.arange(0, DIFF_HEAD_DIM, 2, dtype=F32) / DIFF_HEAD_DIM))
    inv = jnp.tile(inv, LANES // half)[None, :]
    return pl.pallas_call(
        functools.partial(_rope_table_kernel, pos0=pos0, period=period),
        out_shape=(jax.ShapeDtypeStruct((rows, LANES), F32),) * 2,
        name="rope_table",
    )(inv)


def _ffn_kernel(*refs, n_chunks, has_mix, has_final):
    it = iter(refs)
    x_ref = next(it)
    if has_mix:
        a_ref, y_ref, wo_ref = next(it), next(it), next(it)
    nw_ref, wg_ref, wu_ref, wd_ref = next(it), next(it), next(it), next(it)
    fw_ref = next(it) if has_final else None
    o_ref = next(it)

    x = x_ref[...]
    if has_mix:
        half = a_ref.shape[1]
        x = x + _dot(a_ref[...].astype(BF16), wo_ref[:half, :]) + _dot(y_ref[...].astype(BF16), wo_ref[half:, :])
    u = _rms(x, nw_ref[...]).astype(BF16)
    acc = jnp.zeros(x.shape, F32)
    for c in range(n_chunks):
        sl = slice(c * FFN_CHUNK, (c + 1) * FFN_CHUNK)
        g = _dot(u, wg_ref[:, sl])
        up = _dot(u, wu_ref[:, sl])
        h = (_silu(g) * up).astype(BF16)
        acc = acc + _dot(h, wd_ref[sl, :])
    y = x + 0.5 * acc
    if has_final:
        y = _rms(y, fw_ref[...])
    o_ref[...] = y


def _stacked_spec(w, idx):
    n_lead = len(idx)
    return pl.BlockSpec((None,) * n_lead + w.shape[n_lead:], lambda *_: tuple(idx) + (0,) * (w.ndim - n_lead),
                        pipeline_mode=pl.Buffered(1))


def _ffn(x, nw, wg, wu, wd, w_idx, mix=None, final_w=None):
    m, d = x.shape
    hidden = wg.shape[-1]
    tm = _tile(m, FFN_TOKEN_TILE)
    args, specs = [x], [_row_spec(tm, d)]
    if mix is not None:
        a, y, wo, wo_idx = mix
        args += [a, y, wo]
        specs += [_row_spec(tm, a.shape[1]), _row_spec(tm, y.shape[1]), _stacked_spec(wo, (wo_idx,))]
    args += [nw, wg, wu, wd]
    specs += [_const_spec(nw.shape), _stacked_spec(wg, w_idx), _stacked_spec(wu, w_idx), _stacked_spec(wd, w_idx)]
    if final_w is not None:
        args.append(final_w)
        specs.append(_const_spec(final_w.shape))
    return pl.pallas_call(
        functools.partial(_ffn_kernel, n_chunks=hidden // FFN_CHUNK, has_mix=mix is not None,
                          has_final=final_w is not None),
        grid=(m // tm,),
        in_specs=specs,
        out_specs=_row_spec(tm, d),
        out_shape=jax.ShapeDtypeStruct((m, d), F32),
        compiler_params=_cparams(("parallel",), VMEM_LIMIT_BIG),
        name="ffn",
    )(*args)


def _block_diag_queries(q, t_new):
    qt = jnp.concatenate([q] * (2 * DIFF_HEADS), axis=0)
    row = lax.broadcasted_iota(jnp.int32, qt.shape, 0)
    col = lax.broadcasted_iota(jnp.int32, qt.shape, 1)
    same = lax.div(row, t_new) == lax.div(col, DIFF_HEAD_DIM)
    return jnp.where(same, qt, 0.0).astype(BF16)


def _softmax_update(scores, m_s, l_s):
    m = m_s[...]
    m_new = m
    for s in scores:
        m_new = jnp.maximum(m_new, jnp.max(s, axis=1, keepdims=True))
    alpha = jnp.exp(m - m_new)
    l = alpha * l_s[...]
    probs = []
    for s in scores:
        p = jnp.exp(s - m_new)
        l = l + jnp.sum(p, axis=1, keepdims=True)
        probs.append(p.astype(BF16))
    m_s[...] = m_new
    l_s[...] = l
    return alpha, probs


def _weighted_values(alpha, probs, values, acc_s, head_rows):
    acc = alpha * acc_s[...]
    for pb, v_heads in zip(probs, values):
        acc = acc + jnp.concatenate(
            [_dot(pb[h * head_rows:(h + 1) * head_rows], v_heads[h]) for h in range(DIFF_HEADS)], axis=0)
    acc_s[...] = acc


def _ffn_host_kernel(pt_ref, *refs, n_chunks, has_mix, has_final, n_pages, steps_per_seq, n_steps, seq0, layer):
    it = iter(refs)
    x_ref = next(it)
    if has_mix:
        a_ref, y_ref, wo_ref = next(it), next(it), next(it)
    nw_ref, wg_ref, wu_ref, wd_ref = next(it), next(it), next(it), next(it)
    fw_ref = next(it) if has_final else None
    q_ref, ck_ref, cv_ref = next(it), next(it), next(it)
    o_ref, m_out, l_out, acc_out = next(it), next(it), next(it), next(it)
    wq_s, m_s, l_s, acc_s, kbuf, vbuf, ksem, vsem = (next(it) for _ in range(8))
    step = pl.program_id(0)
    part = lax.rem(step, steps_per_seq)
    slot = lax.rem(step, 2)
    t_new = q_ref.shape[0]
    page = kbuf.shape[3]
    group = 2
    n_groups = n_pages // group

    def start_pages(of_step, into_slot):
        seq = seq0 + of_step // steps_per_seq
        first = lax.rem(of_step, steps_per_seq) * n_pages
        for g in range(n_pages):
            phys = pt_ref[seq, first + g]
            pltpu.make_async_copy(ck_ref.at[layer, phys], kbuf.at[into_slot, g], ksem.at[into_slot]).start()
            pltpu.make_async_copy(cv_ref.at[layer, phys], vbuf.at[into_slot, g], vsem.at[into_slot]).start()

    def wait_pages(in_slot):
        pltpu.make_async_copy(ck_ref.at[layer, pl.ds(0, n_pages)], kbuf.at[in_slot], ksem.at[in_slot]).wait()
        pltpu.make_async_copy(cv_ref.at[layer, pl.ds(0, n_pages)], vbuf.at[in_slot], vsem.at[in_slot]).wait()

    @pl.when(step == 0)
    def _():
        start_pages(step, slot)

    wait_pages(slot)
    start_pages(jnp.minimum(step + 1, n_steps - 1), 1 - slot)

    @pl.when(part == 0)
    def _():
        wq_s[...] = _block_diag_queries(q_ref[...], t_new)
        m_s[...] = jnp.full(m_s.shape, -jnp.inf, F32)
        l_s[...] = jnp.zeros(l_s.shape, F32)
        acc_s[...] = jnp.zeros(acc_s.shape, F32)

    def page_scores(g):
        wq = wq_s[...]
        return [_dot(wq, kbuf[slot, p].astype(BF16)) for p in range(g * group, (g + 1) * group)]

    def page_values(g):
        return [[vbuf[slot, p, pl.ds(h, page, stride=DIFF_HEADS), :].astype(BF16) for h in range(DIFF_HEADS)]
                for p in range(g * group, (g + 1) * group)]

    x = x_ref[...]
    if has_mix:
        half = a_ref.shape[1]
        x = x + _dot(a_ref[...].astype(BF16), wo_ref[:half, :]) + _dot(y_ref[...].astype(BF16), wo_ref[half:, :])
    u = _rms(x, nw_ref[...]).astype(BF16)
    acc = jnp.zeros(x.shape, F32)
    scores = None
    for c in range(n_chunks):
        sl = slice(c * FFN_CHUNK, (c + 1) * FFN_CHUNK)
        g = _dot(u, wg_ref[:, sl])
        up = _dot(u, wu_ref[:, sl])
        h = (_silu(g) * up).astype(BF16)
        acc = acc + _dot(h, wd_ref[sl, :])
        if scores is not None:
            alpha, probs = _softmax_update(scores, m_s, l_s)
            _weighted_values(alpha, probs, page_values(c - 1), acc_s, 2 * t_new)
        scores = page_scores(c) if c < n_groups else None
    assert scores is None
    y = x + 0.5 * acc
    if has_final:
        y = _rms(y, fw_ref[...])
    o_ref[...] = y

    @pl.when(part == steps_per_seq - 1)
    def _():
        m_out[...] = m_s[...]
        l_out[...] = l_s[...]
        acc_out[...] = acc_s[...]

    @pl.when(step == n_steps - 1)
    def _():
        wait_pages(1 - slot)


def _ffn_host(x, nw, wg, wu, wd, w_idx, paged, q, seq0, n_seq, mix=None, final_w=None):
    cache_k, cache_v, page_table, layer = paged
    m, d = x.shape
    hidden = wg.shape[-1]
    tm = _tile(m, TOKEN_TILE)
    n_steps = m // tm
    n_log = page_table.shape[1]
    steps_per_seq = n_steps // n_seq
    n_pages = n_log // steps_per_seq
    assert steps_per_seq * n_seq == n_steps and n_pages * steps_per_seq == n_log and n_pages % 2 == 0
    assert n_pages // 2 < hidden // FFN_CHUNK
    t_new = q.shape[0] // page_table.shape[0]
    n_rows = 2 * DIFF_HEADS * t_new
    n_layers, n_phys, page = cache_k.shape[:3]
    ck = jnp.transpose(cache_k, (0, 1, 3, 4, 5, 2)).reshape(n_layers, n_phys, QK_WIDTH, page)
    cv = cache_v.reshape(n_layers, n_phys, page * DIFF_HEADS, HEAD_LANES)

    rows = lambda w: pl.BlockSpec((tm, w), lambda i, pt: (i, 0))
    const = lambda a: pl.BlockSpec(a.shape, lambda i, pt: (0,) * a.ndim)

    def stacked(w, idx):
        n_lead = len(idx)
        return pl.BlockSpec((None,) * n_lead + w.shape[n_lead:], lambda i, pt: tuple(idx) + (0,) * (w.ndim - n_lead),
                            pipeline_mode=pl.Buffered(1))

    args, specs = [x], [rows(d)]
    if mix is not None:
        a, y, wo, wo_idx = mix
        args += [a, y, wo]
        specs += [rows(a.shape[1]), rows(y.shape[1]), stacked(wo, (wo_idx,))]
    args += [nw, wg, wu, wd]
    specs += [const(nw), stacked(wg, w_idx), stacked(wu, w_idx), stacked(wd, w_idx)]
    if final_w is not None:
        args.append(final_w)
        specs.append(const(final_w))
    args += [q, ck, cv]
    specs += [pl.BlockSpec((t_new, QK_WIDTH), lambda i, pt: (seq0 + i // steps_per_seq, 0)),
              pl.BlockSpec(memory_space=pl.ANY), pl.BlockSpec(memory_space=pl.ANY)]
    stat = lambda w: pl.BlockSpec((n_rows, w), lambda i, pt: (i // steps_per_seq, 0))
    grid_spec = pltpu.PrefetchScalarGridSpec(
        num_scalar_prefetch=1,
        grid=(n_steps,),
        in_specs=specs,
        out_specs=[rows(d), stat(1), stat(1), stat(HEAD_LANES)],
        scratch_shapes=[pltpu.VMEM((n_rows, QK_WIDTH), BF16), pltpu.VMEM((n_rows, 1), F32),
                        pltpu.VMEM((n_rows, 1), F32), pltpu.VMEM((n_rows, HEAD_LANES), F32),
                        pltpu.VMEM((2, n_pages, QK_WIDTH, page), F32),
                        pltpu.VMEM((2, n_pages, page * DIFF_HEADS, HEAD_LANES), F32),
                        pltpu.SemaphoreType.DMA((2,)), pltpu.SemaphoreType.DMA((2,))],
    )
    return pl.pallas_call(
        functools.partial(_ffn_host_kernel, n_chunks=hidden // FFN_CHUNK, has_mix=mix is not None,
                          has_final=final_w is not None, n_pages=n_pages, steps_per_seq=steps_per_seq,
                          n_steps=n_steps, seq0=seq0, layer=layer),
        grid_spec=grid_spec,
        out_shape=[jax.ShapeDtypeStruct((m, d), F32), jax.ShapeDtypeStruct((n_seq * n_rows, 1), F32),
                   jax.ShapeDtypeStruct((n_seq * n_rows, 1), F32), jax.ShapeDtypeStruct((n_seq * n_rows, HEAD_LANES), F32)],
        compiler_params=_cparams(("arbitrary",), VMEM_LIMIT_BIG),
        name="ffn_host",
    )(page_table, *args)


def _inproj_kernel(x_ref, nw_ref, w_ref, wdt_ref, cos_ref, sin_ref, *out_refs, prompt_layout):
    if prompt_layout:
        qt_ref, kt_ref, kb_ref, v4_ref, vt_ref, z_ref, xbc_ref, dt_ref = out_refs
    else:
        q_ref, k_ref, v_ref, z_ref, xbc_ref, dt_ref = out_refs
    u = _rms(x_ref[...], nw_ref[...]).astype(BF16)
    tm = u.shape[0]
    reps = QK_WIDTH // LANES
    cos = jnp.concatenate([cos_ref[...]] * reps, axis=1)
    sin = jnp.concatenate([sin_ref[...]] * reps, axis=1)
    lane = lax.broadcasted_iota(jnp.int32, (tm, QK_WIDTH), 1)
    low_half = lax.rem(lane, DIFF_HEAD_DIM) < DIFF_HEAD_DIM // 2
    half = DIFF_HEAD_DIM // 2

    def rope(t):
        up = pltpu.roll(t, QK_WIDTH - half, 1)
        down = pltpu.roll(t, half, 1)
        return t * cos + jnp.where(low_half, up, down) * sin

    q_scale = DIFF_HEAD_DIM ** -0.5 * (LOG2_E if prompt_layout else 1.0)
    q = rope(_dot(u, w_ref[:, 0:QK_WIDTH])) * q_scale
    k = rope(_dot(u, w_ref[:, QK_WIDTH:2 * QK_WIDTH]))
    v = _dot(u, w_ref[:, 2 * QK_WIDTH:3 * QK_WIDTH])
    if prompt_layout:
        def head_blocks(t_ref, t):
            tb = t_ref.shape[-1]
            for h in range(DIFF_HEADS):
                for jb in range(tm // tb):
                    t_ref[h, jb] = t[h * HEAD_LANES:(h + 1) * HEAD_LANES, jb * tb:(jb + 1) * tb].astype(BF16)

        head_blocks(qt_ref, q.T)
        kt_ref[...] = k.T
        kb_ref[...] = k.astype(BF16)
        for h in range(DIFF_HEADS):
            v4_ref[pl.ds(h, tm, stride=DIFF_HEADS), :] = v[:, h * HEAD_LANES:(h + 1) * HEAD_LANES]
        head_blocks(vt_ref, v.T)
    else:
        q_ref[...] = q
        k_ref[...] = k
        v_ref[...] = v
    z0 = 3 * QK_WIDTH
    z_ref[...] = _dot(u, w_ref[:, z0:z0 + SSM_INNER]).astype(z_ref.dtype)
    xbc_ref[...] = _dot(u, w_ref[:, z0 + SSM_INNER:z0 + SSM_INNER + CONV_CH])
    dt_ref[...] = _dot(u, wdt_ref[...])


def _inproj(x, nw, w_in, w_idx, w_dt, cos, sin, prompt_layout):
    m, d = x.shape
    period = cos.shape[0]
    tm = _tile(period, TOKEN_TILE)
    n_per = period // tm
    tab_spec = pl.BlockSpec((tm, LANES), lambda i: (i % n_per, 0))
    rows = lambda w, dt: (_row_spec(tm, w), jax.ShapeDtypeStruct((m, w), dt))
    tail = [rows(SSM_INNER, BF16 if prompt_layout else F32), rows(CONV_CH, F32), rows(LANES, F32)]
    if prompt_layout:
        n_seq = m // period

        def blocked(tile):
            ta = _tile(tm, tile)
            return (pl.BlockSpec((None, DIFF_HEADS, tm // ta, HEAD_LANES, ta), lambda i: (i // n_per, 0, i % n_per, 0, 0)),
                    jax.ShapeDtypeStruct((n_seq, DIFF_HEADS, period // ta, HEAD_LANES, ta), BF16))

        kt_out = (pl.BlockSpec((None, QK_WIDTH, tm), lambda i: (i // n_per, 0, i % n_per)),
                  jax.ShapeDtypeStruct((n_seq, QK_WIDTH, period), F32))
        v4_out = (_row_spec(tm * DIFF_HEADS, HEAD_LANES), jax.ShapeDtypeStruct((m * DIFF_HEADS, HEAD_LANES), F32))
        outs = [blocked(ATTN_Q_TILE), kt_out, rows(QK_WIDTH, BF16), v4_out, blocked(ATTN_K_TILE)] + tail
    else:
        outs = [rows(QK_WIDTH, F32)] * 3 + tail
    return pl.pallas_call(
        functools.partial(_inproj_kernel, prompt_layout=prompt_layout),
        grid=(m // tm,),
        in_specs=[_row_spec(tm, d), _const_spec(nw.shape), pl.BlockSpec((None,) + w_in.shape[1:], lambda i: (w_idx, 0, 0)),
                  _const_spec(w_dt.shape), tab_spec, tab_spec],
        out_specs=[spec for spec, _ in outs],
        out_shape=[shape for _, shape in outs],
        compiler_params=_cparams(("parallel",), VMEM_LIMIT_SMALL),
        name="inproj",
    )(x, nw, w_in, w_dt, cos, sin)


def _lambda(lam_ref, lambda_init):
    v = lam_ref[...]
    d1 = jnp.sum(v[0:1] * v[1:2], axis=1, keepdims=True)
    d2 = jnp.sum(v[2:3] * v[3:4], axis=1, keepdims=True)
    return jnp.exp(d1) - jnp.exp(d2) + lambda_init


def _subln(d, w, lambda_init):
    return _rms(d, w) * (1.0 - lambda_init)


def _attn_prompt_kernel(lam_ref, sub_ref, qt_ref, k_ref, vt_ref, o_ref, m_s, acc_s, *, lambda_init):
    n_qb, _, tqb = qt_ref.shape
    n_kb, _, tk = vt_ref.shape
    strip = tk
    n_strips = n_qb * tqb // strip
    sub = lax.broadcasted_iota(jnp.int32, (HEAD_LANES, strip), 0)
    first_comp = sub < DIFF_HEAD_DIM
    krow = lax.broadcasted_iota(jnp.int32, (tk, strip), 0)
    qcol = lax.broadcasted_iota(jnp.int32, (tk, strip), 1)
    causal = krow <= qcol
    zero = jnp.zeros((HEAD_LANES, strip), BF16)

    def cols(qs, comp):
        return slice((comp * n_strips + qs) * strip, (comp * n_strips + qs + 1) * strip)

    units = [(kb, qs, comp) for kb in range(n_kb) for qs in range(kb, n_strips) for comp in range(2)]

    def scores(unit):
        kb, qs, comp = unit
        qb, off = divmod(qs * strip, tqb)
        qt = qt_ref[qb, :, off:off + strip]
        q = jnp.where(first_comp, qt, zero) if comp == 0 else jnp.where(first_comp, zero, qt)
        return _dot(k_ref[kb * tk:(kb + 1) * tk, :], q)

    pending = [scores(u) for u in units[:ATTN_LOOKAHEAD]]
    ones = jnp.ones((ATTN_SUM_ROWS, tk), BF16)
    for i, (kb, qs, comp) in enumerate(units):
        if i + ATTN_LOOKAHEAD < len(units):
            pending.append(scores(units[i + ATTN_LOOKAHEAD]))
        s = pending.pop(0)
        c = cols(qs, comp)
        if comp == 0 and qs == kb:
            vt = jnp.concatenate([vt_ref[kb], ones], axis=0)
        if kb == qs:
            s = jnp.where(causal, s, -jnp.inf)
        if kb == 0:
            m = jnp.max(s, axis=0, keepdims=True)
            acc_s[:, c] = _dot(vt, jnp.exp2(s - m).astype(BF16))
        else:
            m_old = m_s[:, c]
            m = jnp.maximum(m_old, jnp.max(s, axis=0, keepdims=True))
            acc_s[:, c] = jnp.exp2(m_old - m) * acc_s[:, c] + _dot(vt, jnp.exp2(s - m).astype(BF16))
        m_s[:, c] = m
    lam = _lambda(lam_ref, lambda_init)
    for qs in range(n_strips):
        o0, o1 = (acc_s[:HEAD_LANES, cols(qs, comp)] * (1.0 / acc_s[HEAD_LANES:HEAD_LANES + 1, cols(qs, comp)])
                  for comp in range(2))
        d = (o0 - lam * o1).T
        o_ref[qs * strip:(qs + 1) * strip, :] = _subln(d, sub_ref[...], lambda_init).astype(o_ref.dtype)


def _attn_prompt(qt, kb, vt, lam_vecs, sub_w, lambda_init):
    batch, _, nq, _, tq = qt.shape
    nk, tk = vt.shape[2], vt.shape[4]
    seq = nq * tq
    q_spec = pl.BlockSpec((None, None, nq, HEAD_LANES, tq), lambda b, h: (b, h, 0, 0, 0))
    k_spec = pl.BlockSpec((seq, HEAD_LANES), lambda b, h: (b, h))
    v_spec = pl.BlockSpec((None, None, nk, HEAD_LANES, tk), lambda b, h: (b, h, 0, 0, 0))
    return pl.pallas_call(
        functools.partial(_attn_prompt_kernel, lambda_init=lambda_init),
        grid=(batch, DIFF_HEADS),
        in_specs=[pl.BlockSpec(lam_vecs.shape, lambda b, h: (0, 0)), pl.BlockSpec(sub_w.shape, lambda b, h: (0, 0)),
                  q_spec, k_spec, v_spec],
        out_specs=k_spec,
        out_shape=jax.ShapeDtypeStruct(kb.shape, BF16),
        scratch_shapes=[pltpu.VMEM((1, 2 * seq), F32), pltpu.VMEM((HEAD_LANES + ATTN_SUM_ROWS, 2 * seq), F32)],
        compiler_params=_cparams(("parallel", "parallel"), VMEM_LIMIT_SMALL),
        name="attn_prompt",
    )(lam_vecs, sub_w, qt, kb, vt)


def _attn_merge_kernel(lam_ref, sub_ref, q_ref, kn_ref, vn_ref, m_ref, l_ref, acc_ref, o_ref, m_s, l_s, acc_s,
                       *, lambda_init):
    t_new = q_ref.shape[0]
    head_rows = 2 * t_new
    page = LANES
    wq = _block_diag_queries(q_ref[...], t_new)
    pad = jnp.zeros((page - t_new, QK_WIDTH), F32)
    kn = jnp.concatenate([kn_ref[...], pad], axis=0).astype(BF16)
    vn = jnp.concatenate([vn_ref[...], pad], axis=0).astype(BF16)
    s = _dot_nt(wq, kn)
    row = lax.broadcasted_iota(jnp.int32, s.shape, 0)
    col = lax.broadcasted_iota(jnp.int32, s.shape, 1)
    s = jnp.where(col <= lax.rem(row, t_new), s, -jnp.inf)
    m_s[...] = m_ref[...]
    l_s[...] = l_ref[...]
    acc_s[...] = acc_ref[...]
    alpha, probs = _softmax_update([s], m_s, l_s)
    _weighted_values(alpha, probs, [[vn[:, h * HEAD_LANES:(h + 1) * HEAD_LANES] for h in range(DIFF_HEADS)]],
                     acc_s, head_rows)
    o = acc_s[...] / l_s[...]
    lam = _lambda(lam_ref, lambda_init)
    for h in range(DIFF_HEADS):
        r0 = h * head_rows
        d = o[r0:r0 + t_new] - lam * o[r0 + t_new:r0 + head_rows]
        o_ref[:, h * HEAD_LANES:(h + 1) * HEAD_LANES] = _subln(d, sub_ref[...], lambda_init)


def _attn_merge(q, k_new, v_new, m_past, l_past, acc_past, batch, lam_vecs, sub_w, lambda_init):
    m = q.shape[0]
    t_new = m // batch
    n_rows = 2 * DIFF_HEADS * t_new
    tok = pl.BlockSpec((t_new, QK_WIDTH), lambda b: (b, 0))
    stat = lambda w: pl.BlockSpec((n_rows, w), lambda b: (b, 0))
    return pl.pallas_call(
        functools.partial(_attn_merge_kernel, lambda_init=lambda_init),
        grid=(batch,),
        in_specs=[_const_spec(lam_vecs.shape), _const_spec(sub_w.shape), tok, tok, tok, stat(1), stat(1), stat(HEAD_LANES)],
        out_specs=tok,
        out_shape=jax.ShapeDtypeStruct((m, QK_WIDTH), F32),
        scratch_shapes=[pltpu.VMEM((n_rows, 1), F32), pltpu.VMEM((n_rows, 1), F32), pltpu.VMEM((n_rows, HEAD_LANES), F32)],
        compiler_params=_cparams(("parallel",), VMEM_LIMIT_SMALL),
        name="attn_merge",
    )(lam_vecs, sub_w, q, k_new, v_new, m_past, l_past, acc_past)


def _ssd_kernel(xbc_ref, z_ref, dt_ref, cw_ref, cb_ref, dtb_ref, alog_ref, dsk_ref, nw_ref, exp_ref,
                cinit_ref, h0_ref, y_ref, cout_ref, hout_ref, ext_s, st_s, *, t_in, n_chunks):
    c = pl.program_id(1)
    T = SSD_CHUNK
    tail = CONV_WIDTH - 1

    @pl.when(c == 0)
    def _():
        ext_s[0:SUBLANES, :] = jnp.zeros((SUBLANES, CONV_CH), F32)
        ext_s[SUBLANES - tail:SUBLANES, :] = cinit_ref[...]
        st_s[...] = h0_ref[...].reshape(SSM_INNER, SSM_STATE).T

    u = xbc_ref[...]
    ext_s[SUBLANES:SUBLANES + t_in, :] = u
    cw = cw_ref[...]
    conv = cb_ref[...] + cw[tail:tail + 1] * u
    for j in range(tail):
        conv = conv + cw[j:j + 1] * ext_s[SUBLANES - tail + j:SUBLANES - tail + j + t_in, :]
    ext_s[0:SUBLANES, :] = ext_s[t_in:t_in + SUBLANES, :]
    xc_all = _silu(conv)
    dt_all = jax.nn.softplus(dt_ref[...] + dtb_ref[...])
    z_all = z_ref[...].astype(F32)
    if t_in < T:
        xc_all = jnp.concatenate([xc_all, jnp.zeros((T - t_in, CONV_CH), F32)], axis=0)
        dt_all = jnp.concatenate([dt_all, jnp.zeros((T - t_in, LANES), F32)], axis=0)
        z_all = jnp.concatenate([z_all, jnp.zeros((T - t_in, SSM_INNER), F32)], axis=0)

    a = -jnp.exp(alog_ref[...])
    row = lax.broadcasted_iota(jnp.int32, (T, T), 0)
    col = lax.broadcasted_iota(jnp.int32, (T, T), 1)
    causal = row >= col
    tri = causal.astype(F32)
    expand = exp_ref[...]
    lane = lax.broadcasted_iota(jnp.int32, (T, LANES), 1)
    heads_per_group = SSM_HEADS // SSM_GROUPS
    nw = nw_ref[...]
    for ci in range(max(1, t_in // T)):
        rows = slice(ci * T, (ci + 1) * T)
        xc, dt, z = xc_all[rows], dt_all[rows], z_all[rows]
        acs = _dot_exact(tri, dt * a, exact_lhs=True)
        acs_t = acs.T
        both_x = _dot_exact(jnp.concatenate([dt, acs], axis=0), expand)
        dt_x, acs_x = both_x[:T], both_x[T:]
        last = acs_x[T - 1:T, :]
        decay_out = jnp.exp(acs_x)
        decay_end = jnp.exp(last - acs_x)
        chunk_decay = jnp.exp(last)

        xs = xc[:, :SSM_INNER]
        xdt = xs * dt_x
        xdt_b = xdt.astype(BF16)
        xw_b = (xdt * decay_end).astype(BF16)
        state = st_s[...]
        state_b = state.astype(BF16)
        ys = []
        for g in range(SSM_GROUPS):
            b0 = SSM_INNER + g * SSM_STATE
            c0 = SSM_INNER + SSM_GROUPS * SSM_STATE + g * SSM_STATE
            bg_t = xc[:, b0:b0 + SSM_STATE].T.astype(BF16)
            cg = xc[:, c0:c0 + SSM_STATE].astype(BF16)
            gl = slice(g * GROUP_LANES, (g + 1) * GROUP_LANES)
            cb = _dot(cg, bg_t)
            y_off = _dot(cg, state_b[:, gl]) * decay_out[:, gl]
            st_s[:, gl] = state[:, gl] * chunk_decay[:, gl] + _dot(bg_t, xw_b[:, gl])
            for pair in range(heads_per_group // 2):
                h0 = g * heads_per_group + 2 * pair
                xp = xdt_b[:, h0 * SSM_HEAD_DIM:(h0 + 2) * SSM_HEAD_DIM]
                outs = []
                for hh in (h0, h0 + 1):
                    seg = acs[:, hh:hh + 1] - acs_t[hh:hh + 1, :]
                    w = cb * jnp.exp(jnp.where(causal, seg, -jnp.inf))
                    outs.append(_dot(w.astype(BF16), xp))
                ys.append(jnp.where(lane < SSM_HEAD_DIM, outs[0], outs[1])
                          + y_off[:, 2 * pair * SSM_HEAD_DIM:(2 * pair + 2) * SSM_HEAD_DIM])
        y = jnp.concatenate(ys, axis=1) + dsk_ref[...] * xs
        y = y * _silu(z)
        parts = []
        for g in range(SSM_GROUPS):
            gl = slice(g * GROUP_LANES, (g + 1) * GROUP_LANES)
            parts.append(_rms(y[:, gl], nw[:, gl]))
        y = jnp.concatenate(parts, axis=1).astype(y_ref.dtype)
        if t_in < T:
            y_ref[...] = y[:t_in]
        else:
            y_ref[rows, :] = y

    @pl.when(c == n_chunks - 1)
    def _():
        cout_ref[...] = ext_s[SUBLANES - tail:SUBLANES, :]
        hout_ref[...] = st_s[...].T.reshape(SSM_HEADS, SSM_HEAD_DIM, SSM_STATE)


def _ssd(xbc, z, dt_raw, p, conv_init, h0, batch, seq, y_dtype):
    t_in = _tile(seq, SSD_CHUNK * SSD_CHUNKS_PER_STEP)
    n_chunks = seq // t_in
    assert t_in % SSD_CHUNK == 0 or n_chunks == 1
    tail = CONV_WIDTH - 1
    rows = lambda w: pl.BlockSpec((t_in, w), lambda b, c: (b * n_chunks + c, 0))
    per_b3 = lambda s: pl.BlockSpec((None,) + s, lambda b, c: (b,) + (0,) * len(s))
    consts = [p['conv_w'], p['conv_b'], p['dt_bias'], p['a_log'], p['d_skip'], p['ssm_norm_w'], p['expand']]
    return pl.pallas_call(
        functools.partial(_ssd_kernel, t_in=t_in, n_chunks=n_chunks),
        grid=(batch, n_chunks),
        in_specs=[rows(CONV_CH), rows(SSM_INNER), rows(LANES)]
                 + [pl.BlockSpec(a.shape, lambda b, c: (0, 0)) for a in consts]
                 + [per_b3((tail, CONV_CH)), per_b3((SSM_HEADS, SSM_HEAD_DIM, SSM_STATE))],
        out_specs=[rows(SSM_INNER), per_b3((tail, CONV_CH)), per_b3((SSM_HEADS, SSM_HEAD_DIM, SSM_STATE))],
        out_shape=[jax.ShapeDtypeStruct((batch * seq, SSM_INNER), y_dtype),
                   jax.ShapeDtypeStruct((batch, tail, CONV_CH), F32),
                   jax.ShapeDtypeStruct((batch, SSM_HEADS, SSM_HEAD_DIM, SSM_STATE), F32)],
        scratch_shapes=[pltpu.VMEM((SUBLANES + t_in, CONV_CH), F32), pltpu.VMEM((SSM_STATE, SSM_INNER), F32)],
        compiler_params=_cparams(("parallel", "arbitrary"), VMEM_LIMIT_SMALL),
        name="ssd",
    )(xbc, z, dt_raw, *consts, conv_init, h0)


def _pool_kernel(x_ref, nw_ref, init_ref, pw_ref, ps_ref, o_ref, pout_ref, ext_s, *, tp, n_tiles, pos0):
    t = pl.program_id(1)
    halo = POOL_HALO

    @pl.when(t == 0)
    def _():
        ext_s[0:halo, :] = jnp.zeros((halo, ext_s.shape[1]), F32)
        ext_s[halo - POOL_STATE_LEN:halo, :] = init_ref[...]

    x = x_ref[...]
    u = _rms(x, nw_ref[...])
    ext_s[halo:halo + tp, :] = u
    e = ext_s[...]
    gc = POOL_GROUP_CH
    sums = [e]
    for lvl in range(len(POOL_WINDOWS)):
        prev = sums[-1][:, gc:] if lvl else sums[-1]
        sums.append(prev + pltpu.roll(prev, 1 << lvl, 0))
    pos = pos0 + t * tp + lax.broadcasted_iota(jnp.int32, (tp, 1), 0)
    mixed = []
    for g, w in enumerate(POOL_WINDOWS):
        cnt = jnp.minimum(w, pos + 1).astype(F32)
        win = sums[g + 1][halo:, :gc]
        pooled = win / cnt - u[:, g * gc:(g + 1) * gc]
        mixed.append(_dot(pooled.astype(BF16), pw_ref[g]))
    o_ref[...] = x + jnp.concatenate(mixed, axis=1) * ps_ref[...]

    @pl.when(t == n_tiles - 1)
    def _():
        pout_ref[...] = ext_s[tp + halo - POOL_STATE_LEN:tp + halo, :]

    ext_s[0:halo, :] = ext_s[tp:tp + halo, :]


def _pool(x, nw, pool_init, pool_w, w_idx, pool_scale, batch, seq, pos0):
    d = x.shape[1]
    tp = _tile(seq, TOKEN_TILE)
    n_tiles = seq // tp
    rows = pl.BlockSpec((tp, d), lambda b, t: (b * n_tiles + t, 0))
    state = pl.BlockSpec((None, POOL_STATE_LEN, d), lambda b, t: (b, 0, 0))
    return pl.pallas_call(
        functools.partial(_pool_kernel, tp=tp, n_tiles=n_tiles, pos0=pos0),
        grid=(batch, n_tiles),
        in_specs=[rows, pl.BlockSpec(nw.shape, lambda b, t: (0, 0)), state,
                  _stacked_spec(pool_w, (w_idx,)), pl.BlockSpec(pool_scale.shape, lambda b, t: (0, 0))],
        out_specs=[rows, state],
        out_shape=[jax.ShapeDtypeStruct(x.shape, F32), jax.ShapeDtypeStruct((batch, POOL_STATE_LEN, d), F32)],
        scratch_shapes=[pltpu.VMEM((POOL_HALO + tp, d), F32)],
        compiler_params=_cparams(("parallel", "arbitrary"), VMEM_LIMIT_SMALL),
        name="pool",
    )(x, nw, pool_init, pool_w, pool_scale)


def _prep_params(norm_w, ffn_w_gate, ffn_w_up, ffn_w_down, ab_w_in, ab_w_out, ab_conv_w, ab_conv_b, ab_dt_bias,
                 ab_a_log, ab_d_skip, ab_ssm_norm_w, ab_lambda_q1, ab_lambda_k1, ab_lambda_q2, ab_lambda_k2,
                 ab_subln_w, pool_w, pool_scale, final_norm_w):
    def lane_pad(v):
        return jnp.pad(v, ((0, 0), (0, LANES - v.shape[1])))[:, None, :]

    expand = np.zeros((LANES, SSM_INNER), np.float32)
    for h in range(SSM_HEADS):
        expand[h, h * SSM_HEAD_DIM:(h + 1) * SSM_HEAD_DIM] = 1.0
    return dict(
        norm_w=norm_w[:, :, None, :],
        wg=ffn_w_gate.astype(BF16), wu=ffn_w_up.astype(BF16), wd=ffn_w_down.astype(BF16),
        w_in=ab_w_in.astype(BF16),
        w_dt=jnp.pad(ab_w_in[:, :, MAIN_PROJ:], ((0, 0), (0, 0), (0, LANES - SSM_HEADS))).astype(BF16),
        w_out=ab_w_out.astype(BF16),
        conv_w=ab_conv_w, conv_b=ab_conv_b[:, None, :],
        dt_bias=lane_pad(ab_dt_bias), a_log=lane_pad(ab_a_log),
        d_skip=jnp.repeat(ab_d_skip, SSM_HEAD_DIM, axis=1)[:, None, :],
        ssm_norm_w=ab_ssm_norm_w[:, None, :],
        lam=jnp.stack([ab_lambda_q1, ab_lambda_k1, ab_lambda_q2, ab_lambda_k2], axis=1),
        subln_w=ab_subln_w[:, None, :],
        pool_w=pool_w.astype(BF16), pool_scale=pool_scale[:, None, :],
        final_norm_w=final_norm_w[None, :],
        expand=jnp.asarray(expand),
    )


def _ssm_params(p, i):
    return dict(conv_w=p['conv_w'][i], conv_b=p['conv_b'][i], dt_bias=p['dt_bias'][i], a_log=p['a_log'][i],
                d_skip=p['d_skip'][i], ssm_norm_w=p['ssm_norm_w'][i], expand=p['expand'])


def _lambda_init(layer):
    return 0.8 - 0.6 * math.exp(-0.3 * layer)


def kernel(x_prompt, x_sample, cache_k, cache_v, state_conv, state_ssm, state_pool, page_table, norm_w, ffn_w_gate, ffn_w_up, ffn_w_down, ab_w_in, ab_w_out, ab_conv_w, ab_conv_b, ab_dt_bias, ab_a_log, ab_d_skip, ab_ssm_norm_w, ab_lambda_q1, ab_lambda_k1, ab_lambda_q2, ab_lambda_k2, ab_subln_w, pool_w, pool_scale, final_norm_w):
    p = _prep_params(norm_w, ffn_w_gate, ffn_w_up, ffn_w_down, ab_w_in, ab_w_out, ab_conv_w, ab_conv_b, ab_dt_bias,
                     ab_a_log, ab_d_skip, ab_ssm_norm_w, ab_lambda_q1, ab_lambda_k1, ab_lambda_q2, ab_lambda_k2,
                     ab_subln_w, pool_w, pool_scale, final_norm_w)
    depth = ffn_w_gate.shape[0]
    n_ab, n_pool = state_conv.shape[0], state_pool.shape[0]
    assert n_ab == 1 and depth == 2
    bp, lp, d = x_prompt.shape
    bs, ls, _ = x_sample.shape
    ffn_w = (p['wg'], p['wu'], p['wd'])
    nw = p['norm_w']
    lam, sub_w, li = p['lam'][0], p['subln_w'][0], _lambda_init(0)
    past_len = page_table.shape[1] * cache_k.shape[2]
    n_hosts = 2 * depth
    host_seqs = bs // n_hosts
    assert host_seqs * n_hosts == bs
    paged = (cache_k, cache_v, page_table, 0)

    xs = x_sample.reshape(bs * ls, d)
    cos_s, sin_s = _rope_tables(bs * ls, past_len, ls)
    xs = _ffn(xs, nw[0][0], *ffn_w, (0, 0))
    qs, ks, vs, zs, xbc_s, dt_s = _inproj(xs, nw[0][1], p['w_in'], 0, p['w_dt'][0], cos_s, sin_s, prompt_layout=False)
    stats = []

    def host(x, norm, w_idx, **kw):
        x, *st = _ffn_host(x, norm, *ffn_w, w_idx, paged, qs, len(stats) * host_seqs, host_seqs, **kw)
        stats.append(st)
        return x

    zeros = lambda shape: jnp.zeros(shape, F32)
    xp = x_prompt.reshape(bp * lp, d)
    cos_p, sin_p = _rope_tables(lp, 0, lp)
    xp = host(xp, nw[0][0], (0, 0))
    qt, kt, kb, v4, vt, zp, xbc_p, dt_p = _inproj(xp, nw[0][1], p['w_in'], 0, p['w_dt'][0], cos_p, sin_p,
                                                  prompt_layout=True)
    attn_p = _attn_prompt(qt, kb, vt, lam, sub_w, li)
    y_p, conv_p, ssm_p = _ssd(xbc_p, zp, dt_p, _ssm_params(p, 0), zeros((bp, CONV_WIDTH - 1, CONV_CH)),
                              zeros((bp, SSM_HEADS, SSM_HEAD_DIM, SSM_STATE)), bp, lp, BF16)
    xp = host(xp, nw[0][2], (0, 1), mix=(attn_p, y_p, p['w_out'], 0))
    xp = host(xp, nw[1][0], (1, 0))
    xp, pool_p = _pool(xp, nw[1][1], zeros((bp, POOL_STATE_LEN, d)), p['pool_w'], 0, p['pool_scale'][0], bp, lp, 0)
    yp = host(xp, nw[1][2], (1, 1), final_w=p['final_norm_w'])
    k_p = jnp.transpose(kt.reshape(bp, DIFF_HEADS, 2, DIFF_HEAD_DIM, lp), (0, 4, 1, 2, 3))

    m_past, l_past, acc_past = (jnp.concatenate([st[j] for st in stats], axis=0) for j in range(3))
    attn_s = _attn_merge(qs, ks, vs, m_past, l_past, acc_past, bs, lam, sub_w, li)
    y_s, conv_s, ssm_s = _ssd(xbc_s, zs, dt_s, _ssm_params(p, 0), state_conv[0], state_ssm[0], bs, ls, F32)
    xs = _ffn(xs, nw[0][2], *ffn_w, (0, 1), mix=(attn_s, y_s, p['w_out'], 0))
    xs = _ffn(xs, nw[1][0], *ffn_w, (1, 0))
    xs, pool_s = _pool(xs, nw[1][1], state_pool[0], p['pool_w'], 0, p['pool_scale'][0], bs, ls, past_len)
    ys = _ffn(xs, nw[1][2], *ffn_w, (1, 1), final_w=p['final_norm_w'])

    lead = lambda t: t[None]
    return (yp.reshape(bp, lp, d), ys.reshape(bs, ls, d),
            lead(k_p.reshape(bp, lp, DIFF_HEADS, 2, DIFF_HEAD_DIM)), lead(v4.reshape(bp, lp, DIFF_HEADS, HEAD_LANES)),
            lead(conv_p), lead(ssm_p), lead(pool_p),
            lead(ks.reshape(bs, ls, DIFF_HEADS, 2, DIFF_HEAD_DIM)), lead(vs.reshape(bs, ls, DIFF_HEADS, HEAD_LANES)),
            lead(conv_s), lead(ssm_s), lead(pool_s))
```
